```python
import math
import jax, jax.numpy as jnp
from jax import lax
import numpy as np

D_MODEL = 1024
BATCH = 1
SEQ = 16384
DEPTH = 1
DEC_BATCH = 4
DEC_SEQ = 4096
PAST_LEN = 128

N_HEADS = 8
N_KV_HEADS = 2
HEAD_DIM = 64
Q_GROUP = N_HEADS // N_KV_HEADS
ATT_W = N_HEADS * HEAD_DIM
KV_W = N_KV_HEADS * HEAD_DIM
WINDOW = 128
BLOCK = 128
ROPE_THETA = 500000.0
ROT_DIM = HEAD_DIM // 4
SGU_W = D_MODEL // 2
SGU_GROUPS = 4
SGU_GROUP_DIM = SGU_W // SGU_GROUPS
CHUNK = 128
SPLIT_IDX = [ATT_W,
             ATT_W + KV_W,
             ATT_W + 2 * KV_W,
             ATT_W + 2 * KV_W + SGU_W,
             ATT_W + 2 * KV_W + 2 * SGU_W,
             ATT_W + 2 * KV_W + 2 * SGU_W + D_MODEL]
IN_COLS = ATT_W + 2 * KV_W + 2 * SGU_W + 2 * D_MODEL
N_EXPERTS = 32
TOP_K = 4
D_FF = D_MODEL
SWIGLU_LIMIT = 7.0
SWIGLU_ALPHA = 1.702
MOE_BLOCK = 128
LN_EPS = 1e-5
DN_ALPHA = (2 * DEPTH) ** 0.25
DN_BETA = (8 * DEPTH) ** -0.25

kernel_name = "hybrid_window_sgu_moe_encoder"


def layer_norm(x, g, b):
    xf = x.astype(jnp.float32)
    mu = xf.mean(-1, keepdims=True)
    var = jnp.square(xf - mu).mean(-1, keepdims=True)
    return ((xf - mu) * lax.rsqrt(var + LN_EPS)).astype(x.dtype) * g + b


def partial_rope(x, seq_len):
    half = ROT_DIM // 2
    inv = ROPE_THETA ** (-jnp.arange(half, dtype=jnp.float32) * 2.0 / ROT_DIM)
    ang = jnp.arange(seq_len, dtype=jnp.float32)[:, None] * inv[None, :]
    cos = jnp.cos(ang)[None, :, None, :]
    sin = jnp.sin(ang)[None, :, None, :]
    xr = x[..., :ROT_DIM].astype(jnp.float32)
    x1, x2 = xr[..., :half], xr[..., half:]
    rot = jnp.concatenate([x1 * cos - x2 * sin, x2 * cos + x1 * sin], axis=-1).astype(x.dtype)
    return jnp.concatenate([rot, x[..., ROT_DIM:]], axis=-1)


def banded_sink_attention(q, k, v, sink):
    B, S = q.shape[0], q.shape[1]
    nb = S // BLOCK
    qb = q.reshape(B, nb, BLOCK, N_KV_HEADS, Q_GROUP, HEAD_DIM)

    def windows(t):
        tp = jnp.pad(t, ((0, 0), (BLOCK, BLOCK), (0, 0), (0, 0)))
        tp = tp.reshape(B, nb + 2, BLOCK, N_KV_HEADS, HEAD_DIM)
        return jnp.concatenate([tp[:, :-2], tp[:, 1:-1], tp[:, 2:]], axis=2)

    kw, vw = windows(k), windows(v)
    s = jnp.einsum('bnqhgd,bnkhd->bnhgqk', qb, kw,
                   preferred_element_type=jnp.float32) * (HEAD_DIM ** -0.5)
    qi = jnp.arange(BLOCK)[:, None]
    ki = jnp.arange(3 * BLOCK)[None, :]
    rel = ki - BLOCK - qi
    kpos = (jnp.arange(nb)[:, None, None] - 1) * BLOCK + ki[None]
    valid = (jnp.abs(rel)[None] <= WINDOW) & (kpos >= 0) & (kpos < S)
    s = jnp.where(valid[None, :, None, None], s, -jnp.inf)
    sk = sink.astype(jnp.float32).reshape(1, 1, N_KV_HEADS, Q_GROUP, 1, 1)
    m = jnp.maximum(s.max(-1, keepdims=True), sk)
    p = jnp.exp(s - m)
    denom = p.sum(-1, keepdims=True) + jnp.exp(sk - m)
    p = (p / denom).astype(v.dtype)
    o = jnp.einsum('bnhgqk,bnkhd->bnqhgd', p, vw)
    return o.reshape(B, S, ATT_W)


def spatial_gating(u, v, ln_g, ln_b, w_sp, b_sp):
    B, S = v.shape[0], v.shape[1]
    u = jax.nn.gelu(u)
    v = layer_norm(jax.nn.gelu(v), ln_g, ln_b)
    vc = v.reshape(B, S // CHUNK, CHUNK, SGU_GROUPS, SGU_GROUP_DIM)
    sv = jnp.einsum('gts,bnsgc->bntgc', w_sp, vc) + b_sp.T[None, None, :, :, None]
    return u * sv.reshape(B, S, SGU_W)


def clamped_swiglu(hu):
    gate, lin = hu[..., :D_FF], hu[..., D_FF:]
    gate = jnp.minimum(gate, SWIGLU_LIMIT)
    lin = jnp.clip(lin, -SWIGLU_LIMIT, SWIGLU_LIMIT)
    return gate * jax.nn.sigmoid(SWIGLU_ALPHA * gate) * (lin + 1.0)


def moe(xt, w_router, b_router, w_up, b_up, w_down, b_down):
    T, D = xt.shape
    logits = (xt @ w_router + b_router).astype(jnp.float32)
    top_val, top_idx = lax.top_k(logits, TOP_K)
    top_w = jax.nn.softmax(top_val, axis=-1).astype(xt.dtype)
    M = T * TOP_K
    flat_e = top_idx.reshape(M).astype(jnp.int32)
    flat_tok = jnp.arange(M, dtype=jnp.int32) // TOP_K
    se, order = lax.sort((flat_e, jnp.arange(M, dtype=jnp.int32)), num_keys=1, is_stable=True)
    stok = flat_tok[order]
    sw = top_w.reshape(M)[order]
    counts = jnp.zeros((N_EXPERTS,), jnp.int32).at[flat_e].add(1)
    padded = (counts + MOE_BLOCK - 1) // MOE_BLOCK * MOE_BLOCK
    pad_end = jnp.cumsum(padded)
    pad_start = pad_end - padded
    start = jnp.cumsum(counts) - counts
    dest = pad_start[se] + jnp.arange(M, dtype=jnp.int32) - start[se]
    nblk = -(-M // MOE_BLOCK) + N_EXPERTS
    R = nblk * MOE_BLOCK
    row_tok = jnp.full((R,), T, jnp.int32).at[dest].set(stok)
    row_w = jnp.zeros((R,), xt.dtype).at[dest].set(sw)
    blk_start = jnp.arange(nblk, dtype=jnp.int32) * MOE_BLOCK
    blk_e = jnp.minimum(jnp.searchsorted(pad_end, blk_start, side='right'), N_EXPERTS - 1).astype(jnp.int32)
    xpad = jnp.concatenate([xt, jnp.zeros((1, D), xt.dtype)], axis=0)
    xin = xpad[row_tok].reshape(nblk, MOE_BLOCK, D)

    def expert_block(args):
        xb, e = args
        hu = xb @ w_up[e] + b_up[e]
        return clamped_swiglu(hu) @ w_down[e] + b_down[e]

    out = lax.map(expert_block, (xin, blk_e))
    y = jnp.zeros((T + 1, D), xt.dtype).at[row_tok].add(out.reshape(R, D) * row_w[:, None])
    return y[:T]


def encoder_layer(x, c, w_ada, b_ada, w_in, b_in, sink, sgu_ln_g, sgu_ln_b, w_spatial, b_spatial,
                  w_br_attn, w_br_sgu, w_out, ln1_g, ln1_b, w_router, b_router,
                  w_up, b_up, w_down, b_down, ln2_g, ln2_b):
    B, S, D = x.shape
    mod = jax.nn.silu(c) @ w_ada + b_ada
    sh1, sc1, g1, sh2, sc2, g2 = jnp.split(mod[:, None, :], 6, axis=-1)
    h = x * (1.0 + sc1) + sh1
    z = h @ w_in + b_in
    q, k, v, u, vs, ga, gs = jnp.split(z, SPLIT_IDX, axis=-1)
    q = partial_rope(q.reshape(B, S, N_HEADS, HEAD_DIM), S)
    k = partial_rope(k.reshape(B, S, N_KV_HEADS, HEAD_DIM), S)
    v = v.reshape(B, S, N_KV_HEADS, HEAD_DIM)
    attn = banded_sink_attention(q, k, v, sink)
    sgu = spatial_gating(u, vs, sgu_ln_g, sgu_ln_b, w_spatial, b_spatial)
    merged = jax.nn.sigmoid(ga) * (attn @ w_br_attn) + jax.nn.sigmoid(gs) * (sgu @ w_br_sgu)
    mix = merged @ w_out
    x = layer_norm(DN_ALPHA * x + g1 * mix, ln1_g, ln1_b)
    h2 = x * (1.0 + sc2) + sh2
    ffn = moe(h2.reshape(B * S, D), w_router, b_router, w_up, b_up, w_down, b_down).reshape(B, S, D)
    return layer_norm(DN_ALPHA * x + g2 * ffn, ln2_g, ln2_b)


def setup_inputs(seed: int = 0) -> dict:
    key = jax.random.key(seed)
    ks = jax.random.split(key, 32)
    nrm = lambda k, shape, s: jax.random.normal(k, shape, jnp.float32) * s
    L, D = DEPTH, D_MODEL
    return {
        "x_prompt": nrm(ks[0], (BATCH, SEQ, D), 1.0),
        "x_sample": nrm(ks[1], (DEC_BATCH, DEC_SEQ, D), 1.0),
        "c_prompt": nrm(ks[2], (BATCH, D), 1.0),
        "c_sample": nrm(ks[3], (DEC_BATCH, D), 1.0),
        "w_ada": nrm(ks[4], (L, D, 6 * D), 0.5 * D ** -0.5),
        "b_ada": nrm(ks[5], (L, 6 * D), 0.02),
        "w_in": nrm(ks[6], (L, D, IN_COLS), D ** -0.5),
        "b_in": nrm(ks[7], (L, IN_COLS), 0.02),
        "sink": nrm(ks[8], (L, N_HEADS), 0.5),
        "sgu_ln_g": 1.0 + nrm(ks[9], (L, SGU_W), 0.02),
        "sgu_ln_b": nrm(ks[10], (L, SGU_W), 0.02),
        "w_spatial": nrm(ks[11], (L, SGU_GROUPS, CHUNK, CHUNK), CHUNK ** -0.5),
        "b_spatial": 1.0 + nrm(ks[12], (L, SGU_GROUPS, CHUNK), 0.02),
        "w_br_attn": nrm(ks[13], (L, ATT_W, D), DN_BETA * ATT_W ** -0.5),
        "w_br_sgu": nrm(ks[14], (L, SGU_W, D), DN_BETA * SGU_W ** -0.5),
        "w_out": nrm(ks[15], (L, D, D), DN_BETA * D ** -0.5),
        "ln1_g": 1.0 + nrm(ks[16], (L, D), 0.02),
        "ln1_b": nrm(ks[17], (L, D), 0.02),
        "w_router": nrm(ks[18], (L, D, N_EXPERTS), D ** -0.5),
        "b_router": nrm(ks[19], (L, N_EXPERTS), 0.01),
        "w_up": nrm(ks[20], (L, N_EXPERTS, D, 2 * D_FF), D ** -0.5),
        "b_up": nrm(ks[21], (L, N_EXPERTS, 2 * D_FF), 0.02),
        "w_down": nrm(ks[22], (L, N_EXPERTS, D_FF, D), DN_BETA * D_FF ** -0.5),
        "b_down": nrm(ks[23], (L, N_EXPERTS, D), 0.02),
        "ln2_g": 1.0 + nrm(ks[24], (L, D), 0.02),
        "ln2_b": nrm(ks[25], (L, D), 0.02),
    }


def reference(x_prompt, x_sample, c_prompt, c_sample, w_ada, b_ada, w_in, b_in, sink,
              sgu_ln_g, sgu_ln_b, w_spatial, b_spatial, w_br_attn, w_br_sgu, w_out,
              ln1_g, ln1_b, w_router, b_router, w_up, b_up, w_down, b_down, ln2_g, ln2_b):
    y_prompt = x_prompt
    y_sample = x_sample
    for l in range(DEPTH):
        params = (w_ada[l], b_ada[l], w_in[l], b_in[l], sink[l], sgu_ln_g[l], sgu_ln_b[l],
                  w_spatial[l], b_spatial[l], w_br_attn[l], w_br_sgu[l], w_out[l],
                  ln1_g[l], ln1_b[l], w_router[l], b_router[l], w_up[l], b_up[l],
                  w_down[l], b_down[l], ln2_g[l], ln2_b[l])
        y_prompt = encoder_layer(y_prompt, c_prompt, *params)
        y_sample = encoder_layer(y_sample, c_sample, *params)
    return (y_prompt, y_sample)
```

```python
import functools
import math
from typing import NamedTuple

import jax
import jax.numpy as jnp
from jax import lax
from jax.experimental import pallas as pl
from jax.experimental.pallas import tpu as pltpu

F32 = jnp.float32
BF16 = jnp.bfloat16
I32 = jnp.int32

D_MODEL = 1024
N_HEADS = 8
N_KV_HEADS = 2
HEAD_DIM = 64
ATT_W = N_HEADS * HEAD_DIM
KV_W = N_KV_HEADS * HEAD_DIM
BLOCK = 128
ROPE_THETA = 500000.0
ROT_DIM = HEAD_DIM // 4
ROT_HALF = ROT_DIM // 2
SGU_W = D_MODEL // 2
SGU_GROUPS = 4
N_EXPERTS = 32
TOP_K = 4
D_FF = D_MODEL
SWIGLU_LIMIT = 7.0
SWIGLU_ALPHA = 1.702
LN_EPS = 1e-5
DEPTH = 1
DN_ALPHA = (2 * DEPTH) ** 0.25

LANES = 128
SUBLANES = 8
ROW_TILES = D_MODEL // LANES
VMEM_LIMIT_BYTES = 56 * 1024 * 1024

TM = 256
TK = 512
BM = 256
DISP_CHUNK = 32
COMB_CHUNK = 16
REGION_SLACK = DISP_CHUNK
DISP_ROWS = TM * TOP_K + DISP_CHUNK
COMB_ROWS = ((TM * TOP_K + N_EXPERTS * (COMB_CHUNK - 1)) + 255) // 256 * 256
NEG_INF = float("-inf")


class _Seqs(NamedTuple):
    n_prompt: int
    prompt_len: int
    sample_len: int
    n_tokens: int


def _tile_pos(seqs, t0):
    is_s = t0 >= seqs.n_prompt
    seq_len = jnp.where(is_s, seqs.sample_len, seqs.prompt_len)
    off = jnp.where(is_s, t0 - seqs.n_prompt, t0)
    pos0 = off % seq_len
    row = jnp.where(is_s, seqs.n_prompt // seqs.prompt_len + off // seq_len, off // seq_len)
    return seq_len, pos0, row


def _ada_kernel(c_ref, w_ref, b_ref, o_ref):
    c = c_ref[...]
    a = c * jax.nn.sigmoid(c)
    o_ref[...] = jnp.dot(a, w_ref[...], preferred_element_type=F32,
                         precision=lax.Precision.HIGHEST) + b_ref[...]


def _ada(c_pad, w_ada, b_ada):
    n = w_ada.shape[1]
    bn = 1536
    return pl.pallas_call(
        _ada_kernel,
        grid=(n // bn,),
        in_specs=[pl.BlockSpec((8, D_MODEL), lambda j: (0, 0)),
                  pl.BlockSpec((D_MODEL, bn), lambda j: (0, j)),
                  pl.BlockSpec((1, bn), lambda j: (0, j))],
        out_specs=pl.BlockSpec((8, bn), lambda j: (0, j)),
        out_shape=jax.ShapeDtypeStruct((8, n), F32),
        compiler_params=pltpu.CompilerParams(vmem_limit_bytes=VMEM_LIMIT_BYTES),
        name="ada",
    )(c_pad, w_ada, b_ada.reshape(1, n))


def _rope_tables(length):
    inv = ROPE_THETA ** (-jnp.arange(ROT_HALF, dtype=F32) * 2.0 / ROT_DIM)
    ang = jnp.arange(length, dtype=F32)[:, None] * inv[None, :]
    cos, sin = jnp.cos(ang), jnp.sin(ang)
    ones = jnp.ones((length, HEAD_DIM - ROT_DIM), F32)
    cos64 = jnp.concatenate([cos, cos, ones], axis=1)
    sin64 = jnp.concatenate([-sin, sin, 0.0 * ones], axis=1)
    return jnp.concatenate([cos64, cos64], axis=1), jnp.concatenate([sin64, sin64], axis=1)


def _rope(x, cos, sin):
    n = x.shape[1]
    reps = n // LANES
    c = jnp.concatenate([cos] * reps, axis=1)
    s = jnp.concatenate([sin] * reps, axis=1)
    lane = lax.broadcasted_iota(I32, x.shape, 1)
    first = (lane & (HEAD_DIM - 1)) < ROT_HALF
    partner = jnp.where(first, pltpu.roll(x, n - ROT_HALF, 1), pltpu.roll(x, ROT_HALF, 1))
    return x * c + partner * s


def _kv_kernel(x_ref, mod_ref, cos_ref, sin_ref, w_ref, b_ref, kt_ref, v_ref):
    mod = mod_ref[0]
    sh1 = mod[:, 0:D_MODEL]
    sc1 = mod[:, D_MODEL:2 * D_MODEL]
    h = (x_ref[...] * (1.0 + sc1) + sh1).astype(BF16)
    kv = jnp.dot(h, w_ref[...], preferred_element_type=F32) + b_ref[...]
    k = _rope(kv[:, 0:2 * LANES], cos_ref[...], sin_ref[...])
    kt_ref[...] = k.T.astype(BF16)
    v_ref[...] = kv[:, 2 * LANES:4 * LANES].astype(BF16)


def _kv(seqs, x_all, mod3, cos_t, sin_t, w_kv, b_kv):
    T = seqs.n_tokens

    def mod_map(i):
        return (_tile_pos(seqs, i * TK)[2], 0, 0)

    def rope_map(i):
        return (_tile_pos(seqs, i * TK)[1] // TK, 0)

    return pl.pallas_call(
        _kv_kernel,
        grid=(T // TK,),
        in_specs=[pl.BlockSpec((TK, D_MODEL), lambda i: (i, 0)),
                  pl.BlockSpec((1, 1, 6 * D_MODEL), mod_map),
                  pl.BlockSpec((TK, LANES), rope_map),
                  pl.BlockSpec((TK, LANES), rope_map),
                  pl.BlockSpec((D_MODEL, 4 * LANES), lambda i: (0, 0)),
                  pl.BlockSpec((1, 4 * LANES), lambda i: (0, 0))],
        out_specs=[pl.BlockSpec((2 * LANES, TK), lambda i: (0, i)),
                   pl.BlockSpec((TK, 2 * LANES), lambda i: (i, 0))],
        out_shape=[jax.ShapeDtypeStruct((2 * LANES, T), BF16),
                   jax.ShapeDtypeStruct((T, 2 * LANES), BF16)],
        compiler_params=pltpu.CompilerParams(dimension_semantics=("arbitrary",),
                                             vmem_limit_bytes=VMEM_LIMIT_BYTES),
        name="kv",
    )(x_all, mod3, cos_t, sin_t, w_kv, b_kv)


def _layer_norm(x, g, b):
    mu = jnp.mean(x, axis=-1, keepdims=True)
    xc = x - mu
    var = jnp.mean(xc * xc, axis=-1, keepdims=True)
    return xc * lax.rsqrt(var + LN_EPS) * g + b


def _attention_block(q_blk, kwin, vwin, valid, sink_ref, h):
    lane = lax.broadcasted_iota(I32, (BLOCK, LANES), 1)
    lo = lane < HEAD_DIM
    parts, sinks = [], []
    for p in range(2):
        qp = q_blk[:, p * LANES:(p + 1) * LANES]
        parts.append(jnp.where(lo, qp, 0.0).astype(BF16))
        parts.append(jnp.where(lo, 0.0, qp).astype(BF16))
    for g in range(4):
        sinks.append(jnp.full((BLOCK, 1), sink_ref[h * 4 + g], F32))
    qm = jnp.concatenate(parts, axis=0)
    sk = jnp.concatenate(sinks, axis=0)
    s = jnp.dot(qm, kwin, preferred_element_type=F32)
    vmask = jnp.concatenate([valid] * 4, axis=0)
    s = jnp.where(vmask, s, NEG_INF)
    m = jnp.maximum(jnp.max(s, axis=-1, keepdims=True), sk)
    p = jnp.exp(s - m)
    denom = jnp.sum(p, axis=-1, keepdims=True) + jnp.exp(sk - m)
    pn = (p / denom).astype(BF16)
    o = jnp.dot(pn, vwin, preferred_element_type=F32)
    outs = []
    for p2 in range(2):
        outs.append(jnp.where(lo, o[(2 * p2) * BLOCK:(2 * p2 + 1) * BLOCK],
                              o[(2 * p2 + 1) * BLOCK:(2 * p2 + 2) * BLOCK]))
    return jnp.concatenate(outs, axis=1)


def _mixer_kernel(seqs, sink_ref, x_ref, mod_ref, cos_ref, sin_ref,
                  ktp_ref, ktc_ref, ktn_ref, vp_ref, vc_ref, vn_ref,
                  wmix_ref, bmix_ref, wsp_ref, bspt_ref, sg_ref, sb_ref,
                  wba_ref, wbs_ref, wout_ref, l1g_ref, l1b_ref, wr_ref, br_ref,
                  x1_ref, h2_ref, rcol_ref, rrow_ref, cnt_ref):
    i = pl.program_id(0)
    seq_len, pos0, _ = _tile_pos(seqs, i * TM)
    mod = mod_ref[0]
    sh1, sc1, g1 = (mod[:, j * D_MODEL:(j + 1) * D_MODEL] for j in range(3))
    sh2, sc2 = (mod[:, j * D_MODEL:(j + 1) * D_MODEL] for j in range(3, 5))
    x = x_ref[...]
    h = (x * (1.0 + sc1) + sh1).astype(BF16)
    z = jnp.dot(h, wmix_ref[...], preferred_element_type=F32) + bmix_ref[...]
    q = _rope(z[:, 0:ATT_W], cos_ref[...], sin_ref[...]) * (HEAD_DIM ** -0.5)
    u = jax.nn.gelu(z[:, ATT_W:ATT_W + SGU_W])
    vs = _layer_norm(jax.nn.gelu(z[:, ATT_W + SGU_W:ATT_W + 2 * SGU_W]), sg_ref[...], sb_ref[...])
    ga = z[:, ATT_W + 2 * SGU_W:ATT_W + 2 * SGU_W + D_MODEL]
    gs = z[:, ATT_W + 2 * SGU_W + D_MODEL:]

    kfull = jnp.concatenate([ktp_ref[...], ktc_ref[...], ktn_ref[...]], axis=1)
    vfull = jnp.concatenate([vp_ref[...], vc_ref[...], vn_ref[...]], axis=0)
    qi = lax.broadcasted_iota(I32, (BLOCK, 3 * BLOCK), 0)
    ki = lax.broadcasted_iota(I32, (BLOCK, 3 * BLOCK), 1)
    band = (ki >= qi) & (ki <= qi + 2 * BLOCK)
    vs_b = vs.astype(BF16)
    attn_rows, sgu_rows = [], []
    for jb in range(TM // BLOCK):
        posb = pos0 + jb * BLOCK
        valid = band & (ki >= jnp.where(posb == 0, BLOCK, 0)) \
                     & (ki < jnp.where(posb + BLOCK == seq_len, 2 * BLOCK, 3 * BLOCK))
        heads = []
        for hk in range(N_KV_HEADS):
            kwin = kfull[hk * LANES:(hk + 1) * LANES, jb * BLOCK:(jb + 3) * BLOCK]
            vwin = vfull[jb * BLOCK:(jb + 3) * BLOCK, hk * LANES:(hk + 1) * LANES]
            q_blk = q[jb * BLOCK:(jb + 1) * BLOCK, hk * 2 * LANES:(hk + 1) * 2 * LANES]
            heads.append(_attention_block(q_blk, kwin, vwin, valid, sink_ref, hk))
        attn_rows.append(jnp.concatenate(heads, axis=1))
        groups = []
        for g in range(SGU_GROUPS):
            vg = vs_b[jb * BLOCK:(jb + 1) * BLOCK, g * LANES:(g + 1) * LANES]
            sv = jnp.dot(wsp_ref[g], vg, preferred_element_type=F32) + bspt_ref[:, g:g + 1]
            groups.append(sv)
        sgu_rows.append(jnp.concatenate(groups, axis=1))
    attn = jnp.concatenate(attn_rows, axis=0)
    sgu = u * jnp.concatenate(sgu_rows, axis=0)

    a1 = jnp.dot(attn.astype(BF16), wba_ref[...], preferred_element_type=F32)
    a2 = jnp.dot(sgu.astype(BF16), wbs_ref[...], preferred_element_type=F32)
    merged = jax.nn.sigmoid(ga) * a1 + jax.nn.sigmoid(gs) * a2
    mix = jnp.dot(merged.astype(BF16), wout_ref[...], preferred_element_type=F32)
    x1 = _layer_norm(DN_ALPHA * x + g1 * mix, l1g_ref[...], l1b_ref[...])
    x1_ref[...] = x1
    h2 = x1 * (1.0 + sc2) + sh2
    hi = h2.astype(BF16)
    h2_ref[...] = hi

    lo_part = (h2 - hi.astype(F32)).astype(BF16)
    l1 = jnp.dot(hi, wr_ref[...], preferred_element_type=F32)
    l2 = jnp.dot(lo_part, wr_ref[:, 0:N_EXPERTS], preferred_element_type=F32)
    logits = l1[:, 0:N_EXPERTS] + l1[:, N_EXPERTS:2 * N_EXPERTS] + l2 + br_ref[...]

    eidx = lax.broadcasted_iota(I32, (TM, N_EXPERTS), 1)
    work = logits
    idxs, vals = [], []
    for _ in range(TOP_K):
        m = jnp.max(work, axis=-1, keepdims=True)
        ix = jnp.min(jnp.where(work == m, eidx, N_EXPERTS), axis=-1, keepdims=True)
        idxs.append(ix)
        vals.append(m)
        work = jnp.where(eidx == ix, NEG_INF, work)
    exps = [jnp.exp(v - vals[0]) for v in vals]
    esum = exps[0] + exps[1] + exps[2] + exps[3]
    wts = [e / esum for e in exps]

    sel = jnp.zeros((TM, N_EXPERTS), F32)
    for ix in idxs:
        sel = sel + jnp.where(eidx == ix, 1.0, 0.0)
    sel_b = sel.astype(BF16)
    ti = lax.broadcasted_iota(I32, (TM, TM), 0)
    tj = lax.broadcasted_iota(I32, (TM, TM), 1)
    lower = jnp.where(tj < ti, 1.0, 0.0).astype(BF16)
    rank = jnp.dot(lower, sel_b, preferred_element_type=F32)
    cnt = jnp.sum(sel, axis=0, keepdims=True)
    nch = jnp.floor((cnt + (COMB_CHUNK - 1)) * (1.0 / COMB_CHUNK))
    ei = lax.broadcasted_iota(I32, (N_EXPERTS, N_EXPERTS), 0)
    ej = lax.broadcasted_iota(I32, (N_EXPERTS, N_EXPERTS), 1)
    before = jnp.where(ei < ej, 1.0, 0.0).astype(BF16)
    pre = jnp.concatenate([cnt, nch, jnp.zeros((6, N_EXPERTS), F32)], axis=0).astype(BF16)
    base = jnp.dot(pre, before, preferred_element_type=F32)
    dbase = base[0:1]
    cbase = base[1:2] * COMB_CHUNK

    cols = []
    for ix in idxs:
        cols.append(ix.astype(F32))
    for ix in idxs:
        hit = eidx == ix
        rk = jnp.sum(jnp.where(hit, rank, 0.0), axis=-1, keepdims=True)
        cols.append(rk + jnp.sum(jnp.where(hit, dbase, 0.0), axis=-1, keepdims=True))
    for ix in idxs:
        hit = eidx == ix
        rk = jnp.sum(jnp.where(hit, rank, 0.0), axis=-1, keepdims=True)
        cols.append(rk + jnp.sum(jnp.where(hit, cbase, 0.0), axis=-1, keepdims=True))
    cols.extend(wts)
    lane = lax.broadcasted_iota(I32, (TM, LANES), 1)
    rc = jnp.zeros((TM, LANES), F32)
    for j, c in enumerate(cols):
        rc = jnp.where(lane == j, c, rc)
    rcol_ref[...] = rc
    rrow_ref[0] = rc.T[0:16]
    cnt_ref[0] = jnp.broadcast_to(
        jnp.concatenate([cnt, jnp.zeros((1, LANES - N_EXPERTS), F32)], axis=1), (8, LANES)).astype(I32)


def _mixer(seqs, sink, x_all, mod3, cos_t, sin_t, kt, v, p):
    T = seqs.n_tokens
    nt = T // TM
    nb = T // BLOCK
    r = TM // BLOCK

    def mod_map(i, s):
        return (_tile_pos(seqs, i * TM)[2], 0, 0)

    def rope_map(i, s):
        return (_tile_pos(seqs, i * TM)[1] // TM, 0)

    const2 = lambda i, s: (0, 0)
    in_specs = [
        pl.BlockSpec((TM, D_MODEL), lambda i, s: (i, 0)),
        pl.BlockSpec((1, 1, 6 * D_MODEL), mod_map),
        pl.BlockSpec((TM, LANES), rope_map),
        pl.BlockSpec((TM, LANES), rope_map),
        pl.BlockSpec((2 * LANES, BLOCK), lambda i, s: (0, jnp.maximum(i * r - 1, 0))),
        pl.BlockSpec((2 * LANES, TM), lambda i, s: (0, i)),
        pl.BlockSpec((2 * LANES, BLOCK), lambda i, s: (0, jnp.minimum(i * r + r, nb - 1))),
        pl.BlockSpec((BLOCK, 2 * LANES), lambda i, s: (jnp.maximum(i * r - 1, 0), 0)),
        pl.BlockSpec((TM, 2 * LANES), lambda i, s: (i, 0)),
        pl.BlockSpec((BLOCK, 2 * LANES), lambda i, s: (jnp.minimum(i * r + r, nb - 1), 0)),
        pl.BlockSpec(p["w_mix"].shape, const2),
        pl.BlockSpec(p["b_mix"].shape, const2),
        pl.BlockSpec(p["w_sp"].shape, lambda i, s: (0, 0, 0)),
        pl.BlockSpec(p["b_spt"].shape, const2),
        pl.BlockSpec(p["sgu_g"].shape, const2),
        pl.BlockSpec(p["sgu_b"].shape, const2),
        pl.BlockSpec(p["w_ba"].shape, const2),
        pl.BlockSpec(p["w_bs"].shape, const2),
        pl.BlockSpec(p["w_out"].shape, const2),
        pl.BlockSpec(p["ln1_g"].shape, const2),
        pl.BlockSpec(p["ln1_b"].shape, const2),
        pl.BlockSpec(p["w_r"].shape, const2),
        pl.BlockSpec(p["b_r"].shape, const2),
    ]
    out_specs = [
        pl.BlockSpec((TM, D_MODEL), lambda i, s: (i, 0)),
        pl.BlockSpec((TM, D_MODEL), lambda i, s: (i, 0)),
        pl.BlockSpec((TM, LANES), lambda i, s: (i, 0)),
        pl.BlockSpec((1, 16, TM), lambda i, s: (i, 0, 0)),
        pl.BlockSpec((1, 8, LANES), lambda i, s: (i, 0, 0)),
    ]
    out_shape = [
        jax.ShapeDtypeStruct((T, D_MODEL), F32),
        jax.ShapeDtypeStruct((T, D_MODEL), BF16),
        jax.ShapeDtypeStruct((T, LANES), F32),
        jax.ShapeDtypeStruct((nt, 16, TM), F32),
        jax.ShapeDtypeStruct((nt, 8, LANES), I32),
    ]
    return pl.pallas_call(
        functools.partial(_mixer_kernel, seqs),
        grid_spec=pltpu.PrefetchScalarGridSpec(
            num_scalar_prefetch=1, grid=(nt,), in_specs=in_specs, out_specs=out_specs),
        out_shape=out_shape,
        compiler_params=pltpu.CompilerParams(dimension_semantics=("arbitrary",),
                                             vmem_limit_bytes=VMEM_LIMIT_BYTES),
        name="mixer",
    )(sink, x_all, mod3, cos_t, sin_t, kt, kt, kt, v, v, v,
      p["w_mix"], p["b_mix"], p["w_sp"], p["b_spt"], p["sgu_g"], p["sgu_b"],
      p["w_ba"], p["w_bs"], p["w_out"], p["ln1_g"], p["ln1_b"], p["w_r"], p["b_r"])


def _to_row_tiles(dst_ref, rows, n):
    for c in range(ROW_TILES):
        dst_ref[pl.ds(c, n, stride=ROW_TILES), :] = rows[:, c * LANES:(c + 1) * LANES]


def _from_row_tiles(src_ref, start, n):
    return jnp.concatenate(
        [src_ref[pl.ds(start * ROW_TILES + c, n, stride=ROW_TILES), :] for c in range(ROW_TILES)], axis=1)


def _dispatch_kernel(n_ref, s_ref, tail_ref, h2_ref, rrow_ref, xin_ref, stg_ref, zero_ref, cnt_ref, sem):
    i = pl.program_id(0)
    nt = pl.num_programs(0)
    slot = i % 2
    rr = rrow_ref[0]
    rho = lax.broadcasted_iota(I32, (DISP_ROWS, TM), 0)
    pt = jnp.zeros((DISP_ROWS, TM), F32)
    for k in range(TOP_K):
        pt = pt + jnp.where(rho == rr[TOP_K + k:TOP_K + k + 1].astype(I32), 1.0, 0.0)
    rows = jnp.dot(pt.astype(BF16), h2_ref[...], preferred_element_type=F32)

    chunk = DISP_CHUNK * ROW_TILES

    def copy(src_row, dst_row, sl):
        return pltpu.make_async_copy(
            stg_ref.at[sl, pl.ds(pl.multiple_of(src_row * ROW_TILES, ROW_TILES), chunk)],
            xin_ref.at[pl.ds(pl.multiple_of(dst_row * ROW_TILES, ROW_TILES), chunk)], sem)

    def wait_all(count):
        def body(_, c):
            copy(0, 0, 0).wait()
            return c
        lax.fori_loop(0, count, body, 0)

    def issue(sl):
        def per_expert(e, carry):
            b, total = carry
            n = n_ref[i * N_EXPERTS + e]
            s = s_ref[i * N_EXPERTS + e]
            nch = (n + DISP_CHUNK - 1) // DISP_CHUNK

            def per_chunk(j, c):
                copy(b + j * DISP_CHUNK, s + j * DISP_CHUNK, sl).start()
                return c
            lax.fori_loop(0, nch, per_chunk, 0)
            return b + n, total + nch
        _, total = lax.fori_loop(0, N_EXPERTS, per_expert, (0, 0))
        return total

    for sl in range(2):
        @pl.when(slot == sl)
        def _():
            _to_row_tiles(stg_ref.at[sl], rows, DISP_ROWS)

            @pl.when(i > 0)
            def _():
                wait_all(cnt_ref[0])
            cnt_ref[0] = issue(sl)

    @pl.when(i == nt - 1)
    def _():
        wait_all(cnt_ref[0])
        zero_ref[...] = jnp.zeros_like(zero_ref)

        def zcopy(dst_row):
            return pltpu.make_async_copy(
                zero_ref, xin_ref.at[pl.ds(pl.multiple_of(dst_row * ROW_TILES, ROW_TILES), chunk)], sem)

        def zwait(count):
            def body(_, c):
                zcopy(0).wait()
                return c
            lax.fori_loop(0, count, body, 0)

        def per_expert(e, total):
            lo = tail_ref[e]
            nz = (tail_ref[N_EXPERTS + e] - lo) // DISP_CHUNK

            def per_chunk(j, c):
                zcopy(lo + j * DISP_CHUNK).start()
                return c
            lax.fori_loop(0, nz, per_chunk, 0)
            return total + nz
        zwait(lax.fori_loop(0, N_EXPERTS, per_expert, 0))

        def last_chunk(e, c):
            zcopy(tail_ref[N_EXPERTS + e] - DISP_CHUNK).start()
            return c
        lax.fori_loop(0, N_EXPERTS, last_chunk, 0)
        zwait(N_EXPERTS)


def _dispatch(seqs, n_te, s_te, tails, h2, rrow, n_rows):
    nt = seqs.n_tokens // TM
    return pl.pallas_call(
        _dispatch_kernel,
        grid_spec=pltpu.PrefetchScalarGridSpec(
            num_scalar_prefetch=3, grid=(nt,),
            in_specs=[pl.BlockSpec((TM, D_MODEL), lambda i, *_: (i, 0)),
                      pl.BlockSpec((1, 16, TM), lambda i, *_: (i, 0, 0))],
            out_specs=pl.BlockSpec(memory_space=pl.ANY),
            scratch_shapes=[pltpu.VMEM((2, (DISP_ROWS + DISP_CHUNK) * ROW_TILES, LANES), F32),
                            pltpu.VMEM((DISP_CHUNK * ROW_TILES, LANES), F32),
                            pltpu.SMEM((1,), I32),
                            pltpu.SemaphoreType.DMA]),
        out_shape=jax.ShapeDtypeStruct((n_rows * ROW_TILES, LANES), F32),
        compiler_params=pltpu.CompilerParams(dimension_semantics=("arbitrary",),
                                             vmem_limit_bytes=VMEM_LIMIT_BYTES),
        name="dispatch",
    )(n_te, s_te, tails, h2, rrow)


def _expert_kernel(be_ref, bv_ref, x_ref, wup_ref, bup_ref, wdn_ref, bdn_ref, o_ref, wup_b, wdn_b):
    i = pl.program_id(0)
    valid = bv_ref[i]
    changed = jnp.logical_or(i == 0, be_ref[i] != be_ref[jnp.maximum(i - 1, 0)])

    @pl.when(jnp.logical_and(changed, valid > 0))
    def _():
        wup_b[...] = wup_ref[0].astype(BF16)
        wdn_b[...] = wdn_ref[0].astype(BF16)

    @pl.when(valid > 0)
    def _():
        x = _from_row_tiles(x_ref, 0, BM)
        row = lax.broadcasted_iota(I32, (BM, 1), 0)
        xb = jnp.where(row < valid, x, 0.0).astype(BF16)
        hu = jnp.dot(xb, wup_b[...], preferred_element_type=F32) + bup_ref[0]
        gate = jnp.minimum(hu[:, 0:D_FF], SWIGLU_LIMIT)
        lin = jnp.clip(hu[:, D_FF:], -SWIGLU_LIMIT, SWIGLU_LIMIT)
        act = gate * jax.nn.sigmoid(SWIGLU_ALPHA * gate) * (lin + 1.0)
        y = jnp.dot(act.astype(BF16), wdn_b[...], preferred_element_type=F32) + bdn_ref[0]
        _to_row_tiles(o_ref, y, BM)

    @pl.when(valid == 0)
    def _():
        o_ref[...] = jnp.zeros_like(o_ref)


def _experts(blk_e, blk_v, xin, w_up, b_up, w_down, b_down, n_blk):
    return pl.pallas_call(
        _expert_kernel,
        grid_spec=pltpu.PrefetchScalarGridSpec(
            num_scalar_prefetch=2, grid=(n_blk,),
            in_specs=[pl.BlockSpec((BM * ROW_TILES, LANES), lambda i, be, bv: (i, 0)),
                      pl.BlockSpec((1, D_MODEL, 2 * D_FF), lambda i, be, bv: (be[i], 0, 0)),
                      pl.BlockSpec((1, 1, 2 * D_FF), lambda i, be, bv: (be[i], 0, 0)),
                      pl.BlockSpec((1, D_FF, D_MODEL), lambda i, be, bv: (be[i], 0, 0)),
                      pl.BlockSpec((1, 1, D_MODEL), lambda i, be, bv: (be[i], 0, 0))],
            out_specs=pl.BlockSpec((BM * ROW_TILES, LANES), lambda i, be, bv: (i, 0)),
            scratch_shapes=[pltpu.VMEM((D_MODEL, 2 * D_FF), BF16),
                            pltpu.VMEM((D_FF, D_MODEL), BF16)]),
        out_shape=jax.ShapeDtypeStruct((n_blk * BM * ROW_TILES, LANES), F32),
        compiler_params=pltpu.CompilerParams(dimension_semantics=("arbitrary",),
                                             vmem_limit_bytes=VMEM_LIMIT_BYTES),
        name="experts",
    )(blk_e, blk_v, xin, w_up, b_up.reshape(N_EXPERTS, 1, 2 * D_FF), w_down,
      b_down.reshape(N_EXPERTS, 1, D_MODEL))


def _combine_kernel(seqs, n_ref, s_ref, x1_ref, rcol_ref, mod_ref, g_ref, b_ref, eo_ref, y_ref, stg_ref, sem):
    i = pl.program_id(0)
    chunk = COMB_CHUNK * ROW_TILES

    @pl.when(i == 0)
    def _():
        stg_ref[...] = jnp.zeros_like(stg_ref)

    def copy(src_row, dst_row):
        return pltpu.make_async_copy(
            eo_ref.at[pl.ds(pl.multiple_of(src_row * ROW_TILES, ROW_TILES), chunk)],
            stg_ref.at[pl.ds(pl.multiple_of(dst_row * ROW_TILES, ROW_TILES), chunk)], sem)

    def per_expert(e, carry):
        b, total = carry
        n = n_ref[i * N_EXPERTS + e]
        s = s_ref[i * N_EXPERTS + e]
        nch = (n + COMB_CHUNK - 1) // COMB_CHUNK

        def per_chunk(j, c):
            copy(s + j * COMB_CHUNK, b + j * COMB_CHUNK).start()
            return c
        lax.fori_loop(0, nch, per_chunk, 0)
        return b + nch * COMB_CHUNK, total + nch
    _, total = lax.fori_loop(0, N_EXPERTS, per_expert, (0, 0))

    rc = rcol_ref[...]
    col_k = [rc[:, 2 * TOP_K + k:2 * TOP_K + k + 1].astype(I32) for k in range(TOP_K)]
    w_k = [rc[:, 3 * TOP_K + k:3 * TOP_K + k + 1] for k in range(TOP_K)]

    def wait_body(_, c):
        copy(0, 0).wait()
        return c
    lax.fori_loop(0, total, wait_body, 0)

    y = jnp.zeros((TM, D_MODEL), F32)
    for c in range(COMB_ROWS // 256):
        jl = lax.broadcasted_iota(I32, (TM, 256), 1) + c * 256
        pm = jnp.zeros((TM, 256), F32)
        for k in range(TOP_K):
            pm = pm + jnp.where(jl == col_k[k], w_k[k], 0.0)
        rows = _from_row_tiles(stg_ref, c * 256, 256).astype(BF16)
        y = y + jnp.dot(pm.astype(BF16), rows, preferred_element_type=F32)

    mod = mod_ref[0]
    g2 = mod[:, 5 * D_MODEL:6 * D_MODEL]
    y_ref[...] = _layer_norm(DN_ALPHA * x1_ref[...] + g2 * y, g_ref[...], b_ref[...])


def _combine(seqs, n_te, s_te, x1, rcol, mod3, ln2_g, ln2_b, eo):
    T = seqs.n_tokens
    nt = T // TM

    def mod_map(i, *_):
        return (_tile_pos(seqs, i * TM)[2], 0, 0)

    return pl.pallas_call(
        functools.partial(_combine_kernel, seqs),
        grid_spec=pltpu.PrefetchScalarGridSpec(
            num_scalar_prefetch=2, grid=(nt,),
            in_specs=[pl.BlockSpec((TM, D_MODEL), lambda i, *_: (i, 0)),
                      pl.BlockSpec((TM, LANES), lambda i, *_: (i, 0)),
                      pl.BlockSpec((1, 1, 6 * D_MODEL), mod_map),
                      pl.BlockSpec((1, D_MODEL), lambda i, *_: (0, 0)),
                      pl.BlockSpec((1, D_MODEL), lambda i, *_: (0, 0)),
                      pl.BlockSpec(memory_space=pl.ANY)],
            out_specs=pl.BlockSpec((TM, D_MODEL), lambda i, *_: (i, 0)),
            scratch_shapes=[pltpu.VMEM((COMB_ROWS * ROW_TILES, LANES), F32),
                            pltpu.SemaphoreType.DMA]),
        out_shape=jax.ShapeDtypeStruct((T, D_MODEL), F32),
        compiler_params=pltpu.CompilerParams(dimension_semantics=("arbitrary",),
                                             vmem_limit_bytes=VMEM_LIMIT_BYTES),
        name="combine",
    )(n_te, s_te, x1, rcol, mod3, ln2_g, ln2_b, eo)


def _prep_params(w_in, b_in, w_spatial, b_spatial, sgu_ln_g, sgu_ln_b, w_br_attn, w_br_sgu, w_out,
                 ln1_g, ln1_b, w_router, b_router):
    q_end, k_end, v_end = ATT_W, ATT_W + KV_W, ATT_W + 2 * KV_W

    def dup(w, lo):
        h0, h1 = w[..., lo:lo + HEAD_DIM], w[..., lo + HEAD_DIM:lo + 2 * HEAD_DIM]
        return jnp.concatenate([h0, h0, h1, h1], axis=-1)

    w_kv = jnp.concatenate([dup(w_in, q_end), dup(w_in, k_end)], axis=1).astype(BF16)
    b_kv = jnp.concatenate([dup(b_in, q_end), dup(b_in, k_end)], axis=0).reshape(1, -1)
    w_mix = jnp.concatenate([w_in[:, :q_end], w_in[:, v_end:]], axis=1).astype(BF16)
    b_mix = jnp.concatenate([b_in[:q_end], b_in[v_end:]], axis=0).reshape(1, -1)
    w_hi = w_router.astype(BF16)
    w_lo = (w_router - w_hi.astype(F32)).astype(BF16)
    p = dict(
        w_mix=w_mix, b_mix=b_mix,
        w_sp=w_spatial.astype(BF16), b_spt=b_spatial.T,
        sgu_g=sgu_ln_g.reshape(1, -1), sgu_b=sgu_ln_b.reshape(1, -1),
        w_ba=w_br_attn.astype(BF16), w_bs=w_br_sgu.astype(BF16), w_out=w_out.astype(BF16),
        ln1_g=ln1_g.reshape(1, -1), ln1_b=ln1_b.reshape(1, -1),
        w_r=jnp.concatenate([w_hi, w_lo], axis=1), b_r=b_router.reshape(1, -1),
    )
    return w_kv, b_kv, p


def _layer(seqs, x_all, c_all, w_ada, b_ada, w_in, b_in, sink, sgu_ln_g, sgu_ln_b, w_spatial, b_spatial,
           w_br_attn, w_br_sgu, w_out, ln1_g, ln1_b, w_router, b_router, w_up, b_up, w_down, b_down,
           ln2_g, ln2_b):
    T = seqs.n_tokens
    nt = T // TM
    c_pad = jnp.zeros((8, D_MODEL), F32).at[:c_all.shape[0]].set(c_all)
    mod3 = _ada(c_pad, w_ada, b_ada).reshape(8, 1, 6 * D_MODEL)
    cos_t, sin_t = _rope_tables(max(seqs.prompt_len, seqs.sample_len))
    w_kv, b_kv, p = _prep_params(w_in, b_in, w_spatial, b_spatial, sgu_ln_g, sgu_ln_b, w_br_attn,
                                 w_br_sgu, w_out, ln1_g, ln1_b, w_router, b_router)
    kt, v = _kv(seqs, x_all, mod3, cos_t, sin_t, w_kv, b_kv)
    x1, h2, rcol, rrow, cnt3 = _mixer(seqs, sink, x_all, mod3, cos_t, sin_t, kt, v, p)

    cnt = cnt3[:, 0, :N_EXPERTS]
    count = cnt.sum(0)
    reg = (count + REGION_SLACK + BM - 1) // BM * BM
    pad_end = jnp.cumsum(reg)
    pad_start = pad_end - reg
    s_te = (pad_start[None, :] + jnp.cumsum(cnt, axis=0) - cnt).reshape(-1).astype(I32)
    n_te = cnt.reshape(-1).astype(I32)
    n_blk = (T * TOP_K + N_EXPERTS * (REGION_SLACK + BM - 1)) // BM + 1
    tails = jnp.concatenate([pad_start + count, pad_end[:-1], jnp.array([n_blk * BM])]).astype(I32)
    blk_start = jnp.arange(n_blk, dtype=I32) * BM
    blk_e = jnp.minimum(jnp.searchsorted(pad_end, blk_start, side="right"), N_EXPERTS - 1).astype(I32)
    blk_v = jnp.where(blk_start < pad_end[-1],
                      jnp.clip(count[blk_e] - (blk_start - pad_start[blk_e]), 0, BM), 0).astype(I32)

    xin = _dispatch(seqs, n_te, s_te, tails, h2, rrow, n_blk * BM)
    eo = _experts(blk_e, blk_v, xin, w_up, b_up, w_down, b_down, n_blk)
    return _combine(seqs, n_te, s_te, x1, rcol, mod3, ln2_g.reshape(1, -1), ln2_b.reshape(1, -1), eo)


def kernel(x_prompt, x_sample, c_prompt, c_sample, w_ada, b_ada, w_in, b_in, sink, sgu_ln_g, sgu_ln_b, w_spatial, b_spatial, w_br_attn, w_br_sgu, w_out, ln1_g, ln1_b, w_router, b_router, w_up, b_up, w_down, b_down, ln2_g, ln2_b):
    assert w_ada.shape[0] == DEPTH == 1
    bp, sp, d = x_prompt.shape
    bs, ss, _ = x_sample.shape
    seqs = _Seqs(n_prompt=bp * sp, prompt_len=sp, sample_len=ss, n_tokens=bp * sp + bs * ss)
    x_all = jnp.concatenate([x_prompt.reshape(bp * sp, d), x_sample.reshape(bs * ss, d)], axis=0)
    c_all = jnp.concatenate([c_prompt, c_sample], axis=0)
    y = _layer(seqs, x_all, c_all, w_ada[0], b_ada[0], w_in[0], b_in[0], sink[0], sgu_ln_g[0], sgu_ln_b[0],
               w_spatial[0], b_spatial[0], w_br_attn[0], w_br_sgu[0], w_out[0], ln1_g[0], ln1_b[0],
               w_router[0], b_router[0], w_up[0], b_up[0], w_down[0], b_down[0], ln2_g[0], ln2_b[0])
    return (y[:bp * sp].reshape(bp, sp, d), y[bp * sp:].reshape(bs, ss, d))
```

```python
import functools
import math
from typing import NamedTuple

import jax
import jax.numpy as jnp
from jax import lax
from jax.experimental import pallas as pl
from jax.experimental.pallas import tpu as pltpu

F32 = jnp.float32
BF16 = jnp.bfloat16
I32 = jnp.int32

D_MODEL = 1024
N_HEADS = 8
N_KV_HEADS = 2
HEAD_DIM = 64
ATT_W = N_HEADS * HEAD_DIM
KV_W = N_KV_HEADS * HEAD_DIM
BLOCK = 128
ROPE_THETA = 500000.0
ROT_DIM = HEAD_DIM // 4
ROT_HALF = ROT_DIM // 2
SGU_W = D_MODEL // 2
SGU_GROUPS = 4
N_EXPERTS = 32
TOP_K = 4
D_FF = D_MODEL
SWIGLU_LIMIT = 7.0
SWIGLU_ALPHA = 1.702
LN_EPS = 1e-5
DEPTH = 1
DN_ALPHA = (2 * DEPTH) ** 0.25

LANES = 128
SUBLANES = 8
ROW_TILES = D_MODEL // LANES
VMEM_LIMIT_BYTES = 56 * 1024 * 1024

TM = 256
TK = 512
BM = 256
DISP_CHUNK = 32
COMB_CHUNK = 16
REGION_SLACK = DISP_CHUNK
DISP_ROWS = TM * TOP_K + DISP_CHUNK
COMB_ROWS = ((TM * TOP_K + N_EXPERTS * (COMB_CHUNK - 1)) + 255) // 256 * 256
NEG_INF = float("-inf")


class _Seqs(NamedTuple):
    n_prompt: int
    prompt_len: int
    sample_len: int
    n_tokens: int


def _tile_pos(seqs, t0):
    is_s = t0 >= seqs.n_prompt
    seq_len = jnp.where(is_s, seqs.sample_len, seqs.prompt_len)
    off = jnp.where(is_s, t0 - seqs.n_prompt, t0)
    pos0 = off % seq_len
    row = jnp.where(is_s, seqs.n_prompt // seqs.prompt_len + off // seq_len, off // seq_len)
    return seq_len, pos0, row


def _ada_kernel(c_ref, w_ref, b_ref, o_ref):
    c = c_ref[...]
    a = c * jax.nn.sigmoid(c)
    o_ref[...] = jnp.dot(a, w_ref[...], preferred_element_type=F32,
                         precision=lax.Precision.HIGHEST) + b_ref[...]


def _ada(c_pad, w_ada, b_ada):
    n = w_ada.shape[1]
    bn = 1536
    return pl.pallas_call(
        _ada_kernel,
        grid=(n // bn,),
        in_specs=[pl.BlockSpec((8, D_MODEL), lambda j: (0, 0)),
                  pl.BlockSpec((D_MODEL, bn), lambda j: (0, j)),
                  pl.BlockSpec((1, bn), lambda j: (0, j))],
        out_specs=pl.BlockSpec((8, bn), lambda j: (0, j)),
        out_shape=jax.ShapeDtypeStruct((8, n), F32),
        compiler_params=pltpu.CompilerParams(vmem_limit_bytes=VMEM_LIMIT_BYTES),
        name="ada",
    )(c_pad, w_ada, b_ada.reshape(1, n))


def _rope_tables(length):
    lane = jnp.arange(LANES) % HEAD_DIM
    inv = ROPE_THETA ** (-(lane % ROT_HALF).astype(F32) * 2.0 / ROT_DIM)
    inv = jnp.where(lane < ROT_DIM, inv, 0.0)
    sign = jnp.where(lane < ROT_HALF, -1.0, 1.0)
    ang = jnp.arange(length, dtype=F32)[:, None] * inv[None, :]
    return jnp.cos(ang), jnp.sin(ang) * sign[None, :]


def _rope(x, cos, sin):
    n = x.shape[1]
    reps = n // LANES
    c = jnp.concatenate([cos] * reps, axis=1)
    s = jnp.concatenate([sin] * reps, axis=1)
    lane = lax.broadcasted_iota(I32, x.shape, 1)
    first = (lane & (HEAD_DIM - 1)) < ROT_HALF
    partner = jnp.where(first, pltpu.roll(x, n - ROT_HALF, 1), pltpu.roll(x, ROT_HALF, 1))
    return x * c + partner * s


def _kv_kernel(seqs, xp_ref, xs_ref, mod_ref, cos_ref, sin_ref, w_ref, b_ref, kt_ref, v_ref):
    mod = mod_ref[0]
    sh1 = mod[:, 0:D_MODEL]
    sc1 = mod[:, D_MODEL:2 * D_MODEL]
    x = jnp.where(pl.program_id(0) * TK >= seqs.n_prompt, xs_ref[...], xp_ref[...])
    h = (x * (1.0 + sc1) + sh1).astype(BF16)
    kv = jnp.dot(h, w_ref[...], preferred_element_type=F32) + b_ref[...]
    k = _rope(kv[:, 0:2 * LANES], cos_ref[...], sin_ref[...])
    kt_ref[...] = k.T.astype(BF16)
    v_ref[...] = kv[:, 2 * LANES:4 * LANES].astype(BF16)


def _group_x_specs(seqs, tile):
    ntp = seqs.n_prompt // tile
    return [pl.BlockSpec((tile, D_MODEL), lambda i, *_: (jnp.minimum(i, ntp - 1), 0)),
            pl.BlockSpec((tile, D_MODEL), lambda i, *_: (jnp.maximum(i - ntp, 0), 0))]


def _kv(seqs, xp, xs, mod3, cos_t, sin_t, w_kv, b_kv):
    T = seqs.n_tokens

    def mod_map(i):
        return (_tile_pos(seqs, i * TK)[2], 0, 0)

    def rope_map(i):
        return (_tile_pos(seqs, i * TK)[1] // TK, 0)

    return pl.pallas_call(
        functools.partial(_kv_kernel, seqs),
        grid=(T // TK,),
        in_specs=_group_x_specs(seqs, TK) + [
                  pl.BlockSpec((1, 1, 6 * D_MODEL), mod_map),
                  pl.BlockSpec((TK, LANES), rope_map),
                  pl.BlockSpec((TK, LANES), rope_map),
                  pl.BlockSpec((D_MODEL, 4 * LANES), lambda i: (0, 0)),
                  pl.BlockSpec((1, 4 * LANES), lambda i: (0, 0))],
        out_specs=[pl.BlockSpec((2 * LANES, TK), lambda i: (0, i)),
                   pl.BlockSpec((TK, 2 * LANES), lambda i: (i, 0))],
        out_shape=[jax.ShapeDtypeStruct((2 * LANES, T), BF16),
                   jax.ShapeDtypeStruct((T, 2 * LANES), BF16)],
        compiler_params=pltpu.CompilerParams(dimension_semantics=("arbitrary",),
                                             vmem_limit_bytes=VMEM_LIMIT_BYTES),
        name="kv",
    )(xp, xs, mod3, cos_t, sin_t, w_kv, b_kv)


def _layer_norm(x, g, b):
    mu = jnp.mean(x, axis=-1, keepdims=True)
    xc = x - mu
    var = jnp.mean(xc * xc, axis=-1, keepdims=True)
    return xc * lax.rsqrt(var + LN_EPS) * g + b


def _attention_block(q_blk, kwin, vwin, valid, sink_ref, h):
    lane = lax.broadcasted_iota(I32, (BLOCK, LANES), 1)
    lo = lane < HEAD_DIM
    parts, sinks = [], []
    for p in range(2):
        qp = q_blk[:, p * LANES:(p + 1) * LANES]
        parts.append(jnp.where(lo, qp, 0.0).astype(BF16))
        parts.append(jnp.where(lo, 0.0, qp).astype(BF16))
    for g in range(4):
        sinks.append(jnp.full((BLOCK, 1), sink_ref[h * 4 + g], F32))
    qm = jnp.concatenate(parts, axis=0)
    sk = jnp.concatenate(sinks, axis=0)
    s = jnp.dot(qm, kwin, preferred_element_type=F32)
    vmask = jnp.concatenate([valid] * 4, axis=0)
    s = jnp.where(vmask, s, NEG_INF)
    m = jnp.maximum(jnp.max(s, axis=-1, keepdims=True), sk)
    p = jnp.exp(s - m)
    denom = jnp.sum(p, axis=-1, keepdims=True) + jnp.exp(sk - m)
    pn = (p / denom).astype(BF16)
    o = jnp.dot(pn, vwin, preferred_element_type=F32)
    outs = []
    for p2 in range(2):
        outs.append(jnp.where(lo, o[(2 * p2) * BLOCK:(2 * p2 + 1) * BLOCK],
                              o[(2 * p2 + 1) * BLOCK:(2 * p2 + 2) * BLOCK]))
    return jnp.concatenate(outs, axis=1)


def _mixer_kernel(seqs, sink_ref, xp_ref, xs_ref, mod_ref, cos_ref, sin_ref,
                  ktp_ref, ktc_ref, ktn_ref, vp_ref, vc_ref, vn_ref,
                  wmix_ref, bmix_ref, wsp_ref, bspt_ref, sg_ref, sb_ref,
                  wba_ref, wbs_ref, wout_ref, l1g_ref, l1b_ref, wr_ref, br_ref,
                  x1_ref, h2_ref, rcol_ref, rrow_ref, cnt_ref):
    i = pl.program_id(0)
    seq_len, pos0, _ = _tile_pos(seqs, i * TM)
    mod = mod_ref[0]
    sh1, sc1, g1 = (mod[:, j * D_MODEL:(j + 1) * D_MODEL] for j in range(3))
    sh2, sc2 = (mod[:, j * D_MODEL:(j + 1) * D_MODEL] for j in range(3, 5))
    x = jnp.where(i * TM >= seqs.n_prompt, xs_ref[...], xp_ref[...])
    h = (x * (1.0 + sc1) + sh1).astype(BF16)
    z = jnp.dot(h, wmix_ref[...], preferred_element_type=F32) + bmix_ref[...]
    q = _rope(z[:, 0:ATT_W], cos_ref[...], sin_ref[...]) * (HEAD_DIM ** -0.5)
    u = jax.nn.gelu(z[:, ATT_W:ATT_W + SGU_W])
    vs = _layer_norm(jax.nn.gelu(z[:, ATT_W + SGU_W:ATT_W + 2 * SGU_W]), sg_ref[...], sb_ref[...])
    ga = z[:, ATT_W + 2 * SGU_W:ATT_W + 2 * SGU_W + D_MODEL]
    gs = z[:, ATT_W + 2 * SGU_W + D_MODEL:]

    kfull = jnp.concatenate([ktp_ref[...], ktc_ref[...], ktn_ref[...]], axis=1)
    vfull = jnp.concatenate([vp_ref[...], vc_ref[...], vn_ref[...]], axis=0)
    qi = lax.broadcasted_iota(I32, (BLOCK, 3 * BLOCK), 0)
    ki = lax.broadcasted_iota(I32, (BLOCK, 3 * BLOCK), 1)
    band = (ki >= qi) & (ki <= qi + 2 * BLOCK)
    vs_b = vs.astype(BF16)
    attn_rows, sgu_rows = [], []
    for jb in range(TM // BLOCK):
        posb = pos0 + jb * BLOCK
        valid = band & (ki >= jnp.where(posb == 0, BLOCK, 0)) \
                     & (ki < jnp.where(posb + BLOCK == seq_len, 2 * BLOCK, 3 * BLOCK))
        heads = []
        for hk in range(N_KV_HEADS):
            kwin = kfull[hk * LANES:(hk + 1) * LANES, jb * BLOCK:(jb + 3) * BLOCK]
            vwin = vfull[jb * BLOCK:(jb + 3) * BLOCK, hk * LANES:(hk + 1) * LANES]
            q_blk = q[jb * BLOCK:(jb + 1) * BLOCK, hk * 2 * LANES:(hk + 1) * 2 * LANES]
            heads.append(_attention_block(q_blk, kwin, vwin, valid, sink_ref, hk))
        attn_rows.append(jnp.concatenate(heads, axis=1))
        groups = []
        for g in range(SGU_GROUPS):
            vg = vs_b[jb * BLOCK:(jb + 1) * BLOCK, g * LANES:(g + 1) * LANES]
            sv = jnp.dot(wsp_ref[g], vg, preferred_element_type=F32) + bspt_ref[:, g:g + 1]
            groups.append(sv)
        sgu_rows.append(jnp.concatenate(groups, axis=1))
    attn = jnp.concatenate(attn_rows, axis=0)
    sgu = u * jnp.concatenate(sgu_rows, axis=0)

    a1 = jnp.dot(attn.astype(BF16), wba_ref[...], preferred_element_type=F32)
    a2 = jnp.dot(sgu.astype(BF16), wbs_ref[...], preferred_element_type=F32)
    merged = jax.nn.sigmoid(ga) * a1 + jax.nn.sigmoid(gs) * a2
    mix = jnp.dot(merged.astype(BF16), wout_ref[...], preferred_element_type=F32)
    x1 = _layer_norm(DN_ALPHA * x + g1 * mix, l1g_ref[...], l1b_ref[...])
    x1_ref[...] = x1
    h2 = x1 * (1.0 + sc2) + sh2
    hi = h2.astype(BF16)
    h2_ref[...] = hi

    lo_part = (h2 - hi.astype(F32)).astype(BF16)
    l1 = jnp.dot(hi, wr_ref[...], preferred_element_type=F32)
    l2 = jnp.dot(lo_part, wr_ref[:, 0:N_EXPERTS], preferred_element_type=F32)
    logits = l1[:, 0:N_EXPERTS] + l1[:, N_EXPERTS:2 * N_EXPERTS] + l2 + br_ref[...]

    eidx = lax.broadcasted_iota(I32, (TM, N_EXPERTS), 1)
    work = logits
    idxs, vals = [], []
    for _ in range(TOP_K):
        m = jnp.max(work, axis=-1, keepdims=True)
        ix = jnp.min(jnp.where(work == m, eidx, N_EXPERTS), axis=-1, keepdims=True)
        idxs.append(ix)
        vals.append(m)
        work = jnp.where(eidx == ix, NEG_INF, work)
    exps = [jnp.exp(v - vals[0]) for v in vals]
    esum = exps[0] + exps[1] + exps[2] + exps[3]
    wts = [e / esum for e in exps]

    sel = jnp.zeros((TM, N_EXPERTS), F32)
    for ix in idxs:
        sel = sel + jnp.where(eidx == ix, 1.0, 0.0)
    sel_b = sel.astype(BF16)
    ti = lax.broadcasted_iota(I32, (TM, TM), 0)
    tj = lax.broadcasted_iota(I32, (TM, TM), 1)
    lower = jnp.where(tj < ti, 1.0, 0.0).astype(BF16)
    rank = jnp.dot(lower, sel_b, preferred_element_type=F32)
    cnt = jnp.sum(sel, axis=0, keepdims=True)
    nch = jnp.floor((cnt + (COMB_CHUNK - 1)) * (1.0 / COMB_CHUNK))
    ei = lax.broadcasted_iota(I32, (N_EXPERTS, N_EXPERTS), 0)
    ej = lax.broadcasted_iota(I32, (N_EXPERTS, N_EXPERTS), 1)
    before = jnp.where(ei < ej, 1.0, 0.0).astype(BF16)
    pre = jnp.concatenate([cnt, nch, jnp.zeros((6, N_EXPERTS), F32)], axis=0).astype(BF16)
    base = jnp.dot(pre, before, preferred_element_type=F32)
    dbase = base[0:1]
    cbase = base[1:2] * COMB_CHUNK

    cols = []
    for ix in idxs:
        cols.append(ix.astype(F32))
    for ix in idxs:
        hit = eidx == ix
        rk = jnp.sum(jnp.where(hit, rank, 0.0), axis=-1, keepdims=True)
        cols.append(rk + jnp.sum(jnp.where(hit, dbase, 0.0), axis=-1, keepdims=True))
    for ix in idxs:
        hit = eidx == ix
        rk = jnp.sum(jnp.where(hit, rank, 0.0), axis=-1, keepdims=True)
        cols.append(rk + jnp.sum(jnp.where(hit, cbase, 0.0), axis=-1, keepdims=True))
    cols.extend(wts)
    lane = lax.broadcasted_iota(I32, (TM, LANES), 1)
    rc = jnp.zeros((TM, LANES), F32)
    for j, c in enumerate(cols):
        rc = jnp.where(lane == j, c, rc)
    rcol_ref[...] = rc
    rrow_ref[0] = rc.T[0:16]
    cnt_ref[0] = jnp.broadcast_to(
        jnp.concatenate([cnt, jnp.zeros((1, LANES - N_EXPERTS), F32)], axis=1), (8, LANES)).astype(I32)


def _mixer(seqs, sink, xp, xs, mod3, cos_t, sin_t, kt, v, p):
    T = seqs.n_tokens
    nt = T // TM
    nb = T // BLOCK
    r = TM // BLOCK

    def mod_map(i, s):
        return (_tile_pos(seqs, i * TM)[2], 0, 0)

    def rope_map(i, s):
        return (_tile_pos(seqs, i * TM)[1] // TM, 0)

    const2 = lambda i, s: (0, 0)
    in_specs = _group_x_specs(seqs, TM) + [
        pl.BlockSpec((1, 1, 6 * D_MODEL), mod_map),
        pl.BlockSpec((TM, LANES), rope_map),
        pl.BlockSpec((TM, LANES), rope_map),
        pl.BlockSpec((2 * LANES, BLOCK), lambda i, s: (0, jnp.maximum(i * r - 1, 0))),
        pl.BlockSpec((2 * LANES, TM), lambda i, s: (0, i)),
        pl.BlockSpec((2 * LANES, BLOCK), lambda i, s: (0, jnp.minimum(i * r + r, nb - 1))),
        pl.BlockSpec((BLOCK, 2 * LANES), lambda i, s: (jnp.maximum(i * r - 1, 0), 0)),
        pl.BlockSpec((TM, 2 * LANES), lambda i, s: (i, 0)),
        pl.BlockSpec((BLOCK, 2 * LANES), lambda i, s: (jnp.minimum(i * r + r, nb - 1), 0)),
        pl.BlockSpec(p["w_mix"].shape, const2),
        pl.BlockSpec(p["b_mix"].shape, const2),
        pl.BlockSpec(p["w_sp"].shape, lambda i, s: (0, 0, 0)),
        pl.BlockSpec(p["b_spt"].shape, const2),
        pl.BlockSpec(p["sgu_g"].shape, const2),
        pl.BlockSpec(p["sgu_b"].shape, const2),
        pl.BlockSpec(p["w_ba"].shape, const2),
        pl.BlockSpec(p["w_bs"].shape, const2),
        pl.BlockSpec(p["w_out"].shape, const2),
        pl.BlockSpec(p["ln1_g"].shape, const2),
        pl.BlockSpec(p["ln1_b"].shape, const2),
        pl.BlockSpec(p["w_r"].shape, const2),
        pl.BlockSpec(p["b_r"].shape, const2),
    ]
    out_specs = [
        pl.BlockSpec((TM, D_MODEL), lambda i, s: (i, 0)),
        pl.BlockSpec((TM, D_MODEL), lambda i, s: (i, 0)),
        pl.BlockSpec((TM, LANES), lambda i, s: (i, 0)),
        pl.BlockSpec((1, 16, TM), lambda i, s: (i, 0, 0)),
        pl.BlockSpec((1, 8, LANES), lambda i, s: (i, 0, 0)),
    ]
    out_shape = [
        jax.ShapeDtypeStruct((T, D_MODEL), F32),
        jax.ShapeDtypeStruct((T, D_MODEL), BF16),
        jax.ShapeDtypeStruct((T, LANES), F32),
        jax.ShapeDtypeStruct((nt, 16, TM), F32),
        jax.ShapeDtypeStruct((nt, 8, LANES), I32),
    ]
    return pl.pallas_call(
        functools.partial(_mixer_kernel, seqs),
        grid_spec=pltpu.PrefetchScalarGridSpec(
            num_scalar_prefetch=1, grid=(nt,), in_specs=in_specs, out_specs=out_specs),
        out_shape=out_shape,
        compiler_params=pltpu.CompilerParams(dimension_semantics=("arbitrary",),
                                             vmem_limit_bytes=VMEM_LIMIT_BYTES),
        name="mixer",
    )(sink, xp, xs, mod3, cos_t, sin_t, kt, kt, kt, v, v, v,
      p["w_mix"], p["b_mix"], p["w_sp"], p["b_spt"], p["sgu_g"], p["sgu_b"],
      p["w_ba"], p["w_bs"], p["w_out"], p["ln1_g"], p["ln1_b"], p["w_r"], p["b_r"])


def _to_row_tiles(dst_ref, rows, n):
    for c in range(ROW_TILES):
        dst_ref[pl.ds(c, n, stride=ROW_TILES), :] = rows[:, c * LANES:(c + 1) * LANES]


def _from_row_tiles(src_ref, start, n):
    return jnp.concatenate(
        [src_ref[pl.ds(start * ROW_TILES + c, n, stride=ROW_TILES), :] for c in range(ROW_TILES)], axis=1)


def _dispatch_kernel(n_ref, s_ref, tail_ref, h2_ref, rrow_ref, xin_ref, stg_ref, zero_ref, cnt_ref, sem):
    i = pl.program_id(0)
    nt = pl.num_programs(0)
    slot = i % 2
    rr = rrow_ref[0]
    rho = lax.broadcasted_iota(I32, (DISP_ROWS, TM), 0)
    pt = jnp.zeros((DISP_ROWS, TM), F32)
    for k in range(TOP_K):
        pt = pt + jnp.where(rho == rr[TOP_K + k:TOP_K + k + 1].astype(I32), 1.0, 0.0)
    rows = jnp.dot(pt.astype(BF16), h2_ref[...], preferred_element_type=F32)

    chunk = DISP_CHUNK * ROW_TILES

    def copy(src_row, dst_row, sl):
        return pltpu.make_async_copy(
            stg_ref.at[sl, pl.ds(pl.multiple_of(src_row * ROW_TILES, ROW_TILES), chunk)],
            xin_ref.at[pl.ds(pl.multiple_of(dst_row * ROW_TILES, ROW_TILES), chunk)], sem)

    def wait_all(count):
        def body(_, c):
            copy(0, 0, 0).wait()
            return c
        lax.fori_loop(0, count, body, 0)

    def issue(sl):
        def per_expert(e, carry):
            b, total = carry
            n = n_ref[i * N_EXPERTS + e]
            s = s_ref[i * N_EXPERTS + e]
            nch = (n + DISP_CHUNK - 1) // DISP_CHUNK

            def per_chunk(j, c):
                copy(b + j * DISP_CHUNK, s + j * DISP_CHUNK, sl).start()
                return c
            lax.fori_loop(0, nch, per_chunk, 0)
            return b + n, total + nch
        _, total = lax.fori_loop(0, N_EXPERTS, per_expert, (0, 0))
        return total

    for sl in range(2):
        @pl.when(slot == sl)
        def _():
            _to_row_tiles(stg_ref.at[sl], rows, DISP_ROWS)

            @pl.when(i > 0)
            def _():
                wait_all(cnt_ref[0])
            cnt_ref[0] = issue(sl)

    @pl.when(i == nt - 1)
    def _():
        wait_all(cnt_ref[0])
        zero_ref[...] = jnp.zeros_like(zero_ref)

        def zcopy(dst_row):
            return pltpu.make_async_copy(
                zero_ref, xin_ref.at[pl.ds(pl.multiple_of(dst_row * ROW_TILES, ROW_TILES), chunk)], sem)

        def zwait(count):
            def body(_, c):
                zcopy(0).wait()
                return c
            lax.fori_loop(0, count, body, 0)

        def per_expert(e, total):
            lo = tail_ref[e]
            nz = (tail_ref[N_EXPERTS + e] - lo) // DISP_CHUNK

            def per_chunk(j, c):
                zcopy(lo + j * DISP_CHUNK).start()
                return c
            lax.fori_loop(0, nz, per_chunk, 0)
            return total + nz
        zwait(lax.fori_loop(0, N_EXPERTS, per_expert, 0))

        def last_chunk(e, c):
            zcopy(tail_ref[N_EXPERTS + e] - DISP_CHUNK).start()
            return c
        lax.fori_loop(0, N_EXPERTS, last_chunk, 0)
        zwait(N_EXPERTS)


def _dispatch(seqs, n_te, s_te, tails, h2, rrow, n_rows):
    nt = seqs.n_tokens // TM
    return pl.pallas_call(
        _dispatch_kernel,
        grid_spec=pltpu.PrefetchScalarGridSpec(
            num_scalar_prefetch=3, grid=(nt,),
            in_specs=[pl.BlockSpec((TM, D_MODEL), lambda i, *_: (i, 0)),
                      pl.BlockSpec((1, 16, TM), lambda i, *_: (i, 0, 0))],
            out_specs=pl.BlockSpec(memory_space=pl.ANY),
            scratch_shapes=[pltpu.VMEM((2, (DISP_ROWS + DISP_CHUNK) * ROW_TILES, LANES), F32),
                            pltpu.VMEM((DISP_CHUNK * ROW_TILES, LANES), F32),
                            pltpu.SMEM((1,), I32),
                            pltpu.SemaphoreType.DMA]),
        out_shape=jax.ShapeDtypeStruct((n_rows * ROW_TILES, LANES), F32),
        compiler_params=pltpu.CompilerParams(dimension_semantics=("arbitrary",),
                                             vmem_limit_bytes=VMEM_LIMIT_BYTES),
        name="dispatch",
    )(n_te, s_te, tails, h2, rrow)


def _expert_kernel(be_ref, bv_ref, x_ref, wup_hbm, bup_ref, wdn_hbm, bdn_ref, o_ref,
                   wup_f, wdn_f, wup_b, wdn_b, sem):
    i = pl.program_id(0)
    e = be_ref[i]
    valid = bv_ref[i]
    first = jnp.logical_or(i == 0, e != be_ref[jnp.maximum(i - 1, 0)])

    def fetch(ex):
        slot = ex % 2
        return (pltpu.make_async_copy(wup_hbm.at[ex], wup_f.at[slot], sem.at[0, slot]),
                pltpu.make_async_copy(wdn_hbm.at[ex], wdn_f.at[slot], sem.at[1, slot]))

    @pl.when(i == 0)
    def _():
        for d in fetch(e):
            d.start()

    @pl.when(first)
    def _():
        for d in fetch(e):
            d.wait()

        @pl.when(e + 1 < N_EXPERTS)
        def _():
            for d in fetch(e + 1):
                d.start()

        @pl.when(valid > 0)
        def _():
            wup_b[...] = wup_f[e % 2].astype(BF16)
            wdn_b[...] = wdn_f[e % 2].astype(BF16)

    @pl.when(valid > 0)
    def _():
        x = _from_row_tiles(x_ref, 0, BM)
        row = lax.broadcasted_iota(I32, (BM, 1), 0)
        xb = jnp.where(row < valid, x, 0.0).astype(BF16)
        hu = jnp.dot(xb, wup_b[...], preferred_element_type=F32) + bup_ref[0]
        gate = jnp.minimum(hu[:, 0:D_FF], SWIGLU_LIMIT)
        lin = jnp.clip(hu[:, D_FF:], -SWIGLU_LIMIT, SWIGLU_LIMIT)
        act = gate * jax.nn.sigmoid(SWIGLU_ALPHA * gate) * (lin + 1.0)
        y = jnp.dot(act.astype(BF16), wdn_b[...], preferred_element_type=F32) + bdn_ref[0]
        _to_row_tiles(o_ref, y, BM)

    @pl.when(valid == 0)
    def _():
        o_ref[...] = jnp.zeros_like(o_ref)


def _experts(blk_e, blk_v, xin, w_up, b_up, w_down, b_down, n_blk):
    return pl.pallas_call(
        _expert_kernel,
        grid_spec=pltpu.PrefetchScalarGridSpec(
            num_scalar_prefetch=2, grid=(n_blk,),
            in_specs=[pl.BlockSpec((BM * ROW_TILES, LANES), lambda i, be, bv: (i, 0)),
                      pl.BlockSpec(memory_space=pl.ANY),
                      pl.BlockSpec((1, 1, 2 * D_FF), lambda i, be, bv: (be[i], 0, 0)),
                      pl.BlockSpec(memory_space=pl.ANY),
                      pl.BlockSpec((1, 1, D_MODEL), lambda i, be, bv: (be[i], 0, 0))],
            out_specs=pl.BlockSpec((BM * ROW_TILES, LANES), lambda i, be, bv: (i, 0)),
            scratch_shapes=[pltpu.VMEM((2, D_MODEL, 2 * D_FF), F32),
                            pltpu.VMEM((2, D_FF, D_MODEL), F32),
                            pltpu.VMEM((D_MODEL, 2 * D_FF), BF16),
                            pltpu.VMEM((D_FF, D_MODEL), BF16),
                            pltpu.SemaphoreType.DMA((2, 2))]),
        out_shape=jax.ShapeDtypeStruct((n_blk * BM * ROW_TILES, LANES), F32),
        compiler_params=pltpu.CompilerParams(dimension_semantics=("arbitrary",),
                                             vmem_limit_bytes=VMEM_LIMIT_BYTES),
        name="experts",
    )(blk_e, blk_v, xin, w_up, b_up.reshape(N_EXPERTS, 1, 2 * D_FF), w_down,
      b_down.reshape(N_EXPERTS, 1, D_MODEL))


COMB_KC = COMB_ROWS // 256
COMB_KC_MIN = TM * TOP_K // 256


def _combine_kernel(tile0, n_ref, s_ref, x1_ref, rcol_ref, mod_ref, g_ref, b_ref, eo_ref, y_ref,
                    stg_ref, acc_ref, cnt_ref, sem):
    i = pl.program_id(0)
    nt = pl.num_programs(0)
    slot = i % 2
    chunk = COMB_CHUNK * ROW_TILES

    def copy(src_row, dst_row, sl):
        return pltpu.make_async_copy(
            eo_ref.at[pl.ds(pl.multiple_of(src_row * ROW_TILES, ROW_TILES), chunk)],
            stg_ref.at[sl, pl.ds(pl.multiple_of(dst_row * ROW_TILES, ROW_TILES), chunk)], sem.at[sl])

    def issue(tile, sl):
        def per_expert(e, carry):
            b, total = carry
            n = n_ref[tile * N_EXPERTS + e]
            s = s_ref[tile * N_EXPERTS + e]
            nch = (n + COMB_CHUNK - 1) // COMB_CHUNK

            def per_chunk(j, c):
                copy(s + j * COMB_CHUNK, b + j * COMB_CHUNK, sl).start()
                return c
            lax.fori_loop(0, nch, per_chunk, 0)
            return b + nch * COMB_CHUNK, total + nch
        _, total = lax.fori_loop(0, N_EXPERTS, per_expert, (0, 0))
        cnt_ref[sl] = total

    @pl.when(i == 0)
    def _():
        stg_ref[...] = jnp.zeros_like(stg_ref)
        issue(tile0, 0)

    rc = rcol_ref[...]
    col_k = [rc[:, 2 * TOP_K + k:2 * TOP_K + k + 1].astype(I32) for k in range(TOP_K)]
    w_k = [rc[:, 3 * TOP_K + k:3 * TOP_K + k + 1] for k in range(TOP_K)]

    def chunk_dot(sl, c):
        jl = lax.broadcasted_iota(I32, (TM, 256), 1) + c * 256
        pm = jnp.zeros((TM, 256), F32)
        for k in range(TOP_K):
            pm = pm + jnp.where(jl == col_k[k], w_k[k], 0.0)
        rows = _from_row_tiles(stg_ref.at[sl], c * 256, 256).astype(BF16)
        return jnp.dot(pm.astype(BF16), rows, preferred_element_type=F32)

    for sl in range(2):
        @pl.when(slot == sl)
        def _():
            @pl.when(i + 1 < nt)
            def _():
                issue(tile0 + i + 1, 1 - sl)

            total = cnt_ref[sl]

            def wait_body(_, c):
                copy(0, 0, sl).wait()
                return c
            lax.fori_loop(0, total, wait_body, 0)

            y = chunk_dot(sl, 0)
            for c in range(1, COMB_KC_MIN):
                y = y + chunk_dot(sl, c)
            acc_ref[...] = y
            used = (total * COMB_CHUNK + 255) // 256
            for c in range(COMB_KC_MIN, COMB_KC):
                @pl.when(c < used)
                def _():
                    acc_ref[...] += chunk_dot(sl, c)

    mod = mod_ref[0]
    g2 = mod[:, 5 * D_MODEL:6 * D_MODEL]
    y_ref[...] = _layer_norm(DN_ALPHA * x1_ref[...] + g2 * acc_ref[...], g_ref[...], b_ref[...])


def _combine(seqs, tile0, n_tiles, n_te, s_te, x1, rcol, mod3, ln2_g, ln2_b, eo):
    def mod_map(i, *_):
        return (_tile_pos(seqs, (i + tile0) * TM)[2], 0, 0)

    return pl.pallas_call(
        functools.partial(_combine_kernel, tile0),
        grid_spec=pltpu.PrefetchScalarGridSpec(
            num_scalar_prefetch=2, grid=(n_tiles,),
            in_specs=[pl.BlockSpec((TM, D_MODEL), lambda i, *_: (i + tile0, 0)),
                      pl.BlockSpec((TM, LANES), lambda i, *_: (i + tile0, 0)),
                      pl.BlockSpec((1, 1, 6 * D_MODEL), mod_map),
                      pl.BlockSpec((1, D_MODEL), lambda i, *_: (0, 0)),
                      pl.BlockSpec((1, D_MODEL), lambda i, *_: (0, 0)),
                      pl.BlockSpec(memory_space=pl.ANY)],
            out_specs=pl.BlockSpec((TM, D_MODEL), lambda i, *_: (i, 0)),
            scratch_shapes=[pltpu.VMEM((2, COMB_ROWS * ROW_TILES, LANES), F32),
                            pltpu.VMEM((TM, D_MODEL), F32),
                            pltpu.SMEM((2,), I32),
                            pltpu.SemaphoreType.DMA((2,))]),
        out_shape=jax.ShapeDtypeStruct((n_tiles * TM, D_MODEL), F32),
        compiler_params=pltpu.CompilerParams(dimension_semantics=("arbitrary",),
                                             vmem_limit_bytes=VMEM_LIMIT_BYTES),
        name="combine",
    )(n_te, s_te, x1, rcol, mod3, ln2_g, ln2_b, eo)


def _prep_params(w_in, b_in, w_spatial, b_spatial, sgu_ln_g, sgu_ln_b, w_br_attn, w_br_sgu, w_out,
                 ln1_g, ln1_b, w_router, b_router):
    q_end, k_end, v_end = ATT_W, ATT_W + KV_W, ATT_W + 2 * KV_W

    def dup(w, lo):
        h0, h1 = w[..., lo:lo + HEAD_DIM], w[..., lo + HEAD_DIM:lo + 2 * HEAD_DIM]
        return jnp.concatenate([h0, h0, h1, h1], axis=-1)

    w_kv = jnp.concatenate([dup(w_in, q_end), dup(w_in, k_end)], axis=1).astype(BF16)
    b_kv = jnp.concatenate([dup(b_in, q_end), dup(b_in, k_end)], axis=0).reshape(1, -1)
    w_mix = jnp.concatenate([w_in[:, :q_end], w_in[:, v_end:]], axis=1).astype(BF16)
    b_mix = jnp.concatenate([b_in[:q_end], b_in[v_end:]], axis=0).reshape(1, -1)
    w_hi = w_router.astype(BF16)
    w_lo = (w_router - w_hi.astype(F32)).astype(BF16)
    p = dict(
        w_mix=w_mix, b_mix=b_mix,
        w_sp=w_spatial.astype(BF16), b_spt=b_spatial.T,
        sgu_g=sgu_ln_g.reshape(1, -1), sgu_b=sgu_ln_b.reshape(1, -1),
        w_ba=w_br_attn.astype(BF16), w_bs=w_br_sgu.astype(BF16), w_out=w_out.astype(BF16),
        ln1_g=ln1_g.reshape(1, -1), ln1_b=ln1_b.reshape(1, -1),
        w_r=jnp.concatenate([w_hi, w_lo], axis=1), b_r=b_router.reshape(1, -1),
    )
    return w_kv, b_kv, p


def _layer(seqs, xp, xs, c_all, w_ada, b_ada, w_in, b_in, sink, sgu_ln_g, sgu_ln_b, w_spatial, b_spatial,
           w_br_attn, w_br_sgu, w_out, ln1_g, ln1_b, w_router, b_router, w_up, b_up, w_down, b_down,
           ln2_g, ln2_b):
    T = seqs.n_tokens
    nt = T // TM
    ntp = seqs.n_prompt // TM
    c_pad = jnp.zeros((8, D_MODEL), F32).at[:c_all.shape[0]].set(c_all)
    mod3 = _ada(c_pad, w_ada, b_ada).reshape(8, 1, 6 * D_MODEL)
    cos_t, sin_t = _rope_tables(max(seqs.prompt_len, seqs.sample_len))
    w_kv, b_kv, p = _prep_params(w_in, b_in, w_spatial, b_spatial, sgu_ln_g, sgu_ln_b, w_br_attn,
                                 w_br_sgu, w_out, ln1_g, ln1_b, w_router, b_router)
    kt, v = _kv(seqs, xp, xs, mod3, cos_t, sin_t, w_kv, b_kv)
    x1, h2, rcol, rrow, cnt3 = _mixer(seqs, sink, xp, xs, mod3, cos_t, sin_t, kt, v, p)

    cnt = cnt3[:, 0, :N_EXPERTS]
    count = cnt.sum(0)
    reg = (count + REGION_SLACK + BM - 1) // BM * BM
    pad_end = jnp.cumsum(reg)
    pad_start = pad_end - reg
    s_te = (pad_start[None, :] + jnp.cumsum(cnt, axis=0) - cnt).reshape(-1).astype(I32)
    n_te = cnt.reshape(-1).astype(I32)
    n_blk = (T * TOP_K + N_EXPERTS * (REGION_SLACK + BM - 1)) // BM + 1
    tails = jnp.concatenate([pad_start + count, pad_end[:-1], jnp.array([n_blk * BM])]).astype(I32)
    blk_start = jnp.arange(n_blk, dtype=I32) * BM
    blk_e = jnp.minimum((blk_start[:, None] >= pad_end[None, :]).sum(1), N_EXPERTS - 1).astype(I32)
    owner = blk_e[:, None] == jnp.arange(N_EXPERTS)[None, :]
    blk_rows = (owner * (pad_start + count)[None, :]).sum(1) - blk_start
    blk_v = jnp.where(blk_start < pad_end[-1], jnp.clip(blk_rows, 0, BM), 0).astype(I32)

    xin = _dispatch(seqs, n_te, s_te, tails, h2, rrow, n_blk * BM)
    eo = _experts(blk_e, blk_v, xin, w_up, b_up, w_down, b_down, n_blk)
    g2, b2 = ln2_g.reshape(1, -1), ln2_b.reshape(1, -1)
    return (_combine(seqs, 0, ntp, n_te, s_te, x1, rcol, mod3, g2, b2, eo),
            _combine(seqs, ntp, nt - ntp, n_te, s_te, x1, rcol, mod3, g2, b2, eo))


def kernel(x_prompt, x_sample, c_prompt, c_sample, w_ada, b_ada, w_in, b_in, sink, sgu_ln_g, sgu_ln_b, w_spatial, b_spatial, w_br_attn, w_br_sgu, w_out, ln1_g, ln1_b, w_router, b_router, w_up, b_up, w_down, b_down, ln2_g, ln2_b):
    assert w_ada.shape[0] == DEPTH == 1
    bp, sp, d = x_prompt.shape
    bs, ss, _ = x_sample.shape
    seqs = _Seqs(n_prompt=bp * sp, prompt_len=sp, sample_len=ss, n_tokens=bp * sp + bs * ss)
    c_all = jnp.concatenate([c_prompt, c_sample], axis=0)
    yp, ys = _layer(seqs, x_prompt.reshape(bp * sp, d), x_sample.reshape(bs * ss, d), c_all,
                    w_ada[0], b_ada[0], w_in[0], b_in[0], sink[0], sgu_ln_g[0], sgu_ln_b[0],
                    w_spatial[0], b_spatial[0], w_br_attn[0], w_br_sgu[0], w_out[0], ln1_g[0], ln1_b[0],
                    w_router[0], b_router[0], w_up[0], b_up[0], w_down[0], b_down[0], ln2_g[0], ln2_b[0])
    return (yp.reshape(bp, sp, d), ys.reshape(bs, ss, d))
```

```python
import functools
import math
from typing import NamedTuple

import jax
import jax.numpy as jnp
from jax import lax
from jax.experimental import pallas as pl
from jax.experimental.pallas import tpu as pltpu

F32 = jnp.float32
BF16 = jnp.bfloat16
I32 = jnp.int32

D_MODEL = 1024
N_HEADS = 8
N_KV_HEADS = 2
HEAD_DIM = 64
ATT_W = N_HEADS * HEAD_DIM
KV_W = N_KV_HEADS * HEAD_DIM
BLOCK = 128
ROPE_THETA = 500000.0
ROT_DIM = HEAD_DIM // 4
ROT_HALF = ROT_DIM // 2
SGU_W = D_MODEL // 2
SGU_GROUPS = 4
N_EXPERTS = 32
TOP_K = 4
D_FF = D_MODEL
SWIGLU_LIMIT = 7.0
SWIGLU_ALPHA = 1.702
LN_EPS = 1e-5
DEPTH = 1
DN_ALPHA = (2 * DEPTH) ** 0.25

LANES = 128
SUBLANES = 8
ROW_TILES = D_MODEL // LANES
VMEM_LIMIT_BYTES = 56 * 1024 * 1024
U32 = jnp.uint32
PAIR = 2

TM = 256
TK = 512
BM = 256
ROUTER_TOKENS = 2048
DISP_CHUNK = 32
COMB_CHUNK = 16
REGION_SLACK = DISP_CHUNK
TILE_ROWS = TM * TOP_K + N_EXPERTS
DISP_ROWS = TILE_ROWS + DISP_CHUNK
COMB_ROWS = ((TILE_ROWS + N_EXPERTS * (COMB_CHUNK - 1)) + 255) // 256 * 256
NEG_INF = float("-inf")


class _Seqs(NamedTuple):
    n_prompt: int
    prompt_len: int
    sample_len: int
    n_tokens: int


def _tile_pos(seqs, t0):
    is_s = t0 >= seqs.n_prompt
    seq_len = jnp.where(is_s, seqs.sample_len, seqs.prompt_len)
    off = jnp.where(is_s, t0 - seqs.n_prompt, t0)
    pos0 = off % seq_len
    row = jnp.where(is_s, seqs.n_prompt // seqs.prompt_len + off // seq_len, off // seq_len)
    return seq_len, pos0, row


def _ada_kernel(c_ref, w_ref, b_ref, o_ref):
    c = c_ref[...]
    a = c * jax.nn.sigmoid(c)
    o_ref[...] = jnp.dot(a, w_ref[...], preferred_element_type=F32,
                         precision=lax.Precision.HIGHEST) + b_ref[...]


def _ada(c_pad, w_ada, b_ada):
    n = w_ada.shape[1]
    bn = 1536
    return pl.pallas_call(
        _ada_kernel,
        grid=(n // bn,),
        in_specs=[pl.BlockSpec((8, D_MODEL), lambda j: (0, 0)),
                  pl.BlockSpec((D_MODEL, bn), lambda j: (0, j)),
                  pl.BlockSpec((1, bn), lambda j: (0, j))],
        out_specs=pl.BlockSpec((8, bn), lambda j: (0, j)),
        out_shape=jax.ShapeDtypeStruct((8, n), F32),
        compiler_params=pltpu.CompilerParams(vmem_limit_bytes=VMEM_LIMIT_BYTES),
        name="ada",
    )(c_pad, w_ada, b_ada.reshape(1, n))


def _rope_tables(length):
    lane = jnp.arange(LANES) % HEAD_DIM
    inv = ROPE_THETA ** (-(lane % ROT_HALF).astype(F32) * 2.0 / ROT_DIM)
    inv = jnp.where(lane < ROT_DIM, inv, 0.0)
    sign = jnp.where(lane < ROT_HALF, -1.0, 1.0)
    ang = jnp.arange(length, dtype=F32)[:, None] * inv[None, :]
    return jnp.cos(ang), jnp.sin(ang) * sign[None, :]


def _rope(x, cos, sin):
    n = x.shape[1]
    reps = n // LANES
    c = jnp.concatenate([cos] * reps, axis=1)
    s = jnp.concatenate([sin] * reps, axis=1)
    lane = lax.broadcasted_iota(I32, x.shape, 1)
    first = (lane & (HEAD_DIM - 1)) < ROT_HALF
    partner = jnp.where(first, pltpu.roll(x, n - ROT_HALF, 1), pltpu.roll(x, ROT_HALF, 1))
    return x * c + partner * s


def _kv_kernel(seqs, xp_ref, xs_ref, mod_ref, cos_ref, sin_ref, w_ref, b_ref, kt_ref, v_ref):
    mod = mod_ref[0]
    sh1 = mod[:, 0:D_MODEL]
    sc1 = mod[:, D_MODEL:2 * D_MODEL]
    x = jnp.where(pl.program_id(0) * TK >= seqs.n_prompt, xs_ref[...], xp_ref[...])
    h = (x * (1.0 + sc1) + sh1).astype(BF16)
    kv = jnp.dot(h, w_ref[...], preferred_element_type=F32) + b_ref[...]
    k = _rope(kv[:, 0:2 * LANES], cos_ref[...], sin_ref[...])
    kt_ref[...] = k.T.astype(BF16)
    v_ref[...] = kv[:, 2 * LANES:4 * LANES].astype(BF16)


def _group_x_specs(seqs, tile):
    ntp = seqs.n_prompt // tile
    return [pl.BlockSpec((tile, D_MODEL), lambda i, *_: (jnp.minimum(i, ntp - 1), 0)),
            pl.BlockSpec((tile, D_MODEL), lambda i, *_: (jnp.maximum(i - ntp, 0), 0))]


def _kv(seqs, xp, xs, mod3, cos_t, sin_t, w_kv, b_kv):
    T = seqs.n_tokens

    def mod_map(i):
        return (_tile_pos(seqs, i * TK)[2], 0, 0)

    def rope_map(i):
        return (_tile_pos(seqs, i * TK)[1] // TK, 0)

    return pl.pallas_call(
        functools.partial(_kv_kernel, seqs),
        grid=(T // TK,),
        in_specs=_group_x_specs(seqs, TK) + [
                  pl.BlockSpec((1, 1, 6 * D_MODEL), mod_map),
                  pl.BlockSpec((TK, LANES), rope_map),
                  pl.BlockSpec((TK, LANES), rope_map),
                  pl.BlockSpec((D_MODEL, 4 * LANES), lambda i: (0, 0)),
                  pl.BlockSpec((1, 4 * LANES), lambda i: (0, 0))],
        out_specs=[pl.BlockSpec((2 * LANES, TK), lambda i: (0, i)),
                   pl.BlockSpec((TK, 2 * LANES), lambda i: (i, 0))],
        out_shape=[jax.ShapeDtypeStruct((2 * LANES, T), BF16),
                   jax.ShapeDtypeStruct((T, 2 * LANES), BF16)],
        compiler_params=pltpu.CompilerParams(dimension_semantics=("arbitrary",),
                                             vmem_limit_bytes=VMEM_LIMIT_BYTES),
        name="kv",
    )(xp, xs, mod3, cos_t, sin_t, w_kv, b_kv)


def _layer_norm(x, g, b):
    mu = jnp.mean(x, axis=-1, keepdims=True)
    xc = x - mu
    var = jnp.mean(xc * xc, axis=-1, keepdims=True)
    return xc * lax.rsqrt(var + LN_EPS) * g + b


def _attention(q, kfull, vfull, valids, sink_ref):
    lane = lax.broadcasted_iota(I32, (BLOCK, LANES), 1)
    lo = lane < HEAD_DIM
    ones = jnp.ones((3 * BLOCK, LANES), BF16)
    units = [(jb, hk) for jb in range(TM // BLOCK) for hk in range(N_KV_HEADS)]
    scores, sinks = [], []
    for jb, hk in units:
        parts = []
        for p in range(2):
            qp = q[jb * BLOCK:(jb + 1) * BLOCK, (2 * hk + p) * LANES:(2 * hk + p + 1) * LANES]
            parts.append(jnp.where(lo, qp, 0.0).astype(BF16))
            parts.append(jnp.where(lo, 0.0, qp).astype(BF16))
        kwin = kfull[hk * LANES:(hk + 1) * LANES, jb * BLOCK:(jb + 3) * BLOCK]
        s = jnp.dot(jnp.concatenate(parts, axis=0), kwin, preferred_element_type=F32)
        scores.append(jnp.where(jnp.concatenate([valids[jb]] * 4, axis=0), s, NEG_INF))
        sinks.extend(jnp.full((BLOCK, 1), sink_ref[hk * 4 + g], F32) for g in range(4))
    s = jnp.concatenate(scores, axis=0)
    sk = jnp.concatenate(sinks, axis=0)
    m = jnp.maximum(jnp.max(s, axis=-1, keepdims=True), sk)
    p = jnp.exp(s - m).astype(BF16)
    sink_term = jnp.exp(sk - m)
    rows = []
    for ui, (jb, hk) in enumerate(units):
        vwin = jnp.concatenate([vfull[jb * BLOCK:(jb + 3) * BLOCK, hk * LANES:(hk + 1) * LANES], ones], axis=1)
        r0 = ui * 4 * BLOCK
        ov = jnp.dot(p[r0:r0 + 4 * BLOCK], vwin, preferred_element_type=F32)
        o = ov[:, 0:LANES] / (ov[:, LANES:2 * LANES] + sink_term[r0:r0 + 4 * BLOCK])
        pair = [jnp.where(lo, o[(2 * p2) * BLOCK:(2 * p2 + 1) * BLOCK],
                          o[(2 * p2 + 1) * BLOCK:(2 * p2 + 2) * BLOCK]) for p2 in range(2)]
        rows.append(jnp.concatenate(pair, axis=1))
    n_h = N_KV_HEADS
    return jnp.concatenate(
        [jnp.concatenate(rows[jb * n_h:(jb + 1) * n_h], axis=1) for jb in range(TM // BLOCK)], axis=0)


def _mixer_kernel(seqs, sink_ref, xp_ref, xs_ref, mod_ref, cos_ref, sin_ref,
                  ktp_ref, ktc_ref, ktn_ref, vp_ref, vc_ref, vn_ref,
                  wmix_ref, bmix_ref, wsp_ref, bspt_ref, sg_ref, sb_ref,
                  wba_ref, wbs_ref, wout_ref, l1g_ref, l1b_ref, wr_ref, br_ref,
                  x1_ref, h2_ref, logit_ref):
    i = pl.program_id(0)
    seq_len, pos0, _ = _tile_pos(seqs, i * TM)
    mod = mod_ref[0]
    sh1, sc1, g1 = (mod[:, j * D_MODEL:(j + 1) * D_MODEL] for j in range(3))
    sh2, sc2 = (mod[:, j * D_MODEL:(j + 1) * D_MODEL] for j in range(3, 5))
    x = jnp.where(i * TM >= seqs.n_prompt, xs_ref[...], xp_ref[...])
    h = (x * (1.0 + sc1) + sh1).astype(BF16)
    z = jnp.dot(h, wmix_ref[...], preferred_element_type=F32) + bmix_ref[...]
    q = _rope(z[:, 0:ATT_W], cos_ref[...], sin_ref[...]) * (HEAD_DIM ** -0.5)
    u = jax.nn.gelu(z[:, ATT_W:ATT_W + SGU_W])
    vs = _layer_norm(jax.nn.gelu(z[:, ATT_W + SGU_W:ATT_W + 2 * SGU_W]), sg_ref[...], sb_ref[...])
    ga = z[:, ATT_W + 2 * SGU_W:ATT_W + 2 * SGU_W + D_MODEL]
    gs = z[:, ATT_W + 2 * SGU_W + D_MODEL:]

    kfull = jnp.concatenate([ktp_ref[...], ktc_ref[...], ktn_ref[...]], axis=1)
    vfull = jnp.concatenate([vp_ref[...], vc_ref[...], vn_ref[...]], axis=0)
    qi = lax.broadcasted_iota(I32, (BLOCK, 3 * BLOCK), 0)
    ki = lax.broadcasted_iota(I32, (BLOCK, 3 * BLOCK), 1)
    band = (ki >= qi) & (ki <= qi + 2 * BLOCK)
    vs_b = vs.astype(BF16)
    valids, sgu_rows = [], []
    for jb in range(TM // BLOCK):
        posb = pos0 + jb * BLOCK
        valids.append(band & (ki >= jnp.where(posb == 0, BLOCK, 0))
                      & (ki < jnp.where(posb + BLOCK == seq_len, 2 * BLOCK, 3 * BLOCK)))
        groups = []
        for g in range(SGU_GROUPS):
            vg = vs_b[jb * BLOCK:(jb + 1) * BLOCK, g * LANES:(g + 1) * LANES]
            sv = jnp.dot(wsp_ref[g], vg, preferred_element_type=F32) + bspt_ref[:, g:g + 1]
            groups.append(sv)
        sgu_rows.append(jnp.concatenate(groups, axis=1))
    attn = _attention(q, kfull, vfull, valids, sink_ref)
    sgu = u * jnp.concatenate(sgu_rows, axis=0)

    a1 = jnp.dot(attn.astype(BF16), wba_ref[...], preferred_element_type=F32)
    a2 = jnp.dot(sgu.astype(BF16), wbs_ref[...], preferred_element_type=F32)
    merged = jax.nn.sigmoid(ga) * a1 + jax.nn.sigmoid(gs) * a2
    mix = jnp.dot(merged.astype(BF16), wout_ref[...], preferred_element_type=F32)
    x1 = _layer_norm(DN_ALPHA * x + g1 * mix, l1g_ref[...], l1b_ref[...])
    x1_ref[...] = x1
    h2 = x1 * (1.0 + sc2) + sh2
    hi = h2.astype(BF16)
    h2_ref[...] = hi

    lo_part = (h2 - hi.astype(F32)).astype(BF16)
    l1 = jnp.dot(hi, wr_ref[...], preferred_element_type=F32)
    l2 = jnp.dot(lo_part, wr_ref[:, 0:N_EXPERTS], preferred_element_type=F32)
    logit_ref[...] = l1[:, 0:N_EXPERTS] + l1[:, N_EXPERTS:2 * N_EXPERTS] + l2 + br_ref[...]


PACK = 2048.0


def _router_kernel(l_ref, rcol_ref, rrow_ref, cnt_ref):
    rt = l_ref.shape[0]
    ns = rt // TM
    eidx = lax.broadcasted_iota(I32, (rt, N_EXPERTS), 1)
    work = l_ref[...]
    idxs, vals = [], []
    for _ in range(TOP_K):
        m = jnp.max(work, axis=-1, keepdims=True)
        ix = jnp.min(jnp.where(work == m, eidx, N_EXPERTS), axis=-1, keepdims=True)
        idxs.append(ix)
        vals.append(m)
        work = jnp.where(eidx == ix, NEG_INF, work)
    exps = [jnp.exp(v - vals[0]) for v in vals]
    esum = exps[0] + exps[1] + exps[2] + exps[3]
    wts = [e / esum for e in exps]

    sel = jnp.zeros((rt, N_EXPERTS), F32)
    for ix in idxs:
        sel = sel + jnp.where(eidx == ix, 1.0, 0.0)
    sel_b = sel.astype(BF16)
    ti = lax.broadcasted_iota(I32, (TM, TM), 0)
    tj = lax.broadcasted_iota(I32, (TM, TM), 1)
    lower = jnp.where(tj < ti, 1.0, 0.0).astype(BF16)
    tiles = [slice(s * TM, (s + 1) * TM) for s in range(ns)]
    rank = jnp.concatenate([jnp.dot(lower, sel_b[t], preferred_element_type=F32) for t in tiles], axis=0)
    cnt = jnp.concatenate([jnp.sum(sel[t], axis=0, keepdims=True) for t in tiles], axis=0)
    cnt = cnt + (cnt - 2.0 * jnp.floor(cnt * 0.5))
    nch = jnp.floor((cnt + (COMB_CHUNK - 1)) * (1.0 / COMB_CHUNK))
    ei = lax.broadcasted_iota(I32, (N_EXPERTS, N_EXPERTS), 0)
    ej = lax.broadcasted_iota(I32, (N_EXPERTS, N_EXPERTS), 1)
    before = jnp.where(ei < ej, 1.0, 0.0).astype(BF16)
    fill = (-2 * ns) % 16
    pre = jnp.concatenate([cnt, nch] + ([jnp.zeros((fill, N_EXPERTS), F32)] if fill else []),
                          axis=0).astype(BF16)
    base = jnp.dot(pre, before, preferred_element_type=F32)
    dbase = jnp.concatenate([jnp.broadcast_to(base[s:s + 1], (TM, N_EXPERTS)) for s in range(ns)], axis=0)
    cbase = jnp.concatenate([jnp.broadcast_to(base[ns + s:ns + s + 1] * COMB_CHUNK, (TM, N_EXPERTS))
                             for s in range(ns)], axis=0)
    both = (rank + dbase) * PACK + (rank + cbase)

    cols = [ix.astype(F32) for ix in idxs]
    packed = [jnp.sum(jnp.where(eidx == ix, both, 0.0), axis=-1, keepdims=True) for ix in idxs]
    drow = [jnp.floor(v * (1.0 / PACK)) for v in packed]
    cols.extend(drow)
    cols.extend(v - d * PACK for v, d in zip(packed, drow))
    cols.extend(wts)
    lane = lax.broadcasted_iota(I32, (rt, LANES), 1)
    rc = jnp.zeros((rt, LANES), F32)
    for j, c in enumerate(cols):
        rc = jnp.where(lane == j, c, rc)
    rcol_ref[...] = rc
    for s in range(ns):
        rrow_ref[s] = rc[tiles[s]].T[0:16]
        cnt_ref[s] = jnp.broadcast_to(
            jnp.concatenate([cnt[s:s + 1], jnp.zeros((1, LANES - N_EXPERTS), F32)], axis=1),
            (8, LANES)).astype(I32)


def _router(logits):
    T = logits.shape[0]
    rt = math.gcd(T, ROUTER_TOKENS)
    nt = T // TM
    return pl.pallas_call(
        _router_kernel,
        grid=(T // rt,),
        in_specs=[pl.BlockSpec((rt, N_EXPERTS), lambda i: (i, 0))],
        out_specs=[pl.BlockSpec((rt, LANES), lambda i: (i, 0)),
                   pl.BlockSpec((rt // TM, 16, TM), lambda i: (i, 0, 0)),
                   pl.BlockSpec((rt // TM, 8, LANES), lambda i: (i, 0, 0))],
        out_shape=[jax.ShapeDtypeStruct((T, LANES), F32),
                   jax.ShapeDtypeStruct((nt, 16, TM), F32),
                   jax.ShapeDtypeStruct((nt, 8, LANES), I32)],
        compiler_params=pltpu.CompilerParams(dimension_semantics=("arbitrary",),
                                             vmem_limit_bytes=VMEM_LIMIT_BYTES),
        name="router",
    )(logits)


def _mixer(seqs, sink, xp, xs, mod3, cos_t, sin_t, kt, v, p):
    T = seqs.n_tokens
    nt = T // TM
    nb = T // BLOCK
    r = TM // BLOCK

    def mod_map(i, s):
        return (_tile_pos(seqs, i * TM)[2], 0, 0)

    def rope_map(i, s):
        return (_tile_pos(seqs, i * TM)[1] // TM, 0)

    const2 = lambda i, s: (0, 0)
    in_specs = _group_x_specs(seqs, TM) + [
        pl.BlockSpec((1, 1, 6 * D_MODEL), mod_map),
        pl.BlockSpec((TM, LANES), rope_map),
        pl.BlockSpec((TM, LANES), rope_map),
        pl.BlockSpec((2 * LANES, BLOCK), lambda i, s: (0, jnp.maximum(i * r - 1, 0))),
        pl.BlockSpec((2 * LANES, TM), lambda i, s: (0, i)),
        pl.BlockSpec((2 * LANES, BLOCK), lambda i, s: (0, jnp.minimum(i * r + r, nb - 1))),
        pl.BlockSpec((BLOCK, 2 * LANES), lambda i, s: (jnp.maximum(i * r - 1, 0), 0)),
        pl.BlockSpec((TM, 2 * LANES), lambda i, s: (i, 0)),
        pl.BlockSpec((BLOCK, 2 * LANES), lambda i, s: (jnp.minimum(i * r + r, nb - 1), 0)),
        pl.BlockSpec(p["w_mix"].shape, const2),
        pl.BlockSpec(p["b_mix"].shape, const2),
        pl.BlockSpec(p["w_sp"].shape, lambda i, s: (0, 0, 0)),
        pl.BlockSpec(p["b_spt"].shape, const2),
        pl.BlockSpec(p["sgu_g"].shape, const2),
        pl.BlockSpec(p["sgu_b"].shape, const2),
        pl.BlockSpec(p["w_ba"].shape, const2),
        pl.BlockSpec(p["w_bs"].shape, const2),
        pl.BlockSpec(p["w_out"].shape, const2),
        pl.BlockSpec(p["ln1_g"].shape, const2),
        pl.BlockSpec(p["ln1_b"].shape, const2),
        pl.BlockSpec(p["w_r"].shape, const2),
        pl.BlockSpec(p["b_r"].shape, const2),
    ]
    out_specs = [
        pl.BlockSpec((TM, D_MODEL), lambda i, s: (i, 0)),
        pl.BlockSpec((TM, D_MODEL), lambda i, s: (i, 0)),
        pl.BlockSpec((TM, N_EXPERTS), lambda i, s: (i, 0)),
    ]
    out_shape = [
        jax.ShapeDtypeStruct((T, D_MODEL), F32),
        jax.ShapeDtypeStruct((T, D_MODEL), BF16),
        jax.ShapeDtypeStruct((T, N_EXPERTS), F32),
    ]
    return pl.pallas_call(
        functools.partial(_mixer_kernel, seqs),
        grid_spec=pltpu.PrefetchScalarGridSpec(
            num_scalar_prefetch=1, grid=(nt,), in_specs=in_specs, out_specs=out_specs),
        out_shape=out_shape,
        compiler_params=pltpu.CompilerParams(dimension_semantics=("arbitrary",),
                                             vmem_limit_bytes=VMEM_LIMIT_BYTES),
        name="mixer",
    )(sink, xp, xs, mod3, cos_t, sin_t, kt, kt, kt, v, v, v,
      p["w_mix"], p["b_mix"], p["w_sp"], p["b_spt"], p["sgu_g"], p["sgu_b"],
      p["w_ba"], p["w_bs"], p["w_out"], p["ln1_g"], p["ln1_b"], p["w_r"], p["b_r"])


def _pair_rows(row):
    return pl.multiple_of((row // PAIR) * ROW_TILES, ROW_TILES)


def _to_row_tiles(dst_ref, rows, n):
    words = pltpu.bitcast(rows, U32)
    for c in range(ROW_TILES):
        dst_ref[pl.ds(c, n // PAIR, stride=ROW_TILES), :] = words[:, c * LANES:(c + 1) * LANES]


def _from_row_tiles(src_ref, start, n):
    base = start // PAIR * ROW_TILES
    words = jnp.concatenate(
        [src_ref[pl.ds(base + c, n // PAIR, stride=ROW_TILES), :] for c in range(ROW_TILES)], axis=1)
    return pltpu.bitcast(words, BF16)


def _dispatch_kernel(n_ref, s_ref, tail_ref, h2_ref, rrow_ref, xin_ref, stg_ref, zero_ref, cnt_ref, sem):
    i = pl.program_id(0)
    nt = pl.num_programs(0)
    slot = i % 2
    rr = rrow_ref[0]
    rho = lax.broadcasted_iota(I32, (DISP_ROWS, TM), 0)
    pt = jnp.zeros((DISP_ROWS, TM), F32)
    for k in range(TOP_K):
        pt = pt + jnp.where(rho == rr[TOP_K + k:TOP_K + k + 1].astype(I32), 1.0, 0.0)
    rows = jnp.dot(pt.astype(BF16), h2_ref[...], preferred_element_type=F32).astype(BF16)

    chunk = DISP_CHUNK // PAIR * ROW_TILES

    def copy(src_row, dst_row, sl):
        return pltpu.make_async_copy(stg_ref.at[sl, pl.ds(_pair_rows(src_row), chunk)],
                                     xin_ref.at[pl.ds(_pair_rows(dst_row), chunk)], sem)

    def wait_all(count):
        def body(_, c):
            copy(0, 0, 0).wait()
            return c
        lax.fori_loop(0, count, body, 0)

    def issue(sl):
        def per_expert(e, carry):
            b, total = carry
            n = n_ref[i * N_EXPERTS + e]
            s = s_ref[i * N_EXPERTS + e]
            nch = (n + DISP_CHUNK - 1) // DISP_CHUNK

            def per_chunk(j, c):
                copy(b + j * DISP_CHUNK, s + j * DISP_CHUNK, sl).start()
                return c
            lax.fori_loop(0, nch, per_chunk, 0)
            return b + n, total + nch
        _, total = lax.fori_loop(0, N_EXPERTS, per_expert, (0, 0))
        return total

    for sl in range(2):
        @pl.when(slot == sl)
        def _():
            _to_row_tiles(stg_ref.at[sl], rows, DISP_ROWS)

            @pl.when(i > 0)
            def _():
                wait_all(cnt_ref[0])
            cnt_ref[0] = issue(sl)

    @pl.when(i == nt - 1)
    def _():
        wait_all(cnt_ref[0])
        zero_ref[...] = jnp.zeros_like(zero_ref)

        def zcopy(dst_row):
            return pltpu.make_async_copy(zero_ref, xin_ref.at[pl.ds(_pair_rows(dst_row), chunk)], sem)

        def zwait(count):
            def body(_, c):
                zcopy(0).wait()
                return c
            lax.fori_loop(0, count, body, 0)

        def per_expert(e, total):
            lo = tail_ref[e]
            nz = (tail_ref[N_EXPERTS + e] - lo) // DISP_CHUNK

            def per_chunk(j, c):
                zcopy(lo + j * DISP_CHUNK).start()
                return c
            lax.fori_loop(0, nz, per_chunk, 0)
            return total + nz
        zwait(lax.fori_loop(0, N_EXPERTS, per_expert, 0))

        def last_chunk(e, c):
            zcopy(tail_ref[N_EXPERTS + e] - DISP_CHUNK).start()
            return c
        lax.fori_loop(0, N_EXPERTS, last_chunk, 0)
        zwait(N_EXPERTS)


def _dispatch(seqs, n_te, s_te, tails, h2, rrow, n_rows):
    nt = seqs.n_tokens // TM
    return pl.pallas_call(
        _dispatch_kernel,
        grid_spec=pltpu.PrefetchScalarGridSpec(
            num_scalar_prefetch=3, grid=(nt,),
            in_specs=[pl.BlockSpec((TM, D_MODEL), lambda i, *_: (i, 0)),
                      pl.BlockSpec((1, 16, TM), lambda i, *_: (i, 0, 0))],
            out_specs=pl.BlockSpec(memory_space=pl.ANY),
            scratch_shapes=[pltpu.VMEM((2, DISP_ROWS // PAIR * ROW_TILES, LANES), U32),
                            pltpu.VMEM((DISP_CHUNK // PAIR * ROW_TILES, LANES), U32),
                            pltpu.SMEM((1,), I32),
                            pltpu.SemaphoreType.DMA]),
        out_shape=jax.ShapeDtypeStruct((n_rows // PAIR * ROW_TILES, LANES), U32),
        compiler_params=pltpu.CompilerParams(dimension_semantics=("arbitrary",),
                                             vmem_limit_bytes=VMEM_LIMIT_BYTES),
        name="dispatch",
    )(n_te, s_te, tails, h2, rrow)


def _expert_kernel(be_ref, bv_ref, x_ref, wup_hbm, bup_ref, wdn_hbm, bdn_ref, o_ref,
                   wup_f, wdn_f, wup_b, wdn_b, sem):
    i = pl.program_id(0)
    e = be_ref[i]
    valid = bv_ref[i]
    first = jnp.logical_or(i == 0, e != be_ref[jnp.maximum(i - 1, 0)])

    def fetch(ex):
        slot = ex % 2
        return (pltpu.make_async_copy(wup_hbm.at[ex], wup_f.at[slot], sem.at[0, slot]),
                pltpu.make_async_copy(wdn_hbm.at[ex], wdn_f.at[slot], sem.at[1, slot]))

    @pl.when(i == 0)
    def _():
        for d in fetch(e):
            d.start()

    @pl.when(first)
    def _():
        for d in fetch(e):
            d.wait()

        @pl.when(e + 1 < N_EXPERTS)
        def _():
            for d in fetch(e + 1):
                d.start()

        @pl.when(valid > 0)
        def _():
            wup_b[...] = wup_f[e % 2].astype(BF16)
            wdn_b[...] = wdn_f[e % 2].astype(BF16)

    @pl.when(valid > 0)
    def _():
        x = _from_row_tiles(x_ref, 0, BM)
        row = lax.broadcasted_iota(I32, (BM, 1), 0)
        xb = jnp.where(row < valid, x, jnp.zeros_like(x))
        hu = jnp.dot(xb, wup_b[...], preferred_element_type=F32) + bup_ref[0]
        gate = jnp.minimum(hu[:, 0:D_FF], SWIGLU_LIMIT)
        lin = jnp.clip(hu[:, D_FF:], -SWIGLU_LIMIT, SWIGLU_LIMIT)
        act = gate * jax.nn.sigmoid(SWIGLU_ALPHA * gate) * (lin + 1.0)
        y = jnp.dot(act.astype(BF16), wdn_b[...], preferred_element_type=F32) + bdn_ref[0]
        _to_row_tiles(o_ref, y.astype(BF16), BM)

    @pl.when(valid == 0)
    def _():
        o_ref[...] = jnp.zeros_like(o_ref)


def _experts(blk_e, blk_v, xin, w_up, b_up, w_down, b_down, n_blk):
    return pl.pallas_call(
        _expert_kernel,
        grid_spec=pltpu.PrefetchScalarGridSpec(
            num_scalar_prefetch=2, grid=(n_blk,),
            in_specs=[pl.BlockSpec((BM // PAIR * ROW_TILES, LANES), lambda i, be, bv: (i, 0)),
                      pl.BlockSpec(memory_space=pl.ANY),
                      pl.BlockSpec((1, 1, 2 * D_FF), lambda i, be, bv: (be[i], 0, 0)),
                      pl.BlockSpec(memory_space=pl.ANY),
                      pl.BlockSpec((1, 1, D_MODEL), lambda i, be, bv: (be[i], 0, 0))],
            out_specs=pl.BlockSpec((BM // PAIR * ROW_TILES, LANES), lambda i, be, bv: (i, 0)),
            scratch_shapes=[pltpu.VMEM((2, D_MODEL, 2 * D_FF), F32),
                            pltpu.VMEM((2, D_FF, D_MODEL), F32),
                            pltpu.VMEM((D_MODEL, 2 * D_FF), BF16),
                            pltpu.VMEM((D_FF, D_MODEL), BF16),
                            pltpu.SemaphoreType.DMA((2, 2))]),
        out_shape=jax.ShapeDtypeStruct((n_blk * BM // PAIR * ROW_TILES, LANES), U32),
        compiler_params=pltpu.CompilerParams(dimension_semantics=("arbitrary",),
                                             vmem_limit_bytes=VMEM_LIMIT_BYTES),
        name="experts",
    )(blk_e, blk_v, xin, w_up, b_up.reshape(N_EXPERTS, 1, 2 * D_FF), w_down,
      b_down.reshape(N_EXPERTS, 1, D_MODEL))


COMB_KC = COMB_ROWS // 256
COMB_KC_MIN = TM * TOP_K // 256


def _combine_kernel(tile0, n_ref, s_ref, x1_ref, rcol_ref, mod_ref, g_ref, b_ref, eo_ref, y_ref,
                    stg_ref, acc_ref, cnt_ref, sem):
    i = pl.program_id(0)
    nt = pl.num_programs(0)
    slot = i % 2
    chunk = COMB_CHUNK // PAIR * ROW_TILES

    def copy(src_row, dst_row, sl):
        return pltpu.make_async_copy(eo_ref.at[pl.ds(_pair_rows(src_row), chunk)],
                                     stg_ref.at[sl, pl.ds(_pair_rows(dst_row), chunk)], sem.at[sl])

    def issue(tile, sl):
        def per_expert(e, carry):
            b, total = carry
            n = n_ref[tile * N_EXPERTS + e]
            s = s_ref[tile * N_EXPERTS + e]
            nch = (n + COMB_CHUNK - 1) // COMB_CHUNK

            def per_chunk(j, c):
                copy(s + j * COMB_CHUNK, b + j * COMB_CHUNK, sl).start()
                return c
            lax.fori_loop(0, nch, per_chunk, 0)
            return b + nch * COMB_CHUNK, total + nch
        _, total = lax.fori_loop(0, N_EXPERTS, per_expert, (0, 0))
        cnt_ref[sl] = total

    @pl.when(i == 0)
    def _():
        stg_ref[...] = jnp.zeros_like(stg_ref)
        issue(tile0, 0)

    rc = rcol_ref[...]
    col_k = [rc[:, 2 * TOP_K + k:2 * TOP_K + k + 1].astype(I32) for k in range(TOP_K)]
    w_k = [rc[:, 3 * TOP_K + k:3 * TOP_K + k + 1] for k in range(TOP_K)]

    def chunk_dot(sl, c):
        jl = lax.broadcasted_iota(I32, (TM, 256), 1) + c * 256
        pm = jnp.zeros((TM, 256), F32)
        for k in range(TOP_K):
            pm = pm + jnp.where(jl == col_k[k], w_k[k], 0.0)
        rows = _from_row_tiles(stg_ref.at[sl], c * 256, 256)
        return jnp.dot(pm.astype(BF16), rows, preferred_element_type=F32)

    for sl in range(2):
        @pl.when(slot == sl)
        def _():
            @pl.when(i + 1 < nt)
            def _():
                issue(tile0 + i + 1, 1 - sl)

            total = cnt_ref[sl]

            def wait_body(_, c):
                copy(0, 0, sl).wait()
                return c
            lax.fori_loop(0, total, wait_body, 0)

            y = chunk_dot(sl, 0)
            for c in range(1, COMB_KC_MIN):
                y = y + chunk_dot(sl, c)
            acc_ref[...] = y
            used = (total * COMB_CHUNK + 255) // 256
            for c in range(COMB_KC_MIN, COMB_KC):
                @pl.when(c < used)
                def _():
                    acc_ref[...] += chunk_dot(sl, c)

    mod = mod_ref[0]
    g2 = mod[:, 5 * D_MODEL:6 * D_MODEL]
    y_ref[...] = _layer_norm(DN_ALPHA * x1_ref[...] + g2 * acc_ref[...], g_ref[...], b_ref[...])


def _combine(seqs, tile0, n_tiles, n_te, s_te, x1, rcol, mod3, ln2_g, ln2_b, eo):
    def mod_map(i, *_):
        return (_tile_pos(seqs, (i + tile0) * TM)[2], 0, 0)

    return pl.pallas_call(
        functools.partial(_combine_kernel, tile0),
        grid_spec=pltpu.PrefetchScalarGridSpec(
            num_scalar_prefetch=2, grid=(n_tiles,),
            in_specs=[pl.BlockSpec((TM, D_MODEL), lambda i, *_: (i + tile0, 0)),
                      pl.BlockSpec((TM, LANES), lambda i, *_: (i + tile0, 0)),
                      pl.BlockSpec((1, 1, 6 * D_MODEL), mod_map),
                      pl.BlockSpec((1, D_MODEL), lambda i, *_: (0, 0)),
                      pl.BlockSpec((1, D_MODEL), lambda i, *_: (0, 0)),
                      pl.BlockSpec(memory_space=pl.ANY)],
            out_specs=pl.BlockSpec((TM, D_MODEL), lambda i, *_: (i, 0)),
            scratch_shapes=[pltpu.VMEM((2, COMB_ROWS // PAIR * ROW_TILES, LANES), U32),
                            pltpu.VMEM((TM, D_MODEL), F32),
                            pltpu.SMEM((2,), I32),
                            pltpu.SemaphoreType.DMA((2,))]),
        out_shape=jax.ShapeDtypeStruct((n_tiles * TM, D_MODEL), F32),
        compiler_params=pltpu.CompilerParams(dimension_semantics=("arbitrary",),
                                             vmem_limit_bytes=VMEM_LIMIT_BYTES),
        name="combine",
    )(n_te, s_te, x1, rcol, mod3, ln2_g, ln2_b, eo)


def _prep_params(w_in, b_in, w_spatial, b_spatial, sgu_ln_g, sgu_ln_b, w_br_attn, w_br_sgu, w_out,
                 ln1_g, ln1_b, w_router, b_router):
    q_end, k_end, v_end = ATT_W, ATT_W + KV_W, ATT_W + 2 * KV_W

    def dup(w, lo):
        h0, h1 = w[..., lo:lo + HEAD_DIM], w[..., lo + HEAD_DIM:lo + 2 * HEAD_DIM]
        return jnp.concatenate([h0, h0, h1, h1], axis=-1)

    w_kv = jnp.concatenate([dup(w_in, q_end), dup(w_in, k_end)], axis=1).astype(BF16)
    b_kv = jnp.concatenate([dup(b_in, q_end), dup(b_in, k_end)], axis=0).reshape(1, -1)
    w_mix = jnp.concatenate([w_in[:, :q_end], w_in[:, v_end:]], axis=1).astype(BF16)
    b_mix = jnp.concatenate([b_in[:q_end], b_in[v_end:]], axis=0).reshape(1, -1)
    w_hi = w_router.astype(BF16)
    w_lo = (w_router - w_hi.astype(F32)).astype(BF16)
    p = dict(
        w_mix=w_mix, b_mix=b_mix,
        w_sp=w_spatial.astype(BF16), b_spt=b_spatial.T,
        sgu_g=sgu_ln_g.reshape(1, -1), sgu_b=sgu_ln_b.reshape(1, -1),
        w_ba=w_br_attn.astype(BF16), w_bs=w_br_sgu.astype(BF16), w_out=w_out.astype(BF16),
        ln1_g=ln1_g.reshape(1, -1), ln1_b=ln1_b.reshape(1, -1),
        w_r=jnp.concatenate([w_hi, w_lo], axis=1), b_r=b_router.reshape(1, -1),
    )
    return w_kv, b_kv, p


def _layer(seqs, xp, xs, c_all, w_ada, b_ada, w_in, b_in, sink, sgu_ln_g, sgu_ln_b, w_spatial, b_spatial,
           w_br_attn, w_br_sgu, w_out, ln1_g, ln1_b, w_router, b_router, w_up, b_up, w_down, b_down,
           ln2_g, ln2_b):
    T = seqs.n_tokens
    nt = T // TM
    ntp = seqs.n_prompt // TM
    c_pad = jnp.zeros((8, D_MODEL), F32).at[:c_all.shape[0]].set(c_all)
    mod3 = _ada(c_pad, w_ada, b_ada).reshape(8, 1, 6 * D_MODEL)
    cos_t, sin_t = _rope_tables(max(seqs.prompt_len, seqs.sample_len))
    w_kv, b_kv, p = _prep_params(w_in, b_in, w_spatial, b_spatial, sgu_ln_g, sgu_ln_b, w_br_attn,
                                 w_br_sgu, w_out, ln1_g, ln1_b, w_router, b_router)
    kt, v = _kv(seqs, xp, xs, mod3, cos_t, sin_t, w_kv, b_kv)
    x1, h2, logits = _mixer(seqs, sink, xp, xs, mod3, cos_t, sin_t, kt, v, p)
    rcol, rrow, cnt3 = _router(logits)

    cnt = cnt3[:, 0, :N_EXPERTS]
    count = cnt.sum(0)
    reg = (count + REGION_SLACK + BM - 1) // BM * BM
    pad_end = jnp.cumsum(reg)
    pad_start = pad_end - reg
    s_te = (pad_start[None, :] + jnp.cumsum(cnt, axis=0) - cnt).reshape(-1).astype(I32)
    n_te = cnt.reshape(-1).astype(I32)
    n_blk = (nt * TILE_ROWS + N_EXPERTS * (REGION_SLACK + BM - 1)) // BM + 1
    tails = jnp.concatenate([pad_start + count, pad_end[:-1], jnp.array([n_blk * BM])]).astype(I32)
    blk_start = jnp.arange(n_blk, dtype=I32) * BM
    blk_e = jnp.minimum((blk_start[:, None] >= pad_end[None, :]).sum(1), N_EXPERTS - 1).astype(I32)
    owner = blk_e[:, None] == jnp.arange(N_EXPERTS)[None, :]
    blk_rows = (owner * (pad_start + count)[None, :]).sum(1) - blk_start
    blk_v = jnp.where(blk_start < pad_end[-1], jnp.clip(blk_rows, 0, BM), 0).astype(I32)

    xin = _dispatch(seqs, n_te, s_te, tails, h2, rrow, n_blk * BM)
    eo = _experts(blk_e, blk_v, xin, w_up, b_up, w_down, b_down, n_blk)
    g2, b2 = ln2_g.reshape(1, -1), ln2_b.reshape(1, -1)
    return (_combine(seqs, 0, ntp, n_te, s_te, x1, rcol, mod3, g2, b2, eo),
            _combine(seqs, ntp, nt - ntp, n_te, s_te, x1, rcol, mod3, g2, b2, eo))


def kernel(x_prompt, x_sample, c_prompt, c_sample, w_ada, b_ada, w_in, b_in, sink, sgu_ln_g, sgu_ln_b, w_spatial, b_spatial, w_br_attn, w_br_sgu, w_out, ln1_g, ln1_b, w_router, b_router, w_up, b_up, w_down, b_down, ln2_g, ln2_b):
    assert w_ada.shape[0] == DEPTH == 1
    bp, sp, d = x_prompt.shape
    bs, ss, _ = x_sample.shape
    seqs = _Seqs(n_prompt=bp * sp, prompt_len=sp, sample_len=ss, n_tokens=bp * sp + bs * ss)
    c_all = jnp.concatenate([c_prompt, c_sample], axis=0)
    yp, ys = _layer(seqs, x_prompt.reshape(bp * sp, d), x_sample.reshape(bs * ss, d), c_all,
                    w_ada[0], b_ada[0], w_in[0], b_in[0], sink[0], sgu_ln_g[0], sgu_ln_b[0],
                    w_spatial[0], b_spatial[0], w_br_attn[0], w_br_sgu[0], w_out[0], ln1_g[0], ln1_b[0],
                    w_router[0], b_router[0], w_up[0], b_up[0], w_down[0], b_down[0], ln2_g[0], ln2_b[0])
    return (yp.reshape(bp, sp, d), ys.reshape(bs, ss, d))
```

```python
import functools
import math
from typing import NamedTuple

import jax
import jax.numpy as jnp
from jax import lax
from jax.experimental import pallas as pl
from jax.experimental.pallas import tpu as pltpu

F32 = jnp.float32
BF16 = jnp.bfloat16
I32 = jnp.int32

D_MODEL = 1024
N_HEADS = 8
N_KV_HEADS = 2
HEAD_DIM = 64
ATT_W = N_HEADS * HEAD_DIM
KV_W = N_KV_HEADS * HEAD_DIM
BLOCK = 128
ROPE_THETA = 500000.0
ROT_DIM = HEAD_DIM // 4
ROT_HALF = ROT_DIM // 2
SGU_W = D_MODEL // 2
SGU_GROUPS = 4
N_EXPERTS = 32
TOP_K = 4
D_FF = D_MODEL
SWIGLU_LIMIT = 7.0
SWIGLU_ALPHA = 1.702
LN_EPS = 1e-5
DEPTH = 1
DN_ALPHA = (2 * DEPTH) ** 0.25

LANES = 128
SUBLANES = 8
ROW_TILES = D_MODEL // LANES
VMEM_LIMIT_BYTES = 56 * 1024 * 1024
U32 = jnp.uint32
PAIR = 2

TM = 256
TK = 512
BM = 512
ROUTER_TOKENS = 2048
DISP_CHUNK = 64
COMB_CHUNK = 48
REGION_SLACK = DISP_CHUNK
TILE_ROWS = TM * TOP_K + N_EXPERTS
DISP_ROWS = TILE_ROWS + DISP_CHUNK
COMB_ROWS = (TILE_ROWS + N_EXPERTS * COMB_CHUNK + 255) // 256 * 256
NEG_INF = float("-inf")


class _Seqs(NamedTuple):
    n_prompt: int
    prompt_len: int
    sample_len: int
    n_tokens: int


def _tile_pos(seqs, t0):
    is_s = t0 >= seqs.n_prompt
    seq_len = jnp.where(is_s, seqs.sample_len, seqs.prompt_len)
    off = jnp.where(is_s, t0 - seqs.n_prompt, t0)
    pos0 = off % seq_len
    row = jnp.where(is_s, seqs.n_prompt // seqs.prompt_len + off // seq_len, off // seq_len)
    return seq_len, pos0, row


def _ada_kernel(c_ref, w_ref, b_ref, o_ref):
    c = c_ref[...]
    a = c * jax.nn.sigmoid(c)
    o_ref[...] = jnp.dot(a, w_ref[...], preferred_element_type=F32,
                         precision=lax.Precision.HIGHEST) + b_ref[...]


def _ada(c_pad, w_ada, b_ada):
    n = w_ada.shape[1]
    bn = 1536
    return pl.pallas_call(
        _ada_kernel,
        grid=(n // bn,),
        in_specs=[pl.BlockSpec((8, D_MODEL), lambda j: (0, 0)),
                  pl.BlockSpec((D_MODEL, bn), lambda j: (0, j)),
                  pl.BlockSpec((1, bn), lambda j: (0, j))],
        out_specs=pl.BlockSpec((8, bn), lambda j: (0, j)),
        out_shape=jax.ShapeDtypeStruct((8, n), F32),
        compiler_params=pltpu.CompilerParams(vmem_limit_bytes=VMEM_LIMIT_BYTES),
        name="ada",
    )(c_pad, w_ada, b_ada.reshape(1, n))


def _rope_tables(length):
    lane = jnp.arange(LANES) % HEAD_DIM
    inv = ROPE_THETA ** (-(lane % ROT_HALF).astype(F32) * 2.0 / ROT_DIM)
    inv = jnp.where(lane < ROT_DIM, inv, 0.0)
    sign = jnp.where(lane < ROT_HALF, -1.0, 1.0)
    ang = jnp.arange(length, dtype=F32)[:, None] * inv[None, :]
    return jnp.cos(ang), jnp.sin(ang) * sign[None, :]


def _rope(x, cos, sin):
    n = x.shape[1]
    reps = n // LANES
    c = jnp.concatenate([cos] * reps, axis=1)
    s = jnp.concatenate([sin] * reps, axis=1)
    lane = lax.broadcasted_iota(I32, x.shape, 1)
    first = (lane & (HEAD_DIM - 1)) < ROT_HALF
    partner = jnp.where(first, pltpu.roll(x, n - ROT_HALF, 1), pltpu.roll(x, ROT_HALF, 1))
    return x * c + partner * s


def _kv_kernel(seqs, xp_ref, xs_ref, mod_ref, cos_ref, sin_ref, w_ref, b_ref, kt_ref, v_ref):
    mod = mod_ref[0]
    sh1 = mod[:, 0:D_MODEL]
    sc1 = mod[:, D_MODEL:2 * D_MODEL]
    x = jnp.where(pl.program_id(0) * TK >= seqs.n_prompt, xs_ref[...], xp_ref[...])
    h = (x * (1.0 + sc1) + sh1).astype(BF16)
    kv = jnp.dot(h, w_ref[...], preferred_element_type=F32) + b_ref[...]
    k = _rope(kv[:, 0:2 * LANES], cos_ref[...], sin_ref[...])
    kt_ref[...] = k.T.astype(BF16)
    v_ref[...] = kv[:, 2 * LANES:4 * LANES].astype(BF16)


def _group_x_specs(seqs, tile):
    ntp = seqs.n_prompt // tile
    return [pl.BlockSpec((tile, D_MODEL), lambda i, *_: (jnp.minimum(i, ntp - 1), 0)),
            pl.BlockSpec((tile, D_MODEL), lambda i, *_: (jnp.maximum(i - ntp, 0), 0))]


def _kv(seqs, xp, xs, mod3, cos_t, sin_t, w_kv, b_kv):
    T = seqs.n_tokens

    def mod_map(i):
        return (_tile_pos(seqs, i * TK)[2], 0, 0)

    def rope_map(i):
        return (_tile_pos(seqs, i * TK)[1] // TK, 0)

    return pl.pallas_call(
        functools.partial(_kv_kernel, seqs),
        grid=(T // TK,),
        in_specs=_group_x_specs(seqs, TK) + [
                  pl.BlockSpec((1, 1, 6 * D_MODEL), mod_map),
                  pl.BlockSpec((TK, LANES), rope_map),
                  pl.BlockSpec((TK, LANES), rope_map),
                  pl.BlockSpec((D_MODEL, 4 * LANES), lambda i: (0, 0)),
                  pl.BlockSpec((1, 4 * LANES), lambda i: (0, 0))],
        out_specs=[pl.BlockSpec((2 * LANES, TK), lambda i: (0, i)),
                   pl.BlockSpec((TK, 2 * LANES), lambda i: (i, 0))],
        out_shape=[jax.ShapeDtypeStruct((2 * LANES, T), BF16),
                   jax.ShapeDtypeStruct((T, 2 * LANES), BF16)],
        compiler_params=pltpu.CompilerParams(dimension_semantics=("arbitrary",),
                                             vmem_limit_bytes=VMEM_LIMIT_BYTES),
        name="kv",
    )(xp, xs, mod3, cos_t, sin_t, w_kv, b_kv)


def _layer_norm(x, g, b):
    mu = jnp.mean(x, axis=-1, keepdims=True)
    xc = x - mu
    var = jnp.mean(xc * xc, axis=-1, keepdims=True)
    return xc * lax.rsqrt(var + LN_EPS) * g + b


def _attention(q, kfull, vfull, valids, sink_ref):
    lane = lax.broadcasted_iota(I32, (BLOCK, LANES), 1)
    lo = lane < HEAD_DIM
    ones = jnp.ones((3 * BLOCK, LANES), BF16)
    units = [(jb, hk) for jb in range(TM // BLOCK) for hk in range(N_KV_HEADS)]
    scores, sinks = [], []
    for jb, hk in units:
        parts = []
        for p in range(2):
            qp = q[jb * BLOCK:(jb + 1) * BLOCK, (2 * hk + p) * LANES:(2 * hk + p + 1) * LANES]
            parts.append(jnp.where(lo, qp, 0.0).astype(BF16))
            parts.append(jnp.where(lo, 0.0, qp).astype(BF16))
        kwin = kfull[hk * LANES:(hk + 1) * LANES, jb * BLOCK:(jb + 3) * BLOCK]
        s = jnp.dot(jnp.concatenate(parts, axis=0), kwin, preferred_element_type=F32)
        scores.append(jnp.where(jnp.concatenate([valids[jb]] * 4, axis=0), s, NEG_INF))
        sinks.extend(jnp.full((BLOCK, 1), sink_ref[hk * 4 + g], F32) for g in range(4))
    s = jnp.concatenate(scores, axis=0)
    sk = jnp.concatenate(sinks, axis=0)
    m = jnp.maximum(jnp.max(s, axis=-1, keepdims=True), sk)
    p = jnp.exp(s - m).astype(BF16)
    sink_term = jnp.exp(sk - m)
    rows = []
    for ui, (jb, hk) in enumerate(units):
        vwin = jnp.concatenate([vfull[jb * BLOCK:(jb + 3) * BLOCK, hk * LANES:(hk + 1) * LANES], ones], axis=1)
        r0 = ui * 4 * BLOCK
        ov = jnp.dot(p[r0:r0 + 4 * BLOCK], vwin, preferred_element_type=F32)
        o = ov[:, 0:LANES] / (ov[:, LANES:2 * LANES] + sink_term[r0:r0 + 4 * BLOCK])
        pair = [jnp.where(lo, o[(2 * p2) * BLOCK:(2 * p2 + 1) * BLOCK],
                          o[(2 * p2 + 1) * BLOCK:(2 * p2 + 2) * BLOCK]) for p2 in range(2)]
        rows.append(jnp.concatenate(pair, axis=1))
    n_h = N_KV_HEADS
    return jnp.concatenate(
        [jnp.concatenate(rows[jb * n_h:(jb + 1) * n_h], axis=1) for jb in range(TM // BLOCK)], axis=0)


def _mixer_kernel(seqs, sink_ref, xp_ref, xs_ref, mod_ref, cos_ref, sin_ref,
                  ktp_ref, ktc_ref, ktn_ref, vp_ref, vc_ref, vn_ref,
                  wmix_ref, bmix_ref, wsp_ref, bspt_ref, sg_ref, sb_ref,
                  wba_ref, wbs_ref, wout_ref, l1g_ref, l1b_ref, wr_ref, br_ref,
                  x1_ref, h2_ref, logit_ref):
    i = pl.program_id(0)
    seq_len, pos0, _ = _tile_pos(seqs, i * TM)
    mod = mod_ref[0]
    sh1, sc1, g1 = (mod[:, j * D_MODEL:(j + 1) * D_MODEL] for j in range(3))
    sh2, sc2 = (mod[:, j * D_MODEL:(j + 1) * D_MODEL] for j in range(3, 5))
    x = jnp.where(i * TM >= seqs.n_prompt, xs_ref[...], xp_ref[...])
    h = (x * (1.0 + sc1) + sh1).astype(BF16)
    z = jnp.dot(h, wmix_ref[...], preferred_element_type=F32) + bmix_ref[...]
    q = _rope(z[:, 0:ATT_W], cos_ref[...], sin_ref[...]) * (HEAD_DIM ** -0.5)
    u = jax.nn.gelu(z[:, ATT_W:ATT_W + SGU_W])
    vs = _layer_norm(jax.nn.gelu(z[:, ATT_W + SGU_W:ATT_W + 2 * SGU_W]), sg_ref[...], sb_ref[...])
    ga = z[:, ATT_W + 2 * SGU_W:ATT_W + 2 * SGU_W + D_MODEL]
    gs = z[:, ATT_W + 2 * SGU_W + D_MODEL:]

    kfull = jnp.concatenate([ktp_ref[...], ktc_ref[...], ktn_ref[...]], axis=1)
    vfull = jnp.concatenate([vp_ref[...], vc_ref[...], vn_ref[...]], axis=0)
    qi = lax.broadcasted_iota(I32, (BLOCK, 3 * BLOCK), 0)
    ki = lax.broadcasted_iota(I32, (BLOCK, 3 * BLOCK), 1)
    band = (ki >= qi) & (ki <= qi + 2 * BLOCK)
    vs_b = vs.astype(BF16)
    valids, sgu_rows = [], []
    for jb in range(TM // BLOCK):
        posb = pos0 + jb * BLOCK
        valids.append(band & (ki >= jnp.where(posb == 0, BLOCK, 0))
                      & (ki < jnp.where(posb + BLOCK == seq_len, 2 * BLOCK, 3 * BLOCK)))
        groups = []
        for g in range(SGU_GROUPS):
            vg = vs_b[jb * BLOCK:(jb + 1) * BLOCK, g * LANES:(g + 1) * LANES]
            sv = jnp.dot(wsp_ref[g], vg, preferred_element_type=F32) + bspt_ref[:, g:g + 1]
            groups.append(sv)
        sgu_rows.append(jnp.concatenate(groups, axis=1))
    attn = _attention(q, kfull, vfull, valids, sink_ref)
    sgu = u * jnp.concatenate(sgu_rows, axis=0)

    a1 = jnp.dot(attn.astype(BF16), wba_ref[...], preferred_element_type=F32)
    a2 = jnp.dot(sgu.astype(BF16), wbs_ref[...], preferred_element_type=F32)
    merged = jax.nn.sigmoid(ga) * a1 + jax.nn.sigmoid(gs) * a2
    mix = jnp.dot(merged.astype(BF16), wout_ref[...], preferred_element_type=F32)
    x1 = _layer_norm(DN_ALPHA * x + g1 * mix, l1g_ref[...], l1b_ref[...])
    x1_ref[...] = x1
    h2 = x1 * (1.0 + sc2) + sh2
    hi = h2.astype(BF16)
    h2_ref[...] = hi

    lo_part = (h2 - hi.astype(F32)).astype(BF16)
    l1 = jnp.dot(hi, wr_ref[...], preferred_element_type=F32)
    l2 = jnp.dot(lo_part, wr_ref[:, 0:N_EXPERTS], preferred_element_type=F32)
    logit_ref[...] = l1[:, 0:N_EXPERTS] + l1[:, N_EXPERTS:2 * N_EXPERTS] + l2 + br_ref[...]


PACK = 4096.0
assert DISP_ROWS <= PACK and COMB_ROWS <= PACK and DISP_ROWS * PACK < 2 ** 24


def _router_kernel(l_ref, rcol_ref, rrow_ref, cnt_ref):
    rt = l_ref.shape[0]
    ns = rt // TM
    eidx = lax.broadcasted_iota(I32, (rt, N_EXPERTS), 1)
    work = l_ref[...]
    idxs, vals = [], []
    for _ in range(TOP_K):
        m = jnp.max(work, axis=-1, keepdims=True)
        ix = jnp.min(jnp.where(work == m, eidx, N_EXPERTS), axis=-1, keepdims=True)
        idxs.append(ix)
        vals.append(m)
        work = jnp.where(eidx == ix, NEG_INF, work)
    exps = [jnp.exp(v - vals[0]) for v in vals]
    esum = exps[0] + exps[1] + exps[2] + exps[3]
    wts = [e / esum for e in exps]

    sel = jnp.zeros((rt, N_EXPERTS), F32)
    for ix in idxs:
        sel = sel + jnp.where(eidx == ix, 1.0, 0.0)
    sel_b = sel.astype(BF16)
    ti = lax.broadcasted_iota(I32, (TM, TM), 0)
    tj = lax.broadcasted_iota(I32, (TM, TM), 1)
    lower = jnp.where(tj < ti, 1.0, 0.0).astype(BF16)
    tiles = [slice(s * TM, (s + 1) * TM) for s in range(ns)]
    rank = jnp.concatenate([jnp.dot(lower, sel_b[t], preferred_element_type=F32) for t in tiles], axis=0)
    cnt = jnp.concatenate([jnp.sum(sel[t], axis=0, keepdims=True) for t in tiles], axis=0)
    cnt = cnt + (cnt - 2.0 * jnp.floor(cnt * 0.5))
    nch = jnp.maximum(jnp.floor((cnt + (COMB_CHUNK - 0.5)) * (1.0 / COMB_CHUNK)), 1.0)
    ei = lax.broadcasted_iota(I32, (N_EXPERTS, N_EXPERTS), 0)
    ej = lax.broadcasted_iota(I32, (N_EXPERTS, N_EXPERTS), 1)
    before = jnp.where(ei < ej, 1.0, 0.0).astype(BF16)
    fill = (-2 * ns) % 16
    pre = jnp.concatenate([cnt, nch] + ([jnp.zeros((fill, N_EXPERTS), F32)] if fill else []),
                          axis=0).astype(BF16)
    base = jnp.dot(pre, before, preferred_element_type=F32)
    dbase = jnp.concatenate([jnp.broadcast_to(base[s:s + 1], (TM, N_EXPERTS)) for s in range(ns)], axis=0)
    cbase = jnp.concatenate([jnp.broadcast_to(base[ns + s:ns + s + 1] * COMB_CHUNK, (TM, N_EXPERTS))
                             for s in range(ns)], axis=0)
    both = (rank + dbase) * PACK + (rank + cbase)

    cols = [ix.astype(F32) for ix in idxs]
    packed = [jnp.sum(jnp.where(eidx == ix, both, 0.0), axis=-1, keepdims=True) for ix in idxs]
    drow = [jnp.floor(v * (1.0 / PACK)) for v in packed]
    cols.extend(drow)
    cols.extend(v - d * PACK for v, d in zip(packed, drow))
    cols.extend(wts)
    lane = lax.broadcasted_iota(I32, (rt, LANES), 1)
    rc = jnp.zeros((rt, LANES), F32)
    for j, c in enumerate(cols):
        rc = jnp.where(lane == j, c, rc)
    rcol_ref[...] = rc
    for s in range(ns):
        rrow_ref[s] = rc[tiles[s]].T[0:16]
        cnt_ref[s] = jnp.broadcast_to(
            jnp.concatenate([cnt[s:s + 1], jnp.zeros((1, LANES - N_EXPERTS), F32)], axis=1),
            (8, LANES)).astype(I32)


def _router(logits):
    T = logits.shape[0]
    rt = math.gcd(T, ROUTER_TOKENS)
    nt = T // TM
    return pl.pallas_call(
        _router_kernel,
        grid=(T // rt,),
        in_specs=[pl.BlockSpec((rt, N_EXPERTS), lambda i: (i, 0))],
        out_specs=[pl.BlockSpec((rt, LANES), lambda i: (i, 0)),
                   pl.BlockSpec((rt // TM, 16, TM), lambda i: (i, 0, 0)),
                   pl.BlockSpec((rt // TM, 8, LANES), lambda i: (i, 0, 0))],
        out_shape=[jax.ShapeDtypeStruct((T, LANES), F32),
                   jax.ShapeDtypeStruct((nt, 16, TM), F32),
                   jax.ShapeDtypeStruct((nt, 8, LANES), I32)],
        compiler_params=pltpu.CompilerParams(dimension_semantics=("arbitrary",),
                                             vmem_limit_bytes=VMEM_LIMIT_BYTES),
        name="router",
    )(logits)


def _mixer(seqs, sink, xp, xs, mod3, cos_t, sin_t, kt, v, p):
    T = seqs.n_tokens
    nt = T // TM
    nb = T // BLOCK
    r = TM // BLOCK

    def mod_map(i, s):
        return (_tile_pos(seqs, i * TM)[2], 0, 0)

    def rope_map(i, s):
        return (_tile_pos(seqs, i * TM)[1] // TM, 0)

    const2 = lambda i, s: (0, 0)
    in_specs = _group_x_specs(seqs, TM) + [
        pl.BlockSpec((1, 1, 6 * D_MODEL), mod_map),
        pl.BlockSpec((TM, LANES), rope_map),
        pl.BlockSpec((TM, LANES), rope_map),
        pl.BlockSpec((2 * LANES, BLOCK), lambda i, s: (0, jnp.maximum(i * r - 1, 0))),
        pl.BlockSpec((2 * LANES, TM), lambda i, s: (0, i)),
        pl.BlockSpec((2 * LANES, BLOCK), lambda i, s: (0, jnp.minimum(i * r + r, nb - 1))),
        pl.BlockSpec((BLOCK, 2 * LANES), lambda i, s: (jnp.maximum(i * r - 1, 0), 0)),
        pl.BlockSpec((TM, 2 * LANES), lambda i, s: (i, 0)),
        pl.BlockSpec((BLOCK, 2 * LANES), lambda i, s: (jnp.minimum(i * r + r, nb - 1), 0)),
        pl.BlockSpec(p["w_mix"].shape, const2),
        pl.BlockSpec(p["b_mix"].shape, const2),
        pl.BlockSpec(p["w_sp"].shape, lambda i, s: (0, 0, 0)),
        pl.BlockSpec(p["b_spt"].shape, const2),
        pl.BlockSpec(p["sgu_g"].shape, const2),
        pl.BlockSpec(p["sgu_b"].shape, const2),
        pl.BlockSpec(p["w_ba"].shape, const2),
        pl.BlockSpec(p["w_bs"].shape, const2),
        pl.BlockSpec(p["w_out"].shape, const2),
        pl.BlockSpec(p["ln1_g"].shape, const2),
        pl.BlockSpec(p["ln1_b"].shape, const2),
        pl.BlockSpec(p["w_r"].shape, const2),
        pl.BlockSpec(p["b_r"].shape, const2),
    ]
    out_specs = [
        pl.BlockSpec((TM, D_MODEL), lambda i, s: (i, 0)),
        pl.BlockSpec((TM, D_MODEL), lambda i, s: (i, 0)),
        pl.BlockSpec((TM, N_EXPERTS), lambda i, s: (i, 0)),
    ]
    out_shape = [
        jax.ShapeDtypeStruct((T, D_MODEL), F32),
        jax.ShapeDtypeStruct((T, D_MODEL), BF16),
        jax.ShapeDtypeStruct((T, N_EXPERTS), F32),
    ]
    return pl.pallas_call(
        functools.partial(_mixer_kernel, seqs),
        grid_spec=pltpu.PrefetchScalarGridSpec(
            num_scalar_prefetch=1, grid=(nt,), in_specs=in_specs, out_specs=out_specs),
        out_shape=out_shape,
        compiler_params=pltpu.CompilerParams(dimension_semantics=("arbitrary",),
                                             vmem_limit_bytes=VMEM_LIMIT_BYTES),
        name="mixer",
    )(sink, xp, xs, mod3, cos_t, sin_t, kt, kt, kt, v, v, v,
      p["w_mix"], p["b_mix"], p["w_sp"], p["b_spt"], p["sgu_g"], p["sgu_b"],
      p["w_ba"], p["w_bs"], p["w_out"], p["ln1_g"], p["ln1_b"], p["w_r"], p["b_r"])


def _pair_rows(row):
    return pl.multiple_of((row // PAIR) * ROW_TILES, ROW_TILES)


def _to_row_tiles(dst_ref, rows, n):
    words = pltpu.bitcast(rows, U32)
    for c in range(ROW_TILES):
        dst_ref[pl.ds(c, n // PAIR, stride=ROW_TILES), :] = words[:, c * LANES:(c + 1) * LANES]


def _from_row_tiles(src_ref, start, n):
    base = start // PAIR * ROW_TILES
    words = jnp.concatenate(
        [src_ref[pl.ds(base + c, n // PAIR, stride=ROW_TILES), :] for c in range(ROW_TILES)], axis=1)
    return pltpu.bitcast(words, BF16)


WAIT_GROUP = 16


def _issue_tile(copy, k0, src_ref, dst_ref, extra_ref, n_extra, step):
    for e in range(N_EXPERTS):
        copy(src_ref[k0 + e], dst_ref[k0 + e]).start()

    @pl.when(n_extra > 0)
    def _():
        def per_expert(e, c):
            def per_chunk(j, c2):
                copy(src_ref[k0 + e] + j * step, dst_ref[k0 + e] + j * step).start()
                return c2
            lax.fori_loop(1, extra_ref[k0 + e] + 1, per_chunk, 0)
            return c
        lax.fori_loop(0, N_EXPERTS, per_expert, 0)


def _wait_tile(group_copy, chunk_copy, n_extra):
    for _ in range(N_EXPERTS // WAIT_GROUP):
        group_copy.wait()

    def body(_, c):
        chunk_copy.wait()
        return c
    lax.fori_loop(0, n_extra, body, 0)


def _dispatch_kernel(b_ref, s_ref, x_ref, nx_ref, tail_ref, h2_ref, rrow_ref, xin_ref, stg_ref, zero_ref, sem):
    i = pl.program_id(0)
    nt = pl.num_programs(0)
    slot = i % 2
    rr = rrow_ref[0]
    rho = lax.broadcasted_iota(I32, (DISP_ROWS, TM), 0)
    pt = jnp.zeros((DISP_ROWS, TM), F32)
    for k in range(TOP_K):
        pt = pt + jnp.where(rho == rr[TOP_K + k:TOP_K + k + 1].astype(I32), 1.0, 0.0)
    rows = jnp.dot(pt.astype(BF16), h2_ref[...], preferred_element_type=F32).astype(BF16)

    chunk = DISP_CHUNK // PAIR * ROW_TILES

    def copy(src_row, dst_row, sl):
        return pltpu.make_async_copy(stg_ref.at[sl, pl.ds(_pair_rows(src_row), chunk)],
                                     xin_ref.at[pl.ds(_pair_rows(dst_row), chunk)], sem)

    def wait_tile(tile):
        group = pltpu.make_async_copy(stg_ref.at[0, pl.ds(0, WAIT_GROUP * chunk)],
                                      xin_ref.at[pl.ds(0, WAIT_GROUP * chunk)], sem)
        _wait_tile(group, copy(0, 0, 0), nx_ref[tile])

    for sl in range(2):
        @pl.when(slot == sl)
        def _():
            _to_row_tiles(stg_ref.at[sl], rows, DISP_ROWS)

            @pl.when(i > 0)
            def _():
                wait_tile(i - 1)
            _issue_tile(lambda src, dst: copy(src, dst, sl), i * N_EXPERTS, b_ref, s_ref, x_ref, nx_ref[i],
                        DISP_CHUNK)

    @pl.when(i == nt - 1)
    def _():
        wait_tile(i)
        zero_ref[...] = jnp.zeros_like(zero_ref)

        def zcopy(dst_row):
            return pltpu.make_async_copy(zero_ref, xin_ref.at[pl.ds(_pair_rows(dst_row), chunk)], sem)

        def zwait(count):
            def body(_, c):
                zcopy(0).wait()
                return c
            lax.fori_loop(0, count, body, 0)

        def per_expert(e, total):
            lo = tail_ref[e]
            nz = (tail_ref[N_EXPERTS + e] - lo) // DISP_CHUNK

            def per_chunk(j, c):
                zcopy(lo + j * DISP_CHUNK).start()
                return c
            lax.fori_loop(0, nz, per_chunk, 0)
            return total + nz
        zwait(lax.fori_loop(0, N_EXPERTS, per_expert, 0))

        def last_chunk(e, c):
            zcopy(tail_ref[N_EXPERTS + e] - DISP_CHUNK).start()
            return c
        lax.fori_loop(0, N_EXPERTS, last_chunk, 0)
        zwait(N_EXPERTS)


def _dispatch(seqs, b_te, s_te, x_te, nx_t, tails, h2, rrow, n_rows):
    nt = seqs.n_tokens // TM
    return pl.pallas_call(
        _dispatch_kernel,
        grid_spec=pltpu.PrefetchScalarGridSpec(
            num_scalar_prefetch=5, grid=(nt,),
            in_specs=[pl.BlockSpec((TM, D_MODEL), lambda i, *_: (i, 0)),
                      pl.BlockSpec((1, 16, TM), lambda i, *_: (i, 0, 0))],
            out_specs=pl.BlockSpec(memory_space=pl.ANY),
            scratch_shapes=[pltpu.VMEM((2, DISP_ROWS // PAIR * ROW_TILES, LANES), U32),
                            pltpu.VMEM((DISP_CHUNK // PAIR * ROW_TILES, LANES), U32),
                            pltpu.SemaphoreType.DMA]),
        out_shape=jax.ShapeDtypeStruct((n_rows // PAIR * ROW_TILES, LANES), U32),
        compiler_params=pltpu.CompilerParams(dimension_semantics=("arbitrary",),
                                             vmem_limit_bytes=VMEM_LIMIT_BYTES),
        name="dispatch",
    )(b_te, s_te, x_te, nx_t, tails, h2, rrow)


def _expert_kernel(be_ref, bv_ref, x_ref, wup_hbm, bup_ref, wdn_hbm, bdn_ref, o_ref,
                   wup_f, wdn_f, wup_b, wdn_b, sem):
    i = pl.program_id(0)
    e = be_ref[i]
    valid = bv_ref[i]
    first = jnp.logical_or(i == 0, e != be_ref[jnp.maximum(i - 1, 0)])

    def fetch(ex):
        slot = ex % 2
        return (pltpu.make_async_copy(wup_hbm.at[ex], wup_f.at[slot], sem.at[0, slot]),
                pltpu.make_async_copy(wdn_hbm.at[ex], wdn_f.at[slot], sem.at[1, slot]))

    @pl.when(i == 0)
    def _():
        for d in fetch(e):
            d.start()

    @pl.when(first)
    def _():
        for d in fetch(e):
            d.wait()

        @pl.when(e + 1 < N_EXPERTS)
        def _():
            for d in fetch(e + 1):
                d.start()

        @pl.when(valid > 0)
        def _():
            wup_b[...] = wup_f[e % 2].astype(BF16)
            wdn_b[...] = wdn_f[e % 2].astype(BF16)

    @pl.when(valid > 0)
    def _():
        x = _from_row_tiles(x_ref, 0, BM)
        row = lax.broadcasted_iota(I32, (BM, 1), 0)
        xb = jnp.where(row < valid, x, jnp.zeros_like(x))
        hu = jnp.dot(xb, wup_b[...], preferred_element_type=F32) + bup_ref[0]
        gate = jnp.minimum(hu[:, 0:D_FF], SWIGLU_LIMIT)
        lin = jnp.clip(hu[:, D_FF:], -SWIGLU_LIMIT, SWIGLU_LIMIT)
        act = gate * jax.nn.sigmoid(SWIGLU_ALPHA * gate) * (lin + 1.0)
        y = jnp.dot(act.astype(BF16), wdn_b[...], preferred_element_type=F32) + bdn_ref[0]
        _to_row_tiles(o_ref, y.astype(BF16), BM)

    @pl.when(valid == 0)
    def _():
        o_ref[...] = jnp.zeros_like(o_ref)


def _experts(blk_e, blk_v, xin, w_up, b_up, w_down, b_down, n_blk):
    return pl.pallas_call(
        _expert_kernel,
        grid_spec=pltpu.PrefetchScalarGridSpec(
            num_scalar_prefetch=2, grid=(n_blk,),
            in_specs=[pl.BlockSpec((BM // PAIR * ROW_TILES, LANES), lambda i, be, bv: (i, 0)),
                      pl.BlockSpec(memory_space=pl.ANY),
                      pl.BlockSpec((1, 1, 2 * D_FF), lambda i, be, bv: (be[i], 0, 0)),
                      pl.BlockSpec(memory_space=pl.ANY),
                      pl.BlockSpec((1, 1, D_MODEL), lambda i, be, bv: (be[i], 0, 0))],
            out_specs=pl.BlockSpec((BM // PAIR * ROW_TILES, LANES), lambda i, be, bv: (i, 0)),
            scratch_shapes=[pltpu.VMEM((2, D_MODEL, 2 * D_FF), F32),
                            pltpu.VMEM((2, D_FF, D_MODEL), F32),
                            pltpu.VMEM((D_MODEL, 2 * D_FF), BF16),
                            pltpu.VMEM((D_FF, D_MODEL), BF16),
                            pltpu.SemaphoreType.DMA((2, 2))]),
        out_shape=jax.ShapeDtypeStruct((n_blk * BM // PAIR * ROW_TILES, LANES), U32),
        compiler_params=pltpu.CompilerParams(dimension_semantics=("arbitrary",),
                                             vmem_limit_bytes=VMEM_LIMIT_BYTES),
        name="experts",
    )(blk_e, blk_v, xin, w_up, b_up.reshape(N_EXPERTS, 1, 2 * D_FF), w_down,
      b_down.reshape(N_EXPERTS, 1, D_MODEL))


COMB_KC = COMB_ROWS // 256
COMB_KC_MIN = N_EXPERTS * COMB_CHUNK // 256


def _combine_kernel(tile0, s_ref, b_ref, x_ref, nx_ref, rows_ref, x1_ref, rcol_ref, mod_ref, g_ref, bb_ref,
                    eo_ref, y_ref, stg_ref, acc_ref, sem):
    i = pl.program_id(0)
    nt = pl.num_programs(0)
    slot = i % 2
    chunk = COMB_CHUNK // PAIR * ROW_TILES

    def copy(src_row, dst_row, sl):
        return pltpu.make_async_copy(eo_ref.at[pl.ds(_pair_rows(src_row), chunk)],
                                     stg_ref.at[sl, pl.ds(_pair_rows(dst_row), chunk)], sem.at[sl])

    def issue(tile, sl):
        _issue_tile(lambda src, dst: copy(src, dst, sl), tile * N_EXPERTS, s_ref, b_ref, x_ref, nx_ref[tile],
                    COMB_CHUNK)

    @pl.when(i == 0)
    def _():
        stg_ref[...] = jnp.zeros_like(stg_ref)
        issue(tile0, 0)

    rc = rcol_ref[...]
    col_k = [rc[:, 2 * TOP_K + k:2 * TOP_K + k + 1].astype(I32) for k in range(TOP_K)]
    w_k = [rc[:, 3 * TOP_K + k:3 * TOP_K + k + 1] for k in range(TOP_K)]

    def chunk_dot(sl, c):
        jl = lax.broadcasted_iota(I32, (TM, 256), 1) + c * 256
        pm = jnp.zeros((TM, 256), F32)
        for k in range(TOP_K):
            pm = pm + jnp.where(jl == col_k[k], w_k[k], 0.0)
        rows = _from_row_tiles(stg_ref.at[sl], c * 256, 256)
        return jnp.dot(pm.astype(BF16), rows, preferred_element_type=F32)

    for sl in range(2):
        @pl.when(slot == sl)
        def _():
            @pl.when(i + 1 < nt)
            def _():
                issue(tile0 + i + 1, 1 - sl)

            group = pltpu.make_async_copy(eo_ref.at[pl.ds(0, WAIT_GROUP * chunk)],
                                          stg_ref.at[sl, pl.ds(0, WAIT_GROUP * chunk)], sem.at[sl])
            _wait_tile(group, copy(0, 0, sl), nx_ref[tile0 + i])

            y = chunk_dot(sl, 0)
            for c in range(1, COMB_KC_MIN):
                y = y + chunk_dot(sl, c)
            acc_ref[...] = y
            used = (rows_ref[tile0 + i] + 255) // 256
            for c in range(COMB_KC_MIN, COMB_KC):
                @pl.when(c < used)
                def _():
                    acc_ref[...] += chunk_dot(sl, c)

    mod = mod_ref[0]
    g2 = mod[:, 5 * D_MODEL:6 * D_MODEL]
    y_ref[...] = _layer_norm(DN_ALPHA * x1_ref[...] + g2 * acc_ref[...], g_ref[...], bb_ref[...])


def _combine(seqs, tile0, n_tiles, s_te, b_te, x_te, nx_t, rows_t, x1, rcol, mod3, ln2_g, ln2_b, eo):
    def mod_map(i, *_):
        return (_tile_pos(seqs, (i + tile0) * TM)[2], 0, 0)

    return pl.pallas_call(
        functools.partial(_combine_kernel, tile0),
        grid_spec=pltpu.PrefetchScalarGridSpec(
            num_scalar_prefetch=5, grid=(n_tiles,),
            in_specs=[pl.BlockSpec((TM, D_MODEL), lambda i, *_: (i + tile0, 0)),
                      pl.BlockSpec((TM, LANES), lambda i, *_: (i + tile0, 0)),
                      pl.BlockSpec((1, 1, 6 * D_MODEL), mod_map),
                      pl.BlockSpec((1, D_MODEL), lambda i, *_: (0, 0)),
                      pl.BlockSpec((1, D_MODEL), lambda i, *_: (0, 0)),
                      pl.BlockSpec(memory_space=pl.ANY)],
            out_specs=pl.BlockSpec((TM, D_MODEL), lambda i, *_: (i, 0)),
            scratch_shapes=[pltpu.VMEM((2, COMB_ROWS // PAIR * ROW_TILES, LANES), U32),
                            pltpu.VMEM((TM, D_MODEL), F32),
                            pltpu.SemaphoreType.DMA((2,))]),
        out_shape=jax.ShapeDtypeStruct((n_tiles * TM, D_MODEL), F32),
        compiler_params=pltpu.CompilerParams(dimension_semantics=("arbitrary",),
                                             vmem_limit_bytes=VMEM_LIMIT_BYTES),
        name="combine",
    )(s_te, b_te, x_te, nx_t, rows_t, x1, rcol, mod3, ln2_g, ln2_b, eo)


def _prep_params(w_in, b_in, w_spatial, b_spatial, sgu_ln_g, sgu_ln_b, w_br_attn, w_br_sgu, w_out,
                 ln1_g, ln1_b, w_router, b_router):
    q_end, k_end, v_end = ATT_W, ATT_W + KV_W, ATT_W + 2 * KV_W

    def dup(w, lo):
        h0, h1 = w[..., lo:lo + HEAD_DIM], w[..., lo + HEAD_DIM:lo + 2 * HEAD_DIM]
        return jnp.concatenate([h0, h0, h1, h1], axis=-1)

    w_kv = jnp.concatenate([dup(w_in, q_end), dup(w_in, k_end)], axis=1).astype(BF16)
    b_kv = jnp.concatenate([dup(b_in, q_end), dup(b_in, k_end)], axis=0).reshape(1, -1)
    w_mix = jnp.concatenate([w_in[:, :q_end], w_in[:, v_end:]], axis=1).astype(BF16)
    b_mix = jnp.concatenate([b_in[:q_end], b_in[v_end:]], axis=0).reshape(1, -1)
    w_hi = w_router.astype(BF16)
    w_lo = (w_router - w_hi.astype(F32)).astype(BF16)
    p = dict(
        w_mix=w_mix, b_mix=b_mix,
        w_sp=w_spatial.astype(BF16), b_spt=b_spatial.T,
        sgu_g=sgu_ln_g.reshape(1, -1), sgu_b=sgu_ln_b.reshape(1, -1),
        w_ba=w_br_attn.astype(BF16), w_bs=w_br_sgu.astype(BF16), w_out=w_out.astype(BF16),
        ln1_g=ln1_g.reshape(1, -1), ln1_b=ln1_b.reshape(1, -1),
        w_r=jnp.concatenate([w_hi, w_lo], axis=1), b_r=b_router.reshape(1, -1),
    )
    return w_kv, b_kv, p


def _layer(seqs, xp, xs, c_all, w_ada, b_ada, w_in, b_in, sink, sgu_ln_g, sgu_ln_b, w_spatial, b_spatial,
           w_br_attn, w_br_sgu, w_out, ln1_g, ln1_b, w_router, b_router, w_up, b_up, w_down, b_down,
           ln2_g, ln2_b):
    T = seqs.n_tokens
    nt = T // TM
    ntp = seqs.n_prompt // TM
    c_pad = jnp.zeros((8, D_MODEL), F32).at[:c_all.shape[0]].set(c_all)
    mod3 = _ada(c_pad, w_ada, b_ada).reshape(8, 1, 6 * D_MODEL)
    cos_t, sin_t = _rope_tables(max(seqs.prompt_len, seqs.sample_len))
    w_kv, b_kv, p = _prep_params(w_in, b_in, w_spatial, b_spatial, sgu_ln_g, sgu_ln_b, w_br_attn,
                                 w_br_sgu, w_out, ln1_g, ln1_b, w_router, b_router)
    kt, v = _kv(seqs, xp, xs, mod3, cos_t, sin_t, w_kv, b_kv)
    x1, h2, logits = _mixer(seqs, sink, xp, xs, mod3, cos_t, sin_t, kt, v, p)
    rcol, rrow, cnt3 = _router(logits)

    cnt = cnt3[:, 0, :N_EXPERTS]
    count = cnt.sum(0)
    reg = (count + REGION_SLACK + BM - 1) // BM * BM
    pad_end = jnp.cumsum(reg)
    pad_start = pad_end - reg
    flat = lambda a: a.reshape(-1).astype(I32)
    s_te = flat(pad_start[None, :] + jnp.cumsum(cnt, axis=0) - cnt)
    d_nch = jnp.maximum((cnt + DISP_CHUNK - 1) // DISP_CHUNK, 1)
    c_nch = jnp.maximum((cnt + COMB_CHUNK - 1) // COMB_CHUNK, 1)
    db_te = flat(jnp.cumsum(cnt, axis=1) - cnt)
    cb_te = flat(COMB_CHUNK * (jnp.cumsum(c_nch, axis=1) - c_nch))
    rows_t = flat(COMB_CHUNK * c_nch.sum(1))
    n_blk = (nt * TILE_ROWS + N_EXPERTS * (REGION_SLACK + BM - 1)) // BM + 1
    tails = jnp.concatenate([pad_start + count, pad_end[:-1], jnp.array([n_blk * BM])]).astype(I32)
    blk_start = jnp.arange(n_blk, dtype=I32) * BM
    blk_e = jnp.minimum((blk_start[:, None] >= pad_end[None, :]).sum(1), N_EXPERTS - 1).astype(I32)
    owner = blk_e[:, None] == jnp.arange(N_EXPERTS)[None, :]
    blk_rows = (owner * (pad_start + count)[None, :]).sum(1) - blk_start
    blk_v = jnp.where(blk_start < pad_end[-1], jnp.clip(blk_rows, 0, BM), 0).astype(I32)

    xin = _dispatch(seqs, db_te, s_te, flat(d_nch - 1), flat((d_nch - 1).sum(1)), tails, h2, rrow, n_blk * BM)
    eo = _experts(blk_e, blk_v, xin, w_up, b_up, w_down, b_down, n_blk)
    comb = (s_te, cb_te, flat(c_nch - 1), flat((c_nch - 1).sum(1)), rows_t,
            x1, rcol, mod3, ln2_g.reshape(1, -1), ln2_b.reshape(1, -1), eo)
    return _combine(seqs, 0, ntp, *comb), _combine(seqs, ntp, nt - ntp, *comb)


def kernel(x_prompt, x_sample, c_prompt, c_sample, w_ada, b_ada, w_in, b_in, sink, sgu_ln_g, sgu_ln_b, w_spatial, b_spatial, w_br_attn, w_br_sgu, w_out, ln1_g, ln1_b, w_router, b_router, w_up, b_up, w_down, b_down, ln2_g, ln2_b):
    assert w_ada.shape[0] == DEPTH == 1
    bp, sp, d = x_prompt.shape
    bs, ss, _ = x_sample.shape
    seqs = _Seqs(n_prompt=bp * sp, prompt_len=sp, sample_len=ss, n_tokens=bp * sp + bs * ss)
    c_all = jnp.concatenate([c_prompt, c_sample], axis=0)
    yp, ys = _layer(seqs, x_prompt.reshape(bp * sp, d), x_sample.reshape(bs * ss, d), c_all,
                    w_ada[0], b_ada[0], w_in[0], b_in[0], sink[0], sgu_ln_g[0], sgu_ln_b[0],
                    w_spatial[0], b_spatial[0], w_br_attn[0], w_br_sgu[0], w_out[0], ln1_g[0], ln1_b[0],
                    w_router[0], b_router[0], w_up[0], b_up[0], w_down[0], b_down[0], ln2_g[0], ln2_b[0])
    return (yp.reshape(bp, sp, d), ys.reshape(bs, ss, d))
```

```python
import functools
import math
from typing import NamedTuple

import jax
import jax.numpy as jnp
from jax import lax
from jax.experimental import pallas as pl
from jax.experimental.pallas import tpu as pltpu

F32 = jnp.float32
BF16 = jnp.bfloat16
I32 = jnp.int32

D_MODEL = 1024
N_HEADS = 8
N_KV_HEADS = 2
HEAD_DIM = 64
ATT_W = N_HEADS * HEAD_DIM
KV_W = N_KV_HEADS * HEAD_DIM
BLOCK = 128
ROPE_THETA = 500000.0
ROT_DIM = HEAD_DIM // 4
ROT_HALF = ROT_DIM // 2
SGU_W = D_MODEL // 2
SGU_GROUPS = 4
N_EXPERTS = 32
TOP_K = 4
D_FF = D_MODEL
SWIGLU_LIMIT = 7.0
SWIGLU_ALPHA = 1.702
LN_EPS = 1e-5
DEPTH = 1
DN_ALPHA = (2 * DEPTH) ** 0.25

LANES = 128
SUBLANES = 8
ROW_TILES = D_MODEL // LANES
VMEM_LIMIT_BYTES = 56 * 1024 * 1024
U32 = jnp.uint32
PAIR = 2

TM = 256
MT = 512
TK = 512
BM = 512
ROUTER_TOKENS = 2048
DISP_CHUNK = 64
COMB_CHUNK = 48
REGION_SLACK = DISP_CHUNK
TILE_ROWS = TM * TOP_K + N_EXPERTS
DISP_ROWS = TILE_ROWS + DISP_CHUNK
COMB_ROWS = (TILE_ROWS + N_EXPERTS * COMB_CHUNK + 255) // 256 * 256
NEG_INF = float("-inf")


class _Seqs(NamedTuple):
    n_prompt: int
    prompt_len: int
    sample_len: int
    n_tokens: int


def _tile_pos(seqs, t0):
    is_s = t0 >= seqs.n_prompt
    seq_len = jnp.where(is_s, seqs.sample_len, seqs.prompt_len)
    off = jnp.where(is_s, t0 - seqs.n_prompt, t0)
    pos0 = off % seq_len
    row = jnp.where(is_s, seqs.n_prompt // seqs.prompt_len + off // seq_len, off // seq_len)
    return seq_len, pos0, row


def _ada_kernel(c_ref, w_ref, b_ref, o_ref):
    c = c_ref[...]
    a = c * jax.nn.sigmoid(c)
    o_ref[...] = jnp.dot(a, w_ref[...], preferred_element_type=F32,
                         precision=lax.Precision.HIGHEST) + b_ref[...]


def _ada(c_pad, w_ada, b_ada):
    n = w_ada.shape[1]
    bn = 1536
    return pl.pallas_call(
        _ada_kernel,
        grid=(n // bn,),
        in_specs=[pl.BlockSpec((8, D_MODEL), lambda j: (0, 0)),
                  pl.BlockSpec((D_MODEL, bn), lambda j: (0, j)),
                  pl.BlockSpec((1, bn), lambda j: (0, j))],
        out_specs=pl.BlockSpec((8, bn), lambda j: (0, j)),
        out_shape=jax.ShapeDtypeStruct((8, n), F32),
        compiler_params=pltpu.CompilerParams(vmem_limit_bytes=VMEM_LIMIT_BYTES),
        name="ada",
    )(c_pad, w_ada, b_ada.reshape(1, n))


def _rope_tables(length):
    lane = jnp.arange(LANES) % HEAD_DIM
    inv = ROPE_THETA ** (-(lane % ROT_HALF).astype(F32) * 2.0 / ROT_DIM)
    inv = jnp.where(lane < ROT_DIM, inv, 0.0)
    sign = jnp.where(lane < ROT_HALF, -1.0, 1.0)
    ang = jnp.arange(length, dtype=F32)[:, None] * inv[None, :]
    return jnp.cos(ang), jnp.sin(ang) * sign[None, :]


def _rope(x, cos, sin):
    n = x.shape[1]
    reps = n // LANES
    c = jnp.concatenate([cos] * reps, axis=1)
    s = jnp.concatenate([sin] * reps, axis=1)
    lane = lax.broadcasted_iota(I32, x.shape, 1)
    first = (lane & (HEAD_DIM - 1)) < ROT_HALF
    partner = jnp.where(first, pltpu.roll(x, n - ROT_HALF, 1), pltpu.roll(x, ROT_HALF, 1))
    return x * c + partner * s


def _kv_kernel(seqs, xp_ref, xs_ref, mod_ref, cos_ref, sin_ref, w_ref, b_ref, kt_ref, v_ref):
    mod = mod_ref[0]
    sh1 = mod[:, 0:D_MODEL]
    sc1 = mod[:, D_MODEL:2 * D_MODEL]
    x = jnp.where(pl.program_id(0) * TK >= seqs.n_prompt, xs_ref[...], xp_ref[...])
    h = (x * (1.0 + sc1) + sh1).astype(BF16)
    kv = jnp.dot(h, w_ref[...], preferred_element_type=F32) + b_ref[...]
    k = _rope(kv[:, 0:2 * LANES], cos_ref[...], sin_ref[...])
    kt_ref[...] = k.T.astype(BF16)
    v_ref[...] = kv[:, 2 * LANES:4 * LANES].astype(BF16)


def _group_x_specs(seqs, tile):
    ntp = seqs.n_prompt // tile
    return [pl.BlockSpec((tile, D_MODEL), lambda i, *_: (jnp.minimum(i, ntp - 1), 0)),
            pl.BlockSpec((tile, D_MODEL), lambda i, *_: (jnp.maximum(i - ntp, 0), 0))]


def _kv(seqs, xp, xs, mod3, cos_t, sin_t, w_kv, b_kv):
    T = seqs.n_tokens

    def mod_map(i):
        return (_tile_pos(seqs, i * TK)[2], 0, 0)

    def rope_map(i):
        return (_tile_pos(seqs, i * TK)[1] // TK, 0)

    return pl.pallas_call(
        functools.partial(_kv_kernel, seqs),
        grid=(T // TK,),
        in_specs=_group_x_specs(seqs, TK) + [
                  pl.BlockSpec((1, 1, 6 * D_MODEL), mod_map),
                  pl.BlockSpec((TK, LANES), rope_map),
                  pl.BlockSpec((TK, LANES), rope_map),
                  pl.BlockSpec((D_MODEL, 4 * LANES), lambda i: (0, 0)),
                  pl.BlockSpec((1, 4 * LANES), lambda i: (0, 0))],
        out_specs=[pl.BlockSpec((2 * LANES, TK), lambda i: (0, i)),
                   pl.BlockSpec((TK, 2 * LANES), lambda i: (i, 0))],
        out_shape=[jax.ShapeDtypeStruct((2 * LANES, T), BF16),
                   jax.ShapeDtypeStruct((T, 2 * LANES), BF16)],
        compiler_params=pltpu.CompilerParams(dimension_semantics=("arbitrary",),
                                             vmem_limit_bytes=VMEM_LIMIT_BYTES),
        name="kv",
    )(xp, xs, mod3, cos_t, sin_t, w_kv, b_kv)


def _layer_norm(x, g, b):
    mu = jnp.mean(x, axis=-1, keepdims=True)
    xc = x - mu
    var = jnp.mean(xc * xc, axis=-1, keepdims=True)
    return xc * lax.rsqrt(var + LN_EPS) * g + b


def _attention(q, kfull, vfull, valids, sink_ref):
    lane = lax.broadcasted_iota(I32, (BLOCK, LANES), 1)
    lo = lane < HEAD_DIM
    ones = jnp.ones((3 * BLOCK, LANES), BF16)
    units = [(jb, hk) for jb in range(MT // BLOCK) for hk in range(N_KV_HEADS)]
    scores, sinks = [], []
    for jb, hk in units:
        parts = []
        for p in range(2):
            qp = q[jb * BLOCK:(jb + 1) * BLOCK, (2 * hk + p) * LANES:(2 * hk + p + 1) * LANES]
            parts.append(jnp.where(lo, qp, 0.0).astype(BF16))
            parts.append(jnp.where(lo, 0.0, qp).astype(BF16))
        kwin = kfull[hk * LANES:(hk + 1) * LANES, jb * BLOCK:(jb + 3) * BLOCK]
        s = jnp.dot(jnp.concatenate(parts, axis=0), kwin, preferred_element_type=F32)
        scores.append(jnp.where(jnp.concatenate([valids[jb]] * 4, axis=0), s, NEG_INF))
        sinks.extend(jnp.full((BLOCK, 1), sink_ref[hk * 4 + g], F32) for g in range(4))
    s = jnp.concatenate(scores, axis=0)
    sk = jnp.concatenate(sinks, axis=0)
    m = jnp.maximum(jnp.max(s, axis=-1, keepdims=True), sk)
    p = jnp.exp(s - m).astype(BF16)
    sink_term = jnp.exp(sk - m)
    rows = []
    for ui, (jb, hk) in enumerate(units):
        vwin = jnp.concatenate([vfull[jb * BLOCK:(jb + 3) * BLOCK, hk * LANES:(hk + 1) * LANES], ones], axis=1)
        r0 = ui * 4 * BLOCK
        ov = jnp.dot(p[r0:r0 + 4 * BLOCK], vwin, preferred_element_type=F32)
        o = ov[:, 0:LANES] / (ov[:, LANES:2 * LANES] + sink_term[r0:r0 + 4 * BLOCK])
        pair = [jnp.where(lo, o[(2 * p2) * BLOCK:(2 * p2 + 1) * BLOCK],
                          o[(2 * p2 + 1) * BLOCK:(2 * p2 + 2) * BLOCK]) for p2 in range(2)]
        rows.append(jnp.concatenate(pair, axis=1))
    n_h = N_KV_HEADS
    return jnp.concatenate(
        [jnp.concatenate(rows[jb * n_h:(jb + 1) * n_h], axis=1) for jb in range(MT // BLOCK)], axis=0)


def _mixer_kernel(seqs, sink_ref, xp_ref, xs_ref, mod_ref, cos_ref, sin_ref,
                  ktp_ref, ktc_ref, ktn_ref, vp_ref, vc_ref, vn_ref,
                  wmix_ref, bmix_ref, wsp_ref, bspt_ref, sg_ref, sb_ref,
                  wba_ref, wbs_ref, wout_ref, l1g_ref, l1b_ref, wr_ref, br_ref,
                  x1_ref, h2_ref, logit_ref):
    i = pl.program_id(0)
    seq_len, pos0, _ = _tile_pos(seqs, i * MT)
    mod = mod_ref[0]
    sh1, sc1, g1 = (mod[:, j * D_MODEL:(j + 1) * D_MODEL] for j in range(3))
    sh2, sc2 = (mod[:, j * D_MODEL:(j + 1) * D_MODEL] for j in range(3, 5))
    x = jnp.where(i * MT >= seqs.n_prompt, xs_ref[...], xp_ref[...])
    h = (x * (1.0 + sc1) + sh1).astype(BF16)
    kv_end = ATT_W + 2 * KV_W
    zq = jnp.dot(h, wmix_ref[:, 0:ATT_W], preferred_element_type=F32) + bmix_ref[:, 0:ATT_W]
    z = jnp.dot(h, wmix_ref[:, kv_end:], preferred_element_type=F32) + bmix_ref[:, kv_end:]
    q = _rope(zq, cos_ref[...], sin_ref[...]) * (HEAD_DIM ** -0.5)
    u = jax.nn.gelu(z[:, 0:SGU_W])
    vs = _layer_norm(jax.nn.gelu(z[:, SGU_W:2 * SGU_W]), sg_ref[...], sb_ref[...])
    ga = z[:, 2 * SGU_W:2 * SGU_W + D_MODEL]
    gs = z[:, 2 * SGU_W + D_MODEL:]

    kfull = jnp.concatenate([ktp_ref[...], ktc_ref[...], ktn_ref[...]], axis=1)
    vfull = jnp.concatenate([vp_ref[...], vc_ref[...], vn_ref[...]], axis=0)
    qi = lax.broadcasted_iota(I32, (BLOCK, 3 * BLOCK), 0)
    ki = lax.broadcasted_iota(I32, (BLOCK, 3 * BLOCK), 1)
    band = (ki >= qi) & (ki <= qi + 2 * BLOCK)
    vs_b = vs.astype(BF16)
    valids, sgu_rows = [], []
    for jb in range(MT // BLOCK):
        posb = pos0 + jb * BLOCK
        valids.append(band & (ki >= jnp.where(posb == 0, BLOCK, 0))
                      & (ki < jnp.where(posb + BLOCK == seq_len, 2 * BLOCK, 3 * BLOCK)))
        groups = []
        for g in range(SGU_GROUPS):
            vg = vs_b[jb * BLOCK:(jb + 1) * BLOCK, g * LANES:(g + 1) * LANES]
            sv = jnp.dot(wsp_ref[g], vg, preferred_element_type=F32) + bspt_ref[:, g:g + 1]
            groups.append(sv)
        sgu_rows.append(jnp.concatenate(groups, axis=1))
    attn = _attention(q, kfull, vfull, valids, sink_ref)
    sgu = u * jnp.concatenate(sgu_rows, axis=0)

    a1 = jnp.dot(attn.astype(BF16), wba_ref[...], preferred_element_type=F32)
    a2 = jnp.dot(sgu.astype(BF16), wbs_ref[...], preferred_element_type=F32)
    merged = jax.nn.sigmoid(ga) * a1 + jax.nn.sigmoid(gs) * a2
    mix = jnp.dot(merged.astype(BF16), wout_ref[...], preferred_element_type=F32)
    x1 = _layer_norm(DN_ALPHA * x + g1 * mix, l1g_ref[...], l1b_ref[...])
    x1_ref[...] = x1
    h2 = x1 * (1.0 + sc2) + sh2
    hi = h2.astype(BF16)
    h2_ref[...] = hi

    lo_part = (h2 - hi.astype(F32)).astype(BF16)
    l1 = jnp.dot(hi, wr_ref[...], preferred_element_type=F32)
    l2 = jnp.dot(lo_part, wr_ref[:, 0:N_EXPERTS], preferred_element_type=F32)
    logit_ref[...] = l1[:, 0:N_EXPERTS] + l1[:, N_EXPERTS:2 * N_EXPERTS] + l2 + br_ref[...]


PACK = 4096.0
assert DISP_ROWS <= PACK and COMB_ROWS <= PACK and DISP_ROWS * PACK < 2 ** 24


def _router_kernel(l_ref, rcol_ref, rrow_ref, cnt_ref):
    rt = l_ref.shape[0]
    ns = rt // TM
    eidx = lax.broadcasted_iota(I32, (rt, N_EXPERTS), 1)
    work = l_ref[...]
    idxs, vals = [], []
    for _ in range(TOP_K):
        m = jnp.max(work, axis=-1, keepdims=True)
        ix = jnp.min(jnp.where(work == m, eidx, N_EXPERTS), axis=-1, keepdims=True)
        idxs.append(ix)
        vals.append(m)
        work = jnp.where(eidx == ix, NEG_INF, work)
    exps = [jnp.exp(v - vals[0]) for v in vals]
    esum = exps[0] + exps[1] + exps[2] + exps[3]
    wts = [e / esum for e in exps]

    sel = jnp.zeros((rt, N_EXPERTS), F32)
    for ix in idxs:
        sel = sel + jnp.where(eidx == ix, 1.0, 0.0)
    sel_b = sel.astype(BF16)
    ti = lax.broadcasted_iota(I32, (TM, TM), 0)
    tj = lax.broadcasted_iota(I32, (TM, TM), 1)
    lower = jnp.where(tj < ti, 1.0, 0.0).astype(BF16)
    tiles = [slice(s * TM, (s + 1) * TM) for s in range(ns)]
    rank = jnp.concatenate([jnp.dot(lower, sel_b[t], preferred_element_type=F32) for t in tiles], axis=0)
    cnt = jnp.concatenate([jnp.sum(sel[t], axis=0, keepdims=True) for t in tiles], axis=0)
    cnt = cnt + (cnt - 2.0 * jnp.floor(cnt * 0.5))
    nch = jnp.maximum(jnp.floor((cnt + (COMB_CHUNK - 0.5)) * (1.0 / COMB_CHUNK)), 1.0)
    ei = lax.broadcasted_iota(I32, (N_EXPERTS, N_EXPERTS), 0)
    ej = lax.broadcasted_iota(I32, (N_EXPERTS, N_EXPERTS), 1)
    before = jnp.where(ei < ej, 1.0, 0.0).astype(BF16)
    fill = (-2 * ns) % 16
    pre = jnp.concatenate([cnt, nch] + ([jnp.zeros((fill, N_EXPERTS), F32)] if fill else []),
                          axis=0).astype(BF16)
    base = jnp.dot(pre, before, preferred_element_type=F32)
    dbase = jnp.concatenate([jnp.broadcast_to(base[s:s + 1], (TM, N_EXPERTS)) for s in range(ns)], axis=0)
    cbase = jnp.concatenate([jnp.broadcast_to(base[ns + s:ns + s + 1] * COMB_CHUNK, (TM, N_EXPERTS))
                             for s in range(ns)], axis=0)
    both = (rank + dbase) * PACK + (rank + cbase)

    cols = [ix.astype(F32) for ix in idxs]
    packed = [jnp.sum(jnp.where(eidx == ix, both, 0.0), axis=-1, keepdims=True) for ix in idxs]
    drow = [jnp.floor(v * (1.0 / PACK)) for v in packed]
    cols.extend(drow)
    cols.extend(v - d * PACK for v, d in zip(packed, drow))
    cols.extend(wts)
    lane = lax.broadcasted_iota(I32, (rt, LANES), 1)
    rc = jnp.zeros((rt, LANES), F32)
    for j, c in enumerate(cols):
        rc = jnp.where(lane == j, c, rc)
    rcol_ref[...] = rc
    for s in range(ns):
        rrow_ref[s] = rc[tiles[s]].T[0:16]
        cnt_ref[s] = jnp.broadcast_to(
            jnp.concatenate([cnt[s:s + 1], jnp.zeros((1, LANES - N_EXPERTS), F32)], axis=1),
            (8, LANES)).astype(I32)


def _router(logits):
    T = logits.shape[0]
    rt = math.gcd(T, ROUTER_TOKENS)
    nt = T // TM
    return pl.pallas_call(
        _router_kernel,
        grid=(T // rt,),
        in_specs=[pl.BlockSpec((rt, N_EXPERTS), lambda i: (i, 0))],
        out_specs=[pl.BlockSpec((rt, LANES), lambda i: (i, 0)),
                   pl.BlockSpec((rt // TM, 16, TM), lambda i: (i, 0, 0)),
                   pl.BlockSpec((rt // TM, 8, LANES), lambda i: (i, 0, 0))],
        out_shape=[jax.ShapeDtypeStruct((T, LANES), F32),
                   jax.ShapeDtypeStruct((nt, 16, TM), F32),
                   jax.ShapeDtypeStruct((nt, 8, LANES), I32)],
        compiler_params=pltpu.CompilerParams(dimension_semantics=("arbitrary",),
                                             vmem_limit_bytes=VMEM_LIMIT_BYTES),
        name="router",
    )(logits)


def _mixer(seqs, sink, xp, xs, mod3, cos_t, sin_t, kt, v, p):
    T = seqs.n_tokens
    nt = T // MT
    nb = T // BLOCK
    r = MT // BLOCK

    def mod_map(i, s):
        return (_tile_pos(seqs, i * MT)[2], 0, 0)

    def rope_map(i, s):
        return (_tile_pos(seqs, i * MT)[1] // MT, 0)

    const2 = lambda i, s: (0, 0)
    once = dict(pipeline_mode=pl.Buffered(1))
    in_specs = _group_x_specs(seqs, MT) + [
        pl.BlockSpec((1, 1, 6 * D_MODEL), mod_map),
        pl.BlockSpec((MT, LANES), rope_map),
        pl.BlockSpec((MT, LANES), rope_map),
        pl.BlockSpec((2 * LANES, BLOCK), lambda i, s: (0, jnp.maximum(i * r - 1, 0))),
        pl.BlockSpec((2 * LANES, MT), lambda i, s: (0, i)),
        pl.BlockSpec((2 * LANES, BLOCK), lambda i, s: (0, jnp.minimum(i * r + r, nb - 1))),
        pl.BlockSpec((BLOCK, 2 * LANES), lambda i, s: (jnp.maximum(i * r - 1, 0), 0)),
        pl.BlockSpec((MT, 2 * LANES), lambda i, s: (i, 0)),
        pl.BlockSpec((BLOCK, 2 * LANES), lambda i, s: (jnp.minimum(i * r + r, nb - 1), 0)),
        pl.BlockSpec(p["w_mix"].shape, const2, **once),
        pl.BlockSpec(p["b_mix"].shape, const2, **once),
        pl.BlockSpec(p["w_sp"].shape, lambda i, s: (0, 0, 0), **once),
        pl.BlockSpec(p["b_spt"].shape, const2, **once),
        pl.BlockSpec(p["sgu_g"].shape, const2, **once),
        pl.BlockSpec(p["sgu_b"].shape, const2, **once),
        pl.BlockSpec(p["w_ba"].shape, const2, **once),
        pl.BlockSpec(p["w_bs"].shape, const2, **once),
        pl.BlockSpec(p["w_out"].shape, const2, **once),
        pl.BlockSpec(p["ln1_g"].shape, const2, **once),
        pl.BlockSpec(p["ln1_b"].shape, const2, **once),
        pl.BlockSpec(p["w_r"].shape, const2, **once),
        pl.BlockSpec(p["b_r"].shape, const2, **once),
    ]
    out_specs = [
        pl.BlockSpec((MT, D_MODEL), lambda i, s: (i, 0)),
        pl.BlockSpec((MT, D_MODEL), lambda i, s: (i, 0)),
        pl.BlockSpec((MT, N_EXPERTS), lambda i, s: (i, 0)),
    ]
    out_shape = [
        jax.ShapeDtypeStruct((T, D_MODEL), F32),
        jax.ShapeDtypeStruct((T, D_MODEL), BF16),
        jax.ShapeDtypeStruct((T, N_EXPERTS), F32),
    ]
    return pl.pallas_call(
        functools.partial(_mixer_kernel, seqs),
        grid_spec=pltpu.PrefetchScalarGridSpec(
            num_scalar_prefetch=1, grid=(nt,), in_specs=in_specs, out_specs=out_specs),
        out_shape=out_shape,
        compiler_params=pltpu.CompilerParams(dimension_semantics=("arbitrary",),
                                             vmem_limit_bytes=VMEM_LIMIT_BYTES),
        name="mixer",
    )(sink, xp, xs, mod3, cos_t, sin_t, kt, kt, kt, v, v, v,
      p["w_mix"], p["b_mix"], p["w_sp"], p["b_spt"], p["sgu_g"], p["sgu_b"],
      p["w_ba"], p["w_bs"], p["w_out"], p["ln1_g"], p["ln1_b"], p["w_r"], p["b_r"])


def _pair_rows(row):
    return pl.multiple_of((row // PAIR) * ROW_TILES, ROW_TILES)


def _to_row_tiles(dst_ref, rows, n):
    words = pltpu.bitcast(rows, U32)
    for c in range(ROW_TILES):
        dst_ref[pl.ds(c, n // PAIR, stride=ROW_TILES), :] = words[:, c * LANES:(c + 1) * LANES]


def _from_row_tiles(src_ref, start, n):
    base = start // PAIR * ROW_TILES
    words = jnp.concatenate(
        [src_ref[pl.ds(base + c, n // PAIR, stride=ROW_TILES), :] for c in range(ROW_TILES)], axis=1)
    return pltpu.bitcast(words, BF16)


WAIT_GROUP = 16


def _issue_tile(copy, k0, src_ref, dst_ref, extra_ref, n_extra, step):
    for e in range(N_EXPERTS):
        copy(src_ref[k0 + e], dst_ref[k0 + e]).start()

    @pl.when(n_extra > 0)
    def _():
        def per_expert(e, c):
            def per_chunk(j, c2):
                copy(src_ref[k0 + e] + j * step, dst_ref[k0 + e] + j * step).start()
                return c2
            lax.fori_loop(1, extra_ref[k0 + e] + 1, per_chunk, 0)
            return c
        lax.fori_loop(0, N_EXPERTS, per_expert, 0)


def _wait_tile(group_copy, chunk_copy, n_extra):
    for _ in range(N_EXPERTS // WAIT_GROUP):
        group_copy.wait()

    def body(_, c):
        chunk_copy.wait()
        return c
    lax.fori_loop(0, n_extra, body, 0)


def _dispatch_kernel(b_ref, s_ref, x_ref, nx_ref, tail_ref, h2_ref, rrow_ref, xin_ref, stg_ref, zero_ref, sem):
    i = pl.program_id(0)
    nt = pl.num_programs(0)
    slot = i % 2
    rr = rrow_ref[0]
    rho = lax.broadcasted_iota(I32, (DISP_ROWS, TM), 0)
    pt = jnp.zeros((DISP_ROWS, TM), F32)
    for k in range(TOP_K):
        pt = pt + jnp.where(rho == rr[TOP_K + k:TOP_K + k + 1].astype(I32), 1.0, 0.0)
    rows = jnp.dot(pt.astype(BF16), h2_ref[...], preferred_element_type=F32).astype(BF16)

    chunk = DISP_CHUNK // PAIR * ROW_TILES

    def copy(src_row, dst_row, sl):
        return pltpu.make_async_copy(stg_ref.at[sl, pl.ds(_pair_rows(src_row), chunk)],
                                     xin_ref.at[pl.ds(_pair_rows(dst_row), chunk)], sem)

    def wait_tile(tile):
        group = pltpu.make_async_copy(stg_ref.at[0, pl.ds(0, WAIT_GROUP * chunk)],
                                      xin_ref.at[pl.ds(0, WAIT_GROUP * chunk)], sem)
        _wait_tile(group, copy(0, 0, 0), nx_ref[tile])

    for sl in range(2):
        @pl.when(slot == sl)
        def _():
            _to_row_tiles(stg_ref.at[sl], rows, DISP_ROWS)

            @pl.when(i > 0)
            def _():
                wait_tile(i - 1)
            _issue_tile(lambda src, dst: copy(src, dst, sl), i * N_EXPERTS, b_ref, s_ref, x_ref, nx_ref[i],
                        DISP_CHUNK)

    @pl.when(i == nt - 1)
    def _():
        wait_tile(i)
        zero_ref[...] = jnp.zeros_like(zero_ref)

        def zcopy(dst_row):
            return pltpu.make_async_copy(zero_ref, xin_ref.at[pl.ds(_pair_rows(dst_row), chunk)], sem)

        def zwait(count):
            def body(_, c):
                zcopy(0).wait()
                return c
            lax.fori_loop(0, count, body, 0)

        def per_expert(e, total):
            lo = tail_ref[e]
            nz = (tail_ref[N_EXPERTS + e] - lo) // DISP_CHUNK

            def per_chunk(j, c):
                zcopy(lo + j * DISP_CHUNK).start()
                return c
            lax.fori_loop(0, nz, per_chunk, 0)
            return total + nz
        zwait(lax.fori_loop(0, N_EXPERTS, per_expert, 0))

        def last_chunk(e, c):
            zcopy(tail_ref[N_EXPERTS + e] - DISP_CHUNK).start()
            return c
        lax.fori_loop(0, N_EXPERTS, last_chunk, 0)
        zwait(N_EXPERTS)


def _dispatch(seqs, b_te, s_te, x_te, nx_t, tails, h2, rrow, n_rows):
    nt = seqs.n_tokens // TM
    return pl.pallas_call(
        _dispatch_kernel,
        grid_spec=pltpu.PrefetchScalarGridSpec(
            num_scalar_prefetch=5, grid=(nt,),
            in_specs=[pl.BlockSpec((TM, D_MODEL), lambda i, *_: (i, 0)),
                      pl.BlockSpec((1, 16, TM), lambda i, *_: (i, 0, 0))],
            out_specs=pl.BlockSpec(memory_space=pl.ANY),
            scratch_shapes=[pltpu.VMEM((2, DISP_ROWS // PAIR * ROW_TILES, LANES), U32),
                            pltpu.VMEM((DISP_CHUNK // PAIR * ROW_TILES, LANES), U32),
                            pltpu.SemaphoreType.DMA]),
        out_shape=jax.ShapeDtypeStruct((n_rows // PAIR * ROW_TILES, LANES), U32),
        compiler_params=pltpu.CompilerParams(dimension_semantics=("arbitrary",),
                                             vmem_limit_bytes=VMEM_LIMIT_BYTES),
        name="dispatch",
    )(b_te, s_te, x_te, nx_t, tails, h2, rrow)


def _expert_kernel(be_ref, bv_ref, x_ref, wup_hbm, bup_ref, wdn_hbm, bdn_ref, o_ref,
                   wup_f, wdn_f, wup_b, wdn_b, sem):
    i = pl.program_id(0)
    e = be_ref[i]
    valid = bv_ref[i]
    first = jnp.logical_or(i == 0, e != be_ref[jnp.maximum(i - 1, 0)])

    def fetch(ex):
        slot = ex % 2
        return (pltpu.make_async_copy(wup_hbm.at[ex], wup_f.at[slot], sem.at[0, slot]),
                pltpu.make_async_copy(wdn_hbm.at[ex], wdn_f.at[slot], sem.at[1, slot]))

    @pl.when(i == 0)
    def _():
        for d in fetch(e):
            d.start()

    @pl.when(first)
    def _():
        for d in fetch(e):
            d.wait()

        @pl.when(e + 1 < N_EXPERTS)
        def _():
            for d in fetch(e + 1):
                d.start()

        @pl.when(valid > 0)
        def _():
            wup_b[...] = wup_f[e % 2].astype(BF16)
            wdn_b[...] = wdn_f[e % 2].astype(BF16)

    @pl.when(valid > 0)
    def _():
        x = _from_row_tiles(x_ref, 0, BM)
        row = lax.broadcasted_iota(I32, (BM, 1), 0)
        xb = jnp.where(row < valid, x, jnp.zeros_like(x))
        hu = jnp.dot(xb, wup_b[...], preferred_element_type=F32) + bup_ref[0]
        gate = jnp.minimum(hu[:, 0:D_FF], SWIGLU_LIMIT)
        lin = jnp.clip(hu[:, D_FF:], -SWIGLU_LIMIT, SWIGLU_LIMIT)
        act = gate * jax.nn.sigmoid(SWIGLU_ALPHA * gate) * (lin + 1.0)
        y = jnp.dot(act.astype(BF16), wdn_b[...], preferred_element_type=F32) + bdn_ref[0]
        _to_row_tiles(o_ref, y.astype(BF16), BM)

    @pl.when(valid == 0)
    def _():
        o_ref[...] = jnp.zeros_like(o_ref)


def _experts(blk_e, blk_v, xin, w_up, b_up, w_down, b_down, n_blk):
    return pl.pallas_call(
        _expert_kernel,
        grid_spec=pltpu.PrefetchScalarGridSpec(
            num_scalar_prefetch=2, grid=(n_blk,),
            in_specs=[pl.BlockSpec((BM // PAIR * ROW_TILES, LANES), lambda i, be, bv: (i, 0)),
                      pl.BlockSpec(memory_space=pl.ANY),
                      pl.BlockSpec((1, 1, 2 * D_FF), lambda i, be, bv: (be[i], 0, 0)),
                      pl.BlockSpec(memory_space=pl.ANY),
                      pl.BlockSpec((1, 1, D_MODEL), lambda i, be, bv: (be[i], 0, 0))],
            out_specs=pl.BlockSpec((BM // PAIR * ROW_TILES, LANES), lambda i, be, bv: (i, 0)),
            scratch_shapes=[pltpu.VMEM((2, D_MODEL, 2 * D_FF), F32),
                            pltpu.VMEM((2, D_FF, D_MODEL), F32),
                            pltpu.VMEM((D_MODEL, 2 * D_FF), BF16),
                            pltpu.VMEM((D_FF, D_MODEL), BF16),
                            pltpu.SemaphoreType.DMA((2, 2))]),
        out_shape=jax.ShapeDtypeStruct((n_blk * BM // PAIR * ROW_TILES, LANES), U32),
        compiler_params=pltpu.CompilerParams(dimension_semantics=("arbitrary",),
                                             vmem_limit_bytes=VMEM_LIMIT_BYTES),
        name="experts",
    )(blk_e, blk_v, xin, w_up, b_up.reshape(N_EXPERTS, 1, 2 * D_FF), w_down,
      b_down.reshape(N_EXPERTS, 1, D_MODEL))


COMB_KC = COMB_ROWS // 256
COMB_KC_MIN = N_EXPERTS * COMB_CHUNK // 256


def _combine_kernel(tile0, s_ref, b_ref, x_ref, nx_ref, rows_ref, x1_ref, rcol_ref, mod_ref, g_ref, bb_ref,
                    eo_ref, y_ref, stg_ref, acc_ref, sem):
    i = pl.program_id(0)
    nt = pl.num_programs(0)
    slot = i % 2
    chunk = COMB_CHUNK // PAIR * ROW_TILES

    def copy(src_row, dst_row, sl):
        return pltpu.make_async_copy(eo_ref.at[pl.ds(_pair_rows(src_row), chunk)],
                                     stg_ref.at[sl, pl.ds(_pair_rows(dst_row), chunk)], sem.at[sl])

    def issue(tile, sl):
        _issue_tile(lambda src, dst: copy(src, dst, sl), tile * N_EXPERTS, s_ref, b_ref, x_ref, nx_ref[tile],
                    COMB_CHUNK)

    @pl.when(i == 0)
    def _():
        stg_ref[...] = jnp.zeros_like(stg_ref)
        issue(tile0, 0)

    rc = rcol_ref[...]
    col_k = [rc[:, 2 * TOP_K + k:2 * TOP_K + k + 1].astype(I32) for k in range(TOP_K)]
    w_k = [rc[:, 3 * TOP_K + k:3 * TOP_K + k + 1] for k in range(TOP_K)]

    def chunk_dot(sl, c):
        jl = lax.broadcasted_iota(I32, (TM, 256), 1) + c * 256
        pm = jnp.zeros((TM, 256), F32)
        for k in range(TOP_K):
            pm = pm + jnp.where(jl == col_k[k], w_k[k], 0.0)
        rows = _from_row_tiles(stg_ref.at[sl], c * 256, 256)
        return jnp.dot(pm.astype(BF16), rows, preferred_element_type=F32)

    for sl in range(2):
        @pl.when(slot == sl)
        def _():
            @pl.when(i + 1 < nt)
            def _():
                issue(tile0 + i + 1, 1 - sl)

            group = pltpu.make_async_copy(eo_ref.at[pl.ds(0, WAIT_GROUP * chunk)],
                                          stg_ref.at[sl, pl.ds(0, WAIT_GROUP * chunk)], sem.at[sl])
            _wait_tile(group, copy(0, 0, sl), nx_ref[tile0 + i])

            y = chunk_dot(sl, 0)
            for c in range(1, COMB_KC_MIN):
                y = y + chunk_dot(sl, c)
            acc_ref[...] = y
            used = (rows_ref[tile0 + i] + 255) // 256
            for c in range(COMB_KC_MIN, COMB_KC):
                @pl.when(c < used)
                def _():
                    acc_ref[...] += chunk_dot(sl, c)

    mod = mod_ref[0]
    g2 = mod[:, 5 * D_MODEL:6 * D_MODEL]
    y_ref[...] = _layer_norm(DN_ALPHA * x1_ref[...] + g2 * acc_ref[...], g_ref[...], bb_ref[...])


def _combine(seqs, tile0, n_tiles, s_te, b_te, x_te, nx_t, rows_t, x1, rcol, mod3, ln2_g, ln2_b, eo):
    def mod_map(i, *_):
        return (_tile_pos(seqs, (i + tile0) * TM)[2], 0, 0)

    return pl.pallas_call(
        functools.partial(_combine_kernel, tile0),
        grid_spec=pltpu.PrefetchScalarGridSpec(
            num_scalar_prefetch=5, grid=(n_tiles,),
            in_specs=[pl.BlockSpec((TM, D_MODEL), lambda i, *_: (i + tile0, 0)),
                      pl.BlockSpec((TM, LANES), lambda i, *_: (i + tile0, 0)),
                      pl.BlockSpec((1, 1, 6 * D_MODEL), mod_map),
                      pl.BlockSpec((1, D_MODEL), lambda i, *_: (0, 0)),
                      pl.BlockSpec((1, D_MODEL), lambda i, *_: (0, 0)),
                      pl.BlockSpec(memory_space=pl.ANY)],
            out_specs=pl.BlockSpec((TM, D_MODEL), lambda i, *_: (i, 0)),
            scratch_shapes=[pltpu.VMEM((2, COMB_ROWS // PAIR * ROW_TILES, LANES), U32),
                            pltpu.VMEM((TM, D_MODEL), F32),
                            pltpu.SemaphoreType.DMA((2,))]),
        out_shape=jax.ShapeDtypeStruct((n_tiles * TM, D_MODEL), F32),
        compiler_params=pltpu.CompilerParams(dimension_semantics=("arbitrary",),
                                             vmem_limit_bytes=VMEM_LIMIT_BYTES),
        name="combine",
    )(s_te, b_te, x_te, nx_t, rows_t, x1, rcol, mod3, ln2_g, ln2_b, eo)


def _prep_params(w_in, b_in, w_spatial, b_spatial, sgu_ln_g, sgu_ln_b, w_br_attn, w_br_sgu, w_out,
                 ln1_g, ln1_b, w_router, b_router):
    q_end, k_end, v_end = ATT_W, ATT_W + KV_W, ATT_W + 2 * KV_W

    def dup(w, lo):
        h0, h1 = w[..., lo:lo + HEAD_DIM], w[..., lo + HEAD_DIM:lo + 2 * HEAD_DIM]
        return jnp.concatenate([h0, h0, h1, h1], axis=-1)

    w_kv = jnp.concatenate([dup(w_in, q_end), dup(w_in, k_end)], axis=1).astype(BF16)
    b_kv = jnp.concatenate([dup(b_in, q_end), dup(b_in, k_end)], axis=0).reshape(1, -1)
    w_mix = w_in.astype(BF16)
    b_mix = b_in.reshape(1, -1)
    w_hi = w_router.astype(BF16)
    w_lo = (w_router - w_hi.astype(F32)).astype(BF16)
    p = dict(
        w_mix=w_mix, b_mix=b_mix,
        w_sp=w_spatial.astype(BF16), b_spt=b_spatial.T,
        sgu_g=sgu_ln_g.reshape(1, -1), sgu_b=sgu_ln_b.reshape(1, -1),
        w_ba=w_br_attn.astype(BF16), w_bs=w_br_sgu.astype(BF16), w_out=w_out.astype(BF16),
        ln1_g=ln1_g.reshape(1, -1), ln1_b=ln1_b.reshape(1, -1),
        w_r=jnp.concatenate([w_hi, w_lo], axis=1), b_r=b_router.reshape(1, -1),
    )
    return w_kv, b_kv, p


def _layer(seqs, xp, xs, c_all, w_ada, b_ada, w_in, b_in, sink, sgu_ln_g, sgu_ln_b, w_spatial, b_spatial,
           w_br_attn, w_br_sgu, w_out, ln1_g, ln1_b, w_router, b_router, w_up, b_up, w_down, b_down,
           ln2_g, ln2_b):
    T = seqs.n_tokens
    nt = T // TM
    ntp = seqs.n_prompt // TM
    c_pad = jnp.zeros((8, D_MODEL), F32).at[:c_all.shape[0]].set(c_all)
    mod3 = _ada(c_pad, w_ada, b_ada).reshape(8, 1, 6 * D_MODEL)
    cos_t, sin_t = _rope_tables(max(seqs.prompt_len, seqs.sample_len))
    w_kv, b_kv, p = _prep_params(w_in, b_in, w_spatial, b_spatial, sgu_ln_g, sgu_ln_b, w_br_attn,
                                 w_br_sgu, w_out, ln1_g, ln1_b, w_router, b_router)
    kt, v = _kv(seqs, xp, xs, mod3, cos_t, sin_t, w_kv, b_kv)
    x1, h2, logits = _mixer(seqs, sink, xp, xs, mod3, cos_t, sin_t, kt, v, p)
    rcol, rrow, cnt3 = _router(logits)

    cnt = cnt3[:, 0, :N_EXPERTS]
    count = cnt.sum(0)
    reg = (count + REGION_SLACK + BM - 1) // BM * BM
    pad_end = jnp.cumsum(reg)
    pad_start = pad_end - reg
    flat = lambda a: a.reshape(-1).astype(I32)
    s_te = flat(pad_start[None, :] + jnp.cumsum(cnt, axis=0) - cnt)
    d_nch = jnp.maximum((cnt + DISP_CHUNK - 1) // DISP_CHUNK, 1)
    c_nch = jnp.maximum((cnt + COMB_CHUNK - 1) // COMB_CHUNK, 1)
    db_te = flat(jnp.cumsum(cnt, axis=1) - cnt)
    cb_te = flat(COMB_CHUNK * (jnp.cumsum(c_nch, axis=1) - c_nch))
    rows_t = flat(COMB_CHUNK * c_nch.sum(1))
    n_blk = (nt * TILE_ROWS + N_EXPERTS * (REGION_SLACK + BM - 1)) // BM + 1
    tails = jnp.concatenate([pad_start + count, pad_end[:-1], jnp.array([n_blk * BM])]).astype(I32)
    blk_start = jnp.arange(n_blk, dtype=I32) * BM
    blk_e = jnp.minimum((blk_start[:, None] >= pad_end[None, :]).sum(1), N_EXPERTS - 1).astype(I32)
    owner = blk_e[:, None] == jnp.arange(N_EXPERTS)[None, :]
    blk_rows = (owner * (pad_start + count)[None, :]).sum(1) - blk_start
    blk_v = jnp.where(blk_start < pad_end[-1], jnp.clip(blk_rows, 0, BM), 0).astype(I32)

    xin = _dispatch(seqs, db_te, s_te, flat(d_nch - 1), flat((d_nch - 1).sum(1)), tails, h2, rrow, n_blk * BM)
    eo = _experts(blk_e, blk_v, xin, w_up, b_up, w_down, b_down, n_blk)
    comb = (s_te, cb_te, flat(c_nch - 1), flat((c_nch - 1).sum(1)), rows_t,
            x1, rcol, mod3, ln2_g.reshape(1, -1), ln2_b.reshape(1, -1), eo)
    return _combine(seqs, 0, ntp, *comb), _combine(seqs, ntp, nt - ntp, *comb)


def kernel(x_prompt, x_sample, c_prompt, c_sample, w_ada, b_ada, w_in, b_in, sink, sgu_ln_g, sgu_ln_b, w_spatial, b_spatial, w_br_attn, w_br_sgu, w_out, ln1_g, ln1_b, w_router, b_router, w_up, b_up, w_down, b_down, ln2_g, ln2_b):
    assert w_ada.shape[0] == DEPTH == 1
    bp, sp, d = x_prompt.shape
    bs, ss, _ = x_sample.shape
    seqs = _Seqs(n_prompt=bp * sp, prompt_len=sp, sample_len=ss, n_tokens=bp * sp + bs * ss)
    c_all = jnp.concatenate([c_prompt, c_sample], axis=0)
    yp, ys = _layer(seqs, x_prompt.reshape(bp * sp, d), x_sample.reshape(bs * ss, d), c_all,
                    w_ada[0], b_ada[0], w_in[0], b_in[0], sink[0], sgu_ln_g[0], sgu_ln_b[0],
                    w_spatial[0], b_spatial[0], w_br_attn[0], w_br_sgu[0], w_out[0], ln1_g[0], ln1_b[0],
                    w_router[0], b_router[0], w_up[0], b_up[0], w_down[0], b_down[0], ln2_g[0], ln2_b[0])
    return (yp.reshape(bp, sp, d), ys.reshape(bs, ss, d))
```

```python
import functools
import math
from typing import NamedTuple

import jax
import jax.numpy as jnp
from jax import lax
from jax.experimental import pallas as pl
from jax.experimental.pallas import tpu as pltpu

F32 = jnp.float32
BF16 = jnp.bfloat16
I32 = jnp.int32

D_MODEL = 1024
N_HEADS = 8
N_KV_HEADS = 2
HEAD_DIM = 64
ATT_W = N_HEADS * HEAD_DIM
KV_W = N_KV_HEADS * HEAD_DIM
BLOCK = 128
ROPE_THETA = 500000.0
ROT_DIM = HEAD_DIM // 4
ROT_HALF = ROT_DIM // 2
SGU_W = D_MODEL // 2
SGU_GROUPS = 4
N_EXPERTS = 32
TOP_K = 4
D_FF = D_MODEL
SWIGLU_LIMIT = 7.0
SWIGLU_ALPHA = 1.702
LN_EPS = 1e-5
DEPTH = 1
DN_ALPHA = (2 * DEPTH) ** 0.25

LANES = 128
SUBLANES = 8
ROW_TILES = D_MODEL // LANES
VMEM_LIMIT_BYTES = 56 * 1024 * 1024
U32 = jnp.uint32
PAIR = 2

TM = 256
MT = 512
TK = 512
BM = 512
ROUTER_TOKENS = 2048
DISP_CHUNK = 64
COMB_CHUNK = 48
REGION_SLACK = DISP_CHUNK
TILE_ROWS = TM * TOP_K + N_EXPERTS
DISP_ROWS = TILE_ROWS + DISP_CHUNK
COMB_ROWS = (TILE_ROWS + N_EXPERTS * COMB_CHUNK + 255) // 256 * 256
NEG_INF = float("-inf")


class _Seqs(NamedTuple):
    n_prompt: int
    prompt_len: int
    sample_len: int
    n_tokens: int


def _tile_pos(seqs, t0):
    is_s = t0 >= seqs.n_prompt
    seq_len = jnp.where(is_s, seqs.sample_len, seqs.prompt_len)
    off = jnp.where(is_s, t0 - seqs.n_prompt, t0)
    pos0 = off % seq_len
    row = jnp.where(is_s, seqs.n_prompt // seqs.prompt_len + off // seq_len, off // seq_len)
    return seq_len, pos0, row


def _ada_kernel(c_ref, w_ref, b_ref, o_ref):
    c = c_ref[...]
    a = c * jax.nn.sigmoid(c)
    o_ref[...] = jnp.dot(a, w_ref[...], preferred_element_type=F32,
                         precision=lax.Precision.HIGHEST) + b_ref[...]


def _ada(c_pad, w_ada, b_ada):
    n = w_ada.shape[1]
    bn = 1536
    return pl.pallas_call(
        _ada_kernel,
        grid=(n // bn,),
        in_specs=[pl.BlockSpec((8, D_MODEL), lambda j: (0, 0)),
                  pl.BlockSpec((D_MODEL, bn), lambda j: (0, j)),
                  pl.BlockSpec((1, bn), lambda j: (0, j))],
        out_specs=pl.BlockSpec((8, bn), lambda j: (0, j)),
        out_shape=jax.ShapeDtypeStruct((8, n), F32),
        compiler_params=pltpu.CompilerParams(vmem_limit_bytes=VMEM_LIMIT_BYTES),
        name="ada",
    )(c_pad, w_ada, b_ada.reshape(1, n))


def _rope_tables(length):
    inv = ROPE_THETA ** (-jnp.arange(ROT_HALF, dtype=F32) * 2.0 / ROT_DIM)
    ang = jnp.arange(length, dtype=F32)[:, None] * inv[None, :]
    lane = jnp.arange(LANES) % HEAD_DIM
    spread = ((lane[None, :] % ROT_HALF == jnp.arange(ROT_HALF)[:, None]) & (lane[None, :] < ROT_DIM)).astype(F32)
    sign = jnp.where(lane < ROT_HALF, -1.0, 1.0)
    exact = dict(precision=lax.Precision.HIGHEST, preferred_element_type=F32)
    cos = jnp.dot(jnp.cos(ang), spread, **exact) + (lane >= ROT_DIM).astype(F32)[None, :]
    sin = jnp.dot(jnp.sin(ang), spread * sign[None, :], **exact)
    return cos, sin


def _rope(x, cos, sin):
    n = x.shape[1]
    reps = n // LANES
    c = jnp.concatenate([cos] * reps, axis=1)
    s = jnp.concatenate([sin] * reps, axis=1)
    lane = lax.broadcasted_iota(I32, x.shape, 1)
    first = (lane & (HEAD_DIM - 1)) < ROT_HALF
    partner = jnp.where(first, pltpu.roll(x, n - ROT_HALF, 1), pltpu.roll(x, ROT_HALF, 1))
    return x * c + partner * s


def _kv_kernel(seqs, xp_ref, xs_ref, mod_ref, cos_ref, sin_ref, w_ref, b_ref, kt_ref, v_ref):
    mod = mod_ref[0]
    sh1 = mod[:, 0:D_MODEL]
    sc1 = mod[:, D_MODEL:2 * D_MODEL]
    x = jnp.where(pl.program_id(0) * TK >= seqs.n_prompt, xs_ref[...], xp_ref[...])
    h = (x * (1.0 + sc1) + sh1).astype(BF16)
    kv = jnp.dot(h, w_ref[...], preferred_element_type=F32) + b_ref[...]
    k = _rope(kv[:, 0:2 * LANES], cos_ref[...], sin_ref[...])
    kt_ref[...] = k.T.astype(BF16)
    v_ref[...] = kv[:, 2 * LANES:4 * LANES].astype(BF16)


def _group_x_specs(seqs, tile):
    ntp = seqs.n_prompt // tile
    return [pl.BlockSpec((tile, D_MODEL), lambda i, *_: (jnp.minimum(i, ntp - 1), 0)),
            pl.BlockSpec((tile, D_MODEL), lambda i, *_: (jnp.maximum(i - ntp, 0), 0))]


def _kv(seqs, xp, xs, mod3, cos_t, sin_t, w_kv, b_kv):
    T = seqs.n_tokens

    def mod_map(i):
        return (_tile_pos(seqs, i * TK)[2], 0, 0)

    def rope_map(i):
        return (_tile_pos(seqs, i * TK)[1] // TK, 0)

    return pl.pallas_call(
        functools.partial(_kv_kernel, seqs),
        grid=(T // TK,),
        in_specs=_group_x_specs(seqs, TK) + [
                  pl.BlockSpec((1, 1, 6 * D_MODEL), mod_map),
                  pl.BlockSpec((TK, LANES), rope_map),
                  pl.BlockSpec((TK, LANES), rope_map),
                  pl.BlockSpec((D_MODEL, 4 * LANES), lambda i: (0, 0)),
                  pl.BlockSpec((1, 4 * LANES), lambda i: (0, 0))],
        out_specs=[pl.BlockSpec((2 * LANES, TK), lambda i: (0, i)),
                   pl.BlockSpec((TK, 2 * LANES), lambda i: (i, 0))],
        out_shape=[jax.ShapeDtypeStruct((2 * LANES, T), BF16),
                   jax.ShapeDtypeStruct((T, 2 * LANES), BF16)],
        compiler_params=pltpu.CompilerParams(dimension_semantics=("arbitrary",),
                                             vmem_limit_bytes=VMEM_LIMIT_BYTES),
        name="kv",
    )(xp, xs, mod3, cos_t, sin_t, w_kv, b_kv)


def _layer_norm(x, g, b):
    mu = jnp.mean(x, axis=-1, keepdims=True)
    xc = x - mu
    var = jnp.mean(xc * xc, axis=-1, keepdims=True)
    return xc * lax.rsqrt(var + LN_EPS) * g + b


def _attention(q, kfull, vfull, valids, sink_ref):
    lane = lax.broadcasted_iota(I32, (BLOCK, LANES), 1)
    lo = lane < HEAD_DIM
    ones = jnp.ones((3 * BLOCK, LANES), BF16)
    units = [(jb, hk) for jb in range(MT // BLOCK) for hk in range(N_KV_HEADS)]
    scores, sinks = [], []
    for jb, hk in units:
        parts = []
        for p in range(2):
            qp = q[jb * BLOCK:(jb + 1) * BLOCK, (2 * hk + p) * LANES:(2 * hk + p + 1) * LANES]
            parts.append(jnp.where(lo, qp, 0.0).astype(BF16))
            parts.append(jnp.where(lo, 0.0, qp).astype(BF16))
        kwin = kfull[hk * LANES:(hk + 1) * LANES, jb * BLOCK:(jb + 3) * BLOCK]
        s = jnp.dot(jnp.concatenate(parts, axis=0), kwin, preferred_element_type=F32)
        scores.append(jnp.where(jnp.concatenate([valids[jb]] * 4, axis=0), s, NEG_INF))
        sinks.extend(jnp.full((BLOCK, 1), sink_ref[hk * 4 + g], F32) for g in range(4))
    s = jnp.concatenate(scores, axis=0)
    sk = jnp.concatenate(sinks, axis=0)
    m = jnp.maximum(jnp.max(s, axis=-1, keepdims=True), sk)
    p = jnp.exp(s - m).astype(BF16)
    sink_term = jnp.exp(sk - m)
    rows = []
    for ui, (jb, hk) in enumerate(units):
        vwin = jnp.concatenate([vfull[jb * BLOCK:(jb + 3) * BLOCK, hk * LANES:(hk + 1) * LANES], ones], axis=1)
        r0 = ui * 4 * BLOCK
        ov = jnp.dot(p[r0:r0 + 4 * BLOCK], vwin, preferred_element_type=F32)
        o = ov[:, 0:LANES] / (ov[:, LANES:2 * LANES] + sink_term[r0:r0 + 4 * BLOCK])
        pair = [jnp.where(lo, o[(2 * p2) * BLOCK:(2 * p2 + 1) * BLOCK],
                          o[(2 * p2 + 1) * BLOCK:(2 * p2 + 2) * BLOCK]) for p2 in range(2)]
        rows.append(jnp.concatenate(pair, axis=1))
    n_h = N_KV_HEADS
    return jnp.concatenate(
        [jnp.concatenate(rows[jb * n_h:(jb + 1) * n_h], axis=1) for jb in range(MT // BLOCK)], axis=0)


def _mixer_kernel(seqs, sink_ref, xp_ref, xs_ref, mod_ref, cos_ref, sin_ref,
                  ktp_ref, ktc_ref, ktn_ref, vp_ref, vc_ref, vn_ref,
                  wmix_ref, bmix_ref, wsp_ref, bspt_ref, sg_ref, sb_ref,
                  wba_ref, wbs_ref, wout_ref, l1g_ref, l1b_ref, wr_ref, br_ref,
                  x1_ref, h2_ref, logit_ref):
    i = pl.program_id(0)
    seq_len, pos0, _ = _tile_pos(seqs, i * MT)
    mod = mod_ref[0]
    sh1, sc1, g1 = (mod[:, j * D_MODEL:(j + 1) * D_MODEL] for j in range(3))
    sh2, sc2 = (mod[:, j * D_MODEL:(j + 1) * D_MODEL] for j in range(3, 5))
    x = jnp.where(i * MT >= seqs.n_prompt, xs_ref[...], xp_ref[...])
    h = (x * (1.0 + sc1) + sh1).astype(BF16)
    kv_end = ATT_W + 2 * KV_W
    zq = jnp.dot(h, wmix_ref[:, 0:ATT_W], preferred_element_type=F32) + bmix_ref[:, 0:ATT_W]
    z = jnp.dot(h, wmix_ref[:, kv_end:], preferred_element_type=F32) + bmix_ref[:, kv_end:]
    q = _rope(zq, cos_ref[...], sin_ref[...]) * (HEAD_DIM ** -0.5)
    u = jax.nn.gelu(z[:, 0:SGU_W])
    vs = _layer_norm(jax.nn.gelu(z[:, SGU_W:2 * SGU_W]), sg_ref[...], sb_ref[...])
    ga = z[:, 2 * SGU_W:2 * SGU_W + D_MODEL]
    gs = z[:, 2 * SGU_W + D_MODEL:]

    kfull = jnp.concatenate([ktp_ref[...], ktc_ref[...], ktn_ref[...]], axis=1)
    vfull = jnp.concatenate([vp_ref[...], vc_ref[...], vn_ref[...]], axis=0)
    qi = lax.broadcasted_iota(I32, (BLOCK, 3 * BLOCK), 0)
    ki = lax.broadcasted_iota(I32, (BLOCK, 3 * BLOCK), 1)
    band = (ki >= qi) & (ki <= qi + 2 * BLOCK)
    vs_b = vs.astype(BF16)
    valids, sgu_rows = [], []
    for jb in range(MT // BLOCK):
        posb = pos0 + jb * BLOCK
        valids.append(band & (ki >= jnp.where(posb == 0, BLOCK, 0))
                      & (ki < jnp.where(posb + BLOCK == seq_len, 2 * BLOCK, 3 * BLOCK)))
        groups = []
        for g in range(SGU_GROUPS):
            vg = vs_b[jb * BLOCK:(jb + 1) * BLOCK, g * LANES:(g + 1) * LANES]
            sv = jnp.dot(wsp_ref[g], vg, preferred_element_type=F32) + bspt_ref[:, g:g + 1]
            groups.append(sv)
        sgu_rows.append(jnp.concatenate(groups, axis=1))
    attn = _attention(q, kfull, vfull, valids, sink_ref)
    sgu = u * jnp.concatenate(sgu_rows, axis=0)

    a1 = jnp.dot(attn.astype(BF16), wba_ref[...], preferred_element_type=F32)
    a2 = jnp.dot(sgu.astype(BF16), wbs_ref[...], preferred_element_type=F32)
    merged = jax.nn.sigmoid(ga) * a1 + jax.nn.sigmoid(gs) * a2
    mix = jnp.dot(merged.astype(BF16), wout_ref[...], preferred_element_type=F32)
    x1 = _layer_norm(DN_ALPHA * x + g1 * mix, l1g_ref[...], l1b_ref[...])
    x1_ref[...] = x1
    h2 = x1 * (1.0 + sc2) + sh2
    hi = h2.astype(BF16)
    h2_ref[...] = hi

    lo_part = (h2 - hi.astype(F32)).astype(BF16)
    l1 = jnp.dot(hi, wr_ref[...], preferred_element_type=F32)
    l2 = jnp.dot(lo_part, wr_ref[:, 0:N_EXPERTS], preferred_element_type=F32)
    logit_ref[...] = l1[:, 0:N_EXPERTS] + l1[:, N_EXPERTS:2 * N_EXPERTS] + l2 + br_ref[...]


PACK = 4096.0
assert DISP_ROWS <= PACK and COMB_ROWS <= PACK and DISP_ROWS * PACK < 2 ** 24


def _router_kernel(l_ref, rcol_ref, rrow_ref, cnt_ref):
    rt = l_ref.shape[0]
    ns = rt // TM
    padded = jnp.concatenate([l_ref[...], jnp.zeros((rt, LANES - N_EXPERTS), F32)], axis=1)
    work = padded.T[0:N_EXPERTS]
    eidx = lax.broadcasted_iota(I32, (N_EXPERTS, rt), 0)
    idxs, vals = [], []
    for _ in range(TOP_K):
        m = jnp.max(work, axis=0, keepdims=True)
        ix = jnp.min(jnp.where(work == m, eidx, N_EXPERTS), axis=0, keepdims=True)
        idxs.append(ix)
        vals.append(m)
        work = jnp.where(eidx == ix, NEG_INF, work)
    exps = [jnp.exp(v - vals[0]) for v in vals]
    esum = exps[0] + exps[1] + exps[2] + exps[3]
    wts = [e / esum for e in exps]

    sel = jnp.zeros((N_EXPERTS, rt), F32)
    for ix in idxs:
        sel = sel + jnp.where(eidx == ix, 1.0, 0.0)
    sel_b = sel.astype(BF16)
    ti = lax.broadcasted_iota(I32, (TM, TM), 0)
    tj = lax.broadcasted_iota(I32, (TM, TM), 1)
    earlier = jnp.where(ti < tj, 1.0, 0.0).astype(BF16)
    tiles = [slice(s * TM, (s + 1) * TM) for s in range(ns)]
    ranks = [jnp.dot(sel_b[:, t], earlier, preferred_element_type=F32) for t in tiles]
    cnts, nchs = [], []
    for t in tiles:
        cnt = jnp.sum(sel[:, t], axis=1, keepdims=True)
        cnt = cnt + (cnt - 2.0 * jnp.floor(cnt * 0.5))
        cnts.append(cnt)
        nchs.append(jnp.maximum(jnp.floor((cnt + (COMB_CHUNK - 0.5)) * (1.0 / COMB_CHUNK)), 1.0))
    lane = lax.broadcasted_iota(I32, (N_EXPERTS, LANES), 1)
    pre = jnp.zeros((N_EXPERTS, LANES), F32)
    for j, col in enumerate(cnts + nchs):
        pre = jnp.where(lane == j, col, pre)
    ei = lax.broadcasted_iota(I32, (N_EXPERTS, N_EXPERTS), 0)
    ej = lax.broadcasted_iota(I32, (N_EXPERTS, N_EXPERTS), 1)
    before = jnp.where(ej < ei, 1.0, 0.0).astype(BF16)
    base = jnp.dot(before, pre.astype(BF16), preferred_element_type=F32)
    both = jnp.concatenate(
        [(ranks[s] + base[:, s:s + 1]) * PACK + (ranks[s] + base[:, ns + s:ns + s + 1] * COMB_CHUNK)
         for s in range(ns)], axis=1)

    packed = [jnp.sum(jnp.where(eidx == ix, both, 0.0), axis=0, keepdims=True) for ix in idxs]
    drow = [jnp.floor(v * (1.0 / PACK)) for v in packed]
    table = [ix.astype(F32) for ix in idxs] + drow + [v - d * PACK for v, d in zip(packed, drow)] + wts
    sub = lax.broadcasted_iota(I32, (LANES, rt), 0)
    rr = jnp.zeros((LANES, rt), F32)
    for j, row in enumerate(table):
        rr = jnp.where(sub == j, row, rr)
    rcol_ref[...] = rr.T
    for s in range(ns):
        rrow_ref[s] = rr[0:16, tiles[s]]
        cnt_ref[s] = jnp.broadcast_to(cnts[s], (N_EXPERTS, LANES)).astype(I32)


def _router(logits):
    T = logits.shape[0]
    rt = math.gcd(T, ROUTER_TOKENS)
    nt = T // TM
    assert 2 * (rt // TM) <= LANES
    return pl.pallas_call(
        _router_kernel,
        grid=(T // rt,),
        in_specs=[pl.BlockSpec((rt, N_EXPERTS), lambda i: (i, 0))],
        out_specs=[pl.BlockSpec((rt, LANES), lambda i: (i, 0)),
                   pl.BlockSpec((rt // TM, 16, TM), lambda i: (i, 0, 0)),
                   pl.BlockSpec((rt // TM, N_EXPERTS, LANES), lambda i: (i, 0, 0))],
        out_shape=[jax.ShapeDtypeStruct((T, LANES), F32),
                   jax.ShapeDtypeStruct((nt, 16, TM), F32),
                   jax.ShapeDtypeStruct((nt, N_EXPERTS, LANES), I32)],
        compiler_params=pltpu.CompilerParams(dimension_semantics=("arbitrary",),
                                             vmem_limit_bytes=VMEM_LIMIT_BYTES),
        name="router",
    )(logits)


def _mixer(seqs, sink, xp, xs, mod3, cos_t, sin_t, kt, v, p):
    T = seqs.n_tokens
    nt = T // MT
    nb = T // BLOCK
    r = MT // BLOCK

    def mod_map(i, s):
        return (_tile_pos(seqs, i * MT)[2], 0, 0)

    def rope_map(i, s):
        return (_tile_pos(seqs, i * MT)[1] // MT, 0)

    const2 = lambda i, s: (0, 0)
    once = dict(pipeline_mode=pl.Buffered(1))
    in_specs = _group_x_specs(seqs, MT) + [
        pl.BlockSpec((1, 1, 6 * D_MODEL), mod_map),
        pl.BlockSpec((MT, LANES), rope_map),
        pl.BlockSpec((MT, LANES), rope_map),
        pl.BlockSpec((2 * LANES, BLOCK), lambda i, s: (0, jnp.maximum(i * r - 1, 0))),
        pl.BlockSpec((2 * LANES, MT), lambda i, s: (0, i)),
        pl.BlockSpec((2 * LANES, BLOCK), lambda i, s: (0, jnp.minimum(i * r + r, nb - 1))),
        pl.BlockSpec((BLOCK, 2 * LANES), lambda i, s: (jnp.maximum(i * r - 1, 0), 0)),
        pl.BlockSpec((MT, 2 * LANES), lambda i, s: (i, 0)),
        pl.BlockSpec((BLOCK, 2 * LANES), lambda i, s: (jnp.minimum(i * r + r, nb - 1), 0)),
        pl.BlockSpec(p["w_mix"].shape, const2, **once),
        pl.BlockSpec(p["b_mix"].shape, const2, **once),
        pl.BlockSpec(p["w_sp"].shape, lambda i, s: (0, 0, 0), **once),
        pl.BlockSpec(p["b_spt"].shape, const2, **once),
        pl.BlockSpec(p["sgu_g"].shape, const2, **once),
        pl.BlockSpec(p["sgu_b"].shape, const2, **once),
        pl.BlockSpec(p["w_ba"].shape, const2, **once),
        pl.BlockSpec(p["w_bs"].shape, const2, **once),
        pl.BlockSpec(p["w_out"].shape, const2, **once),
        pl.BlockSpec(p["ln1_g"].shape, const2, **once),
        pl.BlockSpec(p["ln1_b"].shape, const2, **once),
        pl.BlockSpec(p["w_r"].shape, const2, **once),
        pl.BlockSpec(p["b_r"].shape, const2, **once),
    ]
    out_specs = [
        pl.BlockSpec((MT, D_MODEL), lambda i, s: (i, 0)),
        pl.BlockSpec((MT, D_MODEL), lambda i, s: (i, 0)),
        pl.BlockSpec((MT, N_EXPERTS), lambda i, s: (i, 0)),
    ]
    out_shape = [
        jax.ShapeDtypeStruct((T, D_MODEL), F32),
        jax.ShapeDtypeStruct((T, D_MODEL), BF16),
        jax.ShapeDtypeStruct((T, N_EXPERTS), F32),
    ]
    return pl.pallas_call(
        functools.partial(_mixer_kernel, seqs),
        grid_spec=pltpu.PrefetchScalarGridSpec(
            num_scalar_prefetch=1, grid=(nt,), in_specs=in_specs, out_specs=out_specs),
        out_shape=out_shape,
        compiler_params=pltpu.CompilerParams(dimension_semantics=("arbitrary",),
                                             vmem_limit_bytes=VMEM_LIMIT_BYTES),
        name="mixer",
    )(sink, xp, xs, mod3, cos_t, sin_t, kt, kt, kt, v, v, v,
      p["w_mix"], p["b_mix"], p["w_sp"], p["b_spt"], p["sgu_g"], p["sgu_b"],
      p["w_ba"], p["w_bs"], p["w_out"], p["ln1_g"], p["ln1_b"], p["w_r"], p["b_r"])


def _pair_rows(row):
    return pl.multiple_of((row // PAIR) * ROW_TILES, ROW_TILES)


def _to_row_tiles(dst_ref, rows, n):
    words = pltpu.bitcast(rows, U32)
    for c in range(ROW_TILES):
        dst_ref[pl.ds(c, n // PAIR, stride=ROW_TILES), :] = words[:, c * LANES:(c + 1) * LANES]


def _from_row_tiles(src_ref, start, n):
    base = start // PAIR * ROW_TILES
    words = jnp.concatenate(
        [src_ref[pl.ds(base + c, n // PAIR, stride=ROW_TILES), :] for c in range(ROW_TILES)], axis=1)
    return pltpu.bitcast(words, BF16)


WAIT_GROUP = 16


def _issue_tile(copy, k0, src_ref, dst_ref, extra_ref, n_extra, step):
    for e in range(N_EXPERTS):
        copy(src_ref[k0 + e], dst_ref[k0 + e]).start()

    @pl.when(n_extra > 0)
    def _():
        def per_expert(e, c):
            def per_chunk(j, c2):
                copy(src_ref[k0 + e] + j * step, dst_ref[k0 + e] + j * step).start()
                return c2
            lax.fori_loop(1, extra_ref[k0 + e] + 1, per_chunk, 0)
            return c
        lax.fori_loop(0, N_EXPERTS, per_expert, 0)


def _wait_tile(group_copy, chunk_copy, n_extra):
    for _ in range(N_EXPERTS // WAIT_GROUP):
        group_copy.wait()

    def body(_, c):
        chunk_copy.wait()
        return c
    lax.fori_loop(0, n_extra, body, 0)


def _dispatch_kernel(b_ref, s_ref, x_ref, nx_ref, tail_ref, h2_ref, rrow_ref, xin_ref, stg_ref, zero_ref, sem):
    i = pl.program_id(0)
    nt = pl.num_programs(0)
    slot = i % 2
    rr = rrow_ref[0]
    rho = lax.broadcasted_iota(I32, (DISP_ROWS, TM), 0)
    pt = jnp.zeros((DISP_ROWS, TM), F32)
    for k in range(TOP_K):
        pt = pt + jnp.where(rho == rr[TOP_K + k:TOP_K + k + 1].astype(I32), 1.0, 0.0)
    rows = jnp.dot(pt.astype(BF16), h2_ref[...], preferred_element_type=F32).astype(BF16)

    chunk = DISP_CHUNK // PAIR * ROW_TILES

    def copy(src_row, dst_row, sl):
        return pltpu.make_async_copy(stg_ref.at[sl, pl.ds(_pair_rows(src_row), chunk)],
                                     xin_ref.at[pl.ds(_pair_rows(dst_row), chunk)], sem)

    def wait_tile(tile):
        group = pltpu.make_async_copy(stg_ref.at[0, pl.ds(0, WAIT_GROUP * chunk)],
                                      xin_ref.at[pl.ds(0, WAIT_GROUP * chunk)], sem)
        _wait_tile(group, copy(0, 0, 0), nx_ref[tile])

    for sl in range(2):
        @pl.when(slot == sl)
        def _():
            _to_row_tiles(stg_ref.at[sl], rows, DISP_ROWS)

            @pl.when(i > 0)
            def _():
                wait_tile(i - 1)
            _issue_tile(lambda src, dst: copy(src, dst, sl), i * N_EXPERTS, b_ref, s_ref, x_ref, nx_ref[i],
                        DISP_CHUNK)

    @pl.when(i == nt - 1)
    def _():
        wait_tile(i)
        zero_ref[...] = jnp.zeros_like(zero_ref)

        def zcopy(dst_row):
            return pltpu.make_async_copy(zero_ref, xin_ref.at[pl.ds(_pair_rows(dst_row), chunk)], sem)

        def zwait(count):
            def body(_, c):
                zcopy(0).wait()
                return c
            lax.fori_loop(0, count, body, 0)

        def per_expert(e, total):
            lo = tail_ref[e]
            nz = (tail_ref[N_EXPERTS + e] - lo) // DISP_CHUNK

            def per_chunk(j, c):
                zcopy(lo + j * DISP_CHUNK).start()
                return c
            lax.fori_loop(0, nz, per_chunk, 0)
            return total + nz
        zwait(lax.fori_loop(0, N_EXPERTS, per_expert, 0))

        def last_chunk(e, c):
            zcopy(tail_ref[N_EXPERTS + e] - DISP_CHUNK).start()
            return c
        lax.fori_loop(0, N_EXPERTS, last_chunk, 0)
        zwait(N_EXPERTS)


def _dispatch(seqs, b_te, s_te, x_te, nx_t, tails, h2, rrow, n_rows):
    nt = seqs.n_tokens // TM
    return pl.pallas_call(
        _dispatch_kernel,
        grid_spec=pltpu.PrefetchScalarGridSpec(
            num_scalar_prefetch=5, grid=(nt,),
            in_specs=[pl.BlockSpec((TM, D_MODEL), lambda i, *_: (i, 0)),
                      pl.BlockSpec((1, 16, TM), lambda i, *_: (i, 0, 0))],
            out_specs=pl.BlockSpec(memory_space=pl.ANY),
            scratch_shapes=[pltpu.VMEM((2, DISP_ROWS // PAIR * ROW_TILES, LANES), U32),
                            pltpu.VMEM((DISP_CHUNK // PAIR * ROW_TILES, LANES), U32),
                            pltpu.SemaphoreType.DMA]),
        out_shape=jax.ShapeDtypeStruct((n_rows // PAIR * ROW_TILES, LANES), U32),
        compiler_params=pltpu.CompilerParams(dimension_semantics=("arbitrary",),
                                             vmem_limit_bytes=VMEM_LIMIT_BYTES),
        name="dispatch",
    )(b_te, s_te, x_te, nx_t, tails, h2, rrow)


FF_PIECE = 256
FF_PIECES = D_FF // FF_PIECE


def _expert_kernel(be_ref, bv_ref, x_ref, wup_hbm, bup_ref, wdn_hbm, bdn_ref, o_ref,
                   wup_f, wdn_f, wup_b, wdn_b, sem):
    i = pl.program_id(0)
    e = be_ref[i]
    valid = bv_ref[i]
    first = jnp.logical_or(i == 0, e != be_ref[jnp.maximum(i - 1, 0)])

    def fetch(ex):
        slot = ex % 2
        return (pltpu.make_async_copy(wup_hbm.at[ex], wup_f.at[slot], sem.at[0, slot]),
                pltpu.make_async_copy(wdn_hbm.at[ex], wdn_f.at[slot], sem.at[1, slot]))

    @pl.when(i == 0)
    def _():
        for d in fetch(e):
            d.start()

    @pl.when(first)
    def _():
        for d in fetch(e):
            d.wait()

        @pl.when(e + 1 < N_EXPERTS)
        def _():
            for d in fetch(e + 1):
                d.start()

        @pl.when(valid > 0)
        def _():
            for c in range(FF_PIECES):
                g0, l0 = c * FF_PIECE, D_FF + c * FF_PIECE
                wup_b[:, 2 * g0:2 * g0 + FF_PIECE] = wup_f[e % 2, :, g0:g0 + FF_PIECE].astype(BF16)
                wup_b[:, 2 * g0 + FF_PIECE:2 * g0 + 2 * FF_PIECE] = wup_f[e % 2, :, l0:l0 + FF_PIECE].astype(BF16)
            wdn_b[...] = wdn_f[e % 2].astype(BF16)

    @pl.when(valid > 0)
    def _():
        x = _from_row_tiles(x_ref, 0, BM)
        row = lax.broadcasted_iota(I32, (BM, 1), 0)
        xb = jnp.where(row < valid, x, jnp.zeros_like(x))
        y = bdn_ref[0]
        up = lambda c: jnp.dot(xb, wup_b[:, 2 * c * FF_PIECE:2 * (c + 1) * FF_PIECE], preferred_element_type=F32)
        hu_next = up(0)
        for c in range(FF_PIECES):
            g0, l0 = c * FF_PIECE, D_FF + c * FF_PIECE
            hu = hu_next
            if c + 1 < FF_PIECES:
                hu_next = up(c + 1)
            gate = jnp.minimum(hu[:, 0:FF_PIECE] + bup_ref[0, :, g0:g0 + FF_PIECE], SWIGLU_LIMIT)
            lin = jnp.clip(hu[:, FF_PIECE:] + bup_ref[0, :, l0:l0 + FF_PIECE], -SWIGLU_LIMIT, SWIGLU_LIMIT)
            act = gate * jax.nn.sigmoid(SWIGLU_ALPHA * gate) * (lin + 1.0)
            y = y + jnp.dot(act.astype(BF16), wdn_b[g0:g0 + FF_PIECE, :], preferred_element_type=F32)
        _to_row_tiles(o_ref, y.astype(BF16), BM)

    @pl.when(valid == 0)
    def _():
        o_ref[...] = jnp.zeros_like(o_ref)


def _experts(blk_e, blk_v, xin, w_up, b_up, w_down, b_down, n_blk):
    return pl.pallas_call(
        _expert_kernel,
        grid_spec=pltpu.PrefetchScalarGridSpec(
            num_scalar_prefetch=2, grid=(n_blk,),
            in_specs=[pl.BlockSpec((BM // PAIR * ROW_TILES, LANES), lambda i, be, bv: (i, 0)),
                      pl.BlockSpec(memory_space=pl.ANY),
                      pl.BlockSpec((1, 1, 2 * D_FF), lambda i, be, bv: (be[i], 0, 0)),
                      pl.BlockSpec(memory_space=pl.ANY),
                      pl.BlockSpec((1, 1, D_MODEL), lambda i, be, bv: (be[i], 0, 0))],
            out_specs=pl.BlockSpec((BM // PAIR * ROW_TILES, LANES), lambda i, be, bv: (i, 0)),
            scratch_shapes=[pltpu.VMEM((2, D_MODEL, 2 * D_FF), F32),
                            pltpu.VMEM((2, D_FF, D_MODEL), F32),
                            pltpu.VMEM((D_MODEL, 2 * D_FF), BF16),
                            pltpu.VMEM((D_FF, D_MODEL), BF16),
                            pltpu.SemaphoreType.DMA((2, 2))]),
        out_shape=jax.ShapeDtypeStruct((n_blk * BM // PAIR * ROW_TILES, LANES), U32),
        compiler_params=pltpu.CompilerParams(dimension_semantics=("arbitrary",),
                                             vmem_limit_bytes=VMEM_LIMIT_BYTES),
        name="experts",
    )(blk_e, blk_v, xin, w_up, b_up.reshape(N_EXPERTS, 1, 2 * D_FF), w_down,
      b_down.reshape(N_EXPERTS, 1, D_MODEL))


COMB_KC = COMB_ROWS // 256
COMB_KC_MIN = N_EXPERTS * COMB_CHUNK // 256


def _combine_kernel(tile0, s_ref, b_ref, x_ref, nx_ref, rows_ref, x1_ref, rcol_ref, mod_ref, g_ref, bb_ref,
                    eo_ref, y_ref, stg_ref, acc_ref, sem):
    i = pl.program_id(0)
    nt = pl.num_programs(0)
    slot = i % 2
    chunk = COMB_CHUNK // PAIR * ROW_TILES

    def copy(src_row, dst_row, sl):
        return pltpu.make_async_copy(eo_ref.at[pl.ds(_pair_rows(src_row), chunk)],
                                     stg_ref.at[sl, pl.ds(_pair_rows(dst_row), chunk)], sem.at[sl])

    def issue(tile, sl):
        _issue_tile(lambda src, dst: copy(src, dst, sl), tile * N_EXPERTS, s_ref, b_ref, x_ref, nx_ref[tile],
                    COMB_CHUNK)

    @pl.when(i == 0)
    def _():
        stg_ref[...] = jnp.zeros_like(stg_ref)
        issue(tile0, 0)

    rc = rcol_ref[...]
    col_k = [rc[:, 2 * TOP_K + k:2 * TOP_K + k + 1].astype(I32) for k in range(TOP_K)]
    w_k = [rc[:, 3 * TOP_K + k:3 * TOP_K + k + 1] for k in range(TOP_K)]

    def chunk_dot(sl, c):
        jl = lax.broadcasted_iota(I32, (TM, 256), 1) + c * 256
        pm = jnp.zeros((TM, 256), F32)
        for k in range(TOP_K):
            pm = pm + jnp.where(jl == col_k[k], w_k[k], 0.0)
        rows = _from_row_tiles(stg_ref.at[sl], c * 256, 256)
        return jnp.dot(pm.astype(BF16), rows, preferred_element_type=F32)

    for sl in range(2):
        @pl.when(slot == sl)
        def _():
            @pl.when(i + 1 < nt)
            def _():
                issue(tile0 + i + 1, 1 - sl)

            group = pltpu.make_async_copy(eo_ref.at[pl.ds(0, WAIT_GROUP * chunk)],
                                          stg_ref.at[sl, pl.ds(0, WAIT_GROUP * chunk)], sem.at[sl])
            _wait_tile(group, copy(0, 0, sl), nx_ref[tile0 + i])

            y = chunk_dot(sl, 0)
            for c in range(1, COMB_KC_MIN):
                y = y + chunk_dot(sl, c)
            acc_ref[...] = y
            used = (rows_ref[tile0 + i] + 255) // 256
            for c in range(COMB_KC_MIN, COMB_KC):
                @pl.when(c < used)
                def _():
                    acc_ref[...] += chunk_dot(sl, c)

    mod = mod_ref[0]
    g2 = mod[:, 5 * D_MODEL:6 * D_MODEL]
    y_ref[...] = _layer_norm(DN_ALPHA * x1_ref[...] + g2 * acc_ref[...], g_ref[...], bb_ref[...])


def _combine(seqs, tile0, n_tiles, s_te, b_te, x_te, nx_t, rows_t, x1, rcol, mod3, ln2_g, ln2_b, eo):
    def mod_map(i, *_):
        return (_tile_pos(seqs, (i + tile0) * TM)[2], 0, 0)

    return pl.pallas_call(
        functools.partial(_combine_kernel, tile0),
        grid_spec=pltpu.PrefetchScalarGridSpec(
            num_scalar_prefetch=5, grid=(n_tiles,),
            in_specs=[pl.BlockSpec((TM, D_MODEL), lambda i, *_: (i + tile0, 0)),
                      pl.BlockSpec((TM, LANES), lambda i, *_: (i + tile0, 0)),
                      pl.BlockSpec((1, 1, 6 * D_MODEL), mod_map),
                      pl.BlockSpec((1, D_MODEL), lambda i, *_: (0, 0)),
                      pl.BlockSpec((1, D_MODEL), lambda i, *_: (0, 0)),
                      pl.BlockSpec(memory_space=pl.ANY)],
            out_specs=pl.BlockSpec((TM, D_MODEL), lambda i, *_: (i, 0)),
            scratch_shapes=[pltpu.VMEM((2, COMB_ROWS // PAIR * ROW_TILES, LANES), U32),
                            pltpu.VMEM((TM, D_MODEL), F32),
                            pltpu.SemaphoreType.DMA((2,))]),
        out_shape=jax.ShapeDtypeStruct((n_tiles * TM, D_MODEL), F32),
        compiler_params=pltpu.CompilerParams(dimension_semantics=("arbitrary",),
                                             vmem_limit_bytes=VMEM_LIMIT_BYTES),
        name="combine",
    )(s_te, b_te, x_te, nx_t, rows_t, x1, rcol, mod3, ln2_g, ln2_b, eo)


def _prep_params(w_in, b_in, w_spatial, b_spatial, sgu_ln_g, sgu_ln_b, w_br_attn, w_br_sgu, w_out,
                 ln1_g, ln1_b, w_router, b_router):
    q_end, k_end, v_end = ATT_W, ATT_W + KV_W, ATT_W + 2 * KV_W

    def dup(w, lo):
        h0, h1 = w[..., lo:lo + HEAD_DIM], w[..., lo + HEAD_DIM:lo + 2 * HEAD_DIM]
        return jnp.concatenate([h0, h0, h1, h1], axis=-1)

    w_kv = jnp.concatenate([dup(w_in, q_end), dup(w_in, k_end)], axis=1).astype(BF16)
    b_kv = jnp.concatenate([dup(b_in, q_end), dup(b_in, k_end)], axis=0).reshape(1, -1)
    w_mix = w_in.astype(BF16)
    b_mix = b_in.reshape(1, -1)
    w_hi = w_router.astype(BF16)
    w_lo = (w_router - w_hi.astype(F32)).astype(BF16)
    p = dict(
        w_mix=w_mix, b_mix=b_mix,
        w_sp=w_spatial.astype(BF16), b_spt=b_spatial.T,
        sgu_g=sgu_ln_g.reshape(1, -1), sgu_b=sgu_ln_b.reshape(1, -1),
        w_ba=w_br_attn.astype(BF16), w_bs=w_br_sgu.astype(BF16), w_out=w_out.astype(BF16),
        ln1_g=ln1_g.reshape(1, -1), ln1_b=ln1_b.reshape(1, -1),
        w_r=jnp.concatenate([w_hi, w_lo], axis=1), b_r=b_router.reshape(1, -1),
    )
    return w_kv, b_kv, p


def _layer(seqs, xp, xs, c_all, w_ada, b_ada, w_in, b_in, sink, sgu_ln_g, sgu_ln_b, w_spatial, b_spatial,
           w_br_attn, w_br_sgu, w_out, ln1_g, ln1_b, w_router, b_router, w_up, b_up, w_down, b_down,
           ln2_g, ln2_b):
    T = seqs.n_tokens
    nt = T // TM
    ntp = seqs.n_prompt // TM
    c_pad = jnp.zeros((8, D_MODEL), F32).at[:c_all.shape[0]].set(c_all)
    mod3 = _ada(c_pad, w_ada, b_ada).reshape(8, 1, 6 * D_MODEL)
    cos_t, sin_t = _rope_tables(max(seqs.prompt_len, seqs.sample_len))
    w_kv, b_kv, p = _prep_params(w_in, b_in, w_spatial, b_spatial, sgu_ln_g, sgu_ln_b, w_br_attn,
                                 w_br_sgu, w_out, ln1_g, ln1_b, w_router, b_router)
    kt, v = _kv(seqs, xp, xs, mod3, cos_t, sin_t, w_kv, b_kv)
    x1, h2, logits = _mixer(seqs, sink, xp, xs, mod3, cos_t, sin_t, kt, v, p)
    rcol, rrow, cnt3 = _router(logits)

    cnt = cnt3[:, :, 0]
    count = cnt.sum(0)
    reg = (count + REGION_SLACK + BM - 1) // BM * BM
    pad_end = jnp.cumsum(reg)
    pad_start = pad_end - reg
    flat = lambda a: a.reshape(-1).astype(I32)
    s_te = flat(pad_start[None, :] + jnp.cumsum(cnt, axis=0) - cnt)
    d_nch = jnp.maximum((cnt + DISP_CHUNK - 1) // DISP_CHUNK, 1)
    c_nch = jnp.maximum((cnt + COMB_CHUNK - 1) // COMB_CHUNK, 1)
    db_te = flat(jnp.cumsum(cnt, axis=1) - cnt)
    cb_te = flat(COMB_CHUNK * (jnp.cumsum(c_nch, axis=1) - c_nch))
    rows_t = flat(COMB_CHUNK * c_nch.sum(1))
    n_blk = (nt * TILE_ROWS + N_EXPERTS * (REGION_SLACK + BM - 1)) // BM + 1
    tails = jnp.concatenate([pad_start + count, pad_end[:-1], jnp.array([n_blk * BM])]).astype(I32)
    blk_start = jnp.arange(n_blk, dtype=I32) * BM
    blk_e = jnp.minimum((blk_start[:, None] >= pad_end[None, :]).sum(1), N_EXPERTS - 1).astype(I32)
    owner = blk_e[:, None] == jnp.arange(N_EXPERTS)[None, :]
    blk_rows = (owner * (pad_start + count)[None, :]).sum(1) - blk_start
    blk_v = jnp.where(blk_start < pad_end[-1], jnp.clip(blk_rows, 0, BM), 0).astype(I32)

    xin = _dispatch(seqs, db_te, s_te, flat(d_nch - 1), flat((d_nch - 1).sum(1)), tails, h2, rrow, n_blk * BM)
    eo = _experts(blk_e, blk_v, xin, w_up, b_up, w_down, b_down, n_blk)
    comb = (s_te, cb_te, flat(c_nch - 1), flat((c_nch - 1).sum(1)), rows_t,
            x1, rcol, mod3, ln2_g.reshape(1, -1), ln2_b.reshape(1, -1), eo)
    return _combine(seqs, 0, ntp, *comb), _combine(seqs, ntp, nt - ntp, *comb)


def kernel(x_prompt, x_sample, c_prompt, c_sample, w_ada, b_ada, w_in, b_in, sink, sgu_ln_g, sgu_ln_b, w_spatial, b_spatial, w_br_attn, w_br_sgu, w_out, ln1_g, ln1_b, w_router, b_router, w_up, b_up, w_down, b_down, ln2_g, ln2_b):
    assert w_ada.shape[0] == DEPTH == 1
    bp, sp, d = x_prompt.shape
    bs, ss, _ = x_sample.shape
    seqs = _Seqs(n_prompt=bp * sp, prompt_len=sp, sample_len=ss, n_tokens=bp * sp + bs * ss)
    c_all = jnp.concatenate([c_prompt, c_sample], axis=0)
    yp, ys = _layer(seqs, x_prompt.reshape(bp * sp, d), x_sample.reshape(bs * ss, d), c_all,
                    w_ada[0], b_ada[0], w_in[0], b_in[0], sink[0], sgu_ln_g[0], sgu_ln_b[0],
                    w_spatial[0], b_spatial[0], w_br_attn[0], w_br_sgu[0], w_out[0], ln1_g[0], ln1_b[0],
                    w_router[0], b_router[0], w_up[0], b_up[0], w_down[0], b_down[0], ln2_g[0], ln2_b[0])
    return (yp.reshape(bp, sp, d), ys.reshape(bs, ss, d))
```

```python
import functools
import math
from typing import NamedTuple

import jax
import jax.numpy as jnp
from jax import lax
from jax.experimental import pallas as pl
from jax.experimental.pallas import tpu as pltpu

F32 = jnp.float32
BF16 = jnp.bfloat16
I32 = jnp.int32

D_MODEL = 1024
N_HEADS = 8
N_KV_HEADS = 2
HEAD_DIM = 64
ATT_W = N_HEADS * HEAD_DIM
KV_W = N_KV_HEADS * HEAD_DIM
BLOCK = 128
ROPE_THETA = 500000.0
ROT_DIM = HEAD_DIM // 4
ROT_HALF = ROT_DIM // 2
SGU_W = D_MODEL // 2
SGU_GROUPS = 4
N_EXPERTS = 32
TOP_K = 4
D_FF = D_MODEL
SWIGLU_LIMIT = 7.0
SWIGLU_ALPHA = 1.702
LN_EPS = 1e-5
DEPTH = 1
DN_ALPHA = (2 * DEPTH) ** 0.25

LANES = 128
SUBLANES = 8
ROW_TILES = D_MODEL // LANES
VMEM_LIMIT_BYTES = 56 * 1024 * 1024
U32 = jnp.uint32
PAIR = 2

TM = 256
MT = 512
TK = 1024
BM = 512
ROUTER_TOKENS = 2048
DISP_CHUNK = 64
COMB_CHUNK = 48
REGION_SLACK = DISP_CHUNK
TILE_ROWS = TM * TOP_K + N_EXPERTS
DISP_ROWS = TILE_ROWS + DISP_CHUNK
COMB_ROWS = (TILE_ROWS + N_EXPERTS * COMB_CHUNK + 255) // 256 * 256
NEG_INF = float("-inf")


class _Seqs(NamedTuple):
    n_prompt: int
    prompt_len: int
    sample_len: int
    n_tokens: int


def _tile_pos(seqs, t0):
    is_s = t0 >= seqs.n_prompt
    seq_len = jnp.where(is_s, seqs.sample_len, seqs.prompt_len)
    off = jnp.where(is_s, t0 - seqs.n_prompt, t0)
    pos0 = off % seq_len
    row = jnp.where(is_s, seqs.n_prompt // seqs.prompt_len + off // seq_len, off // seq_len)
    return seq_len, pos0, row


def _ada_kernel(c_ref, w_ref, b_ref, o_ref):
    c = c_ref[...]
    a = c * jax.nn.sigmoid(c)
    o_ref[...] = jnp.dot(a, w_ref[...], preferred_element_type=F32,
                         precision=lax.Precision.HIGHEST) + b_ref[...]


def _ada(c_pad, w_ada, b_ada):
    n = w_ada.shape[1]
    bn = 1536
    return pl.pallas_call(
        _ada_kernel,
        grid=(n // bn,),
        in_specs=[pl.BlockSpec((8, D_MODEL), lambda j: (0, 0)),
                  pl.BlockSpec((D_MODEL, bn), lambda j: (0, j)),
                  pl.BlockSpec((1, bn), lambda j: (0, j))],
        out_specs=pl.BlockSpec((8, bn), lambda j: (0, j)),
        out_shape=jax.ShapeDtypeStruct((8, n), F32),
        compiler_params=pltpu.CompilerParams(vmem_limit_bytes=VMEM_LIMIT_BYTES),
        name="ada",
    )(c_pad, w_ada, b_ada.reshape(1, n))


def _rope_tables(length):
    inv = ROPE_THETA ** (-jnp.arange(ROT_HALF, dtype=F32) * 2.0 / ROT_DIM)
    ang = jnp.arange(length, dtype=F32)[:, None] * inv[None, :]
    lane = jnp.arange(LANES) % HEAD_DIM
    spread = ((lane[None, :] % ROT_HALF == jnp.arange(ROT_HALF)[:, None]) & (lane[None, :] < ROT_DIM)).astype(F32)
    sign = jnp.where(lane < ROT_HALF, -1.0, 1.0)
    exact = dict(precision=lax.Precision.HIGHEST, preferred_element_type=F32)
    cos = jnp.dot(jnp.cos(ang), spread, **exact) + (lane >= ROT_DIM).astype(F32)[None, :]
    sin = jnp.dot(jnp.sin(ang), spread * sign[None, :], **exact)
    return cos, sin


def _rope(x, cos, sin):
    n = x.shape[1]
    reps = n // LANES
    c = jnp.concatenate([cos] * reps, axis=1)
    s = jnp.concatenate([sin] * reps, axis=1)
    lane = lax.broadcasted_iota(I32, x.shape, 1)
    first = (lane & (HEAD_DIM - 1)) < ROT_HALF
    partner = jnp.where(first, pltpu.roll(x, n - ROT_HALF, 1), pltpu.roll(x, ROT_HALF, 1))
    return x * c + partner * s


def _kv_kernel(seqs, xp_ref, xs_ref, mod_ref, cos_ref, sin_ref, w_ref, b_ref, kt_ref, v_ref):
    mod = mod_ref[0]
    sh1 = mod[:, 0:D_MODEL]
    sc1 = mod[:, D_MODEL:2 * D_MODEL]
    x = jnp.where(pl.program_id(0) * TK >= seqs.n_prompt, xs_ref[...], xp_ref[...])
    h = (x * (1.0 + sc1) + sh1).astype(BF16)
    kv = jnp.dot(h, w_ref[...], preferred_element_type=F32) + b_ref[...]
    k = _rope(kv[:, 0:2 * LANES], cos_ref[...], sin_ref[...])
    kt_ref[...] = k.T.astype(BF16)
    v_ref[...] = kv[:, 2 * LANES:4 * LANES].astype(BF16)


def _group_x_specs(seqs, tile):
    ntp = seqs.n_prompt // tile
    return [pl.BlockSpec((tile, D_MODEL), lambda i, *_: (jnp.minimum(i, ntp - 1), 0)),
            pl.BlockSpec((tile, D_MODEL), lambda i, *_: (jnp.maximum(i - ntp, 0), 0))]


def _kv(seqs, xp, xs, mod3, cos_t, sin_t, w_kv, b_kv):
    T = seqs.n_tokens

    def mod_map(i):
        return (_tile_pos(seqs, i * TK)[2], 0, 0)

    def rope_map(i):
        return (_tile_pos(seqs, i * TK)[1] // TK, 0)

    return pl.pallas_call(
        functools.partial(_kv_kernel, seqs),
        grid=(T // TK,),
        in_specs=_group_x_specs(seqs, TK) + [
                  pl.BlockSpec((1, 1, 6 * D_MODEL), mod_map),
                  pl.BlockSpec((TK, LANES), rope_map),
                  pl.BlockSpec((TK, LANES), rope_map),
                  pl.BlockSpec((D_MODEL, 4 * LANES), lambda i: (0, 0)),
                  pl.BlockSpec((1, 4 * LANES), lambda i: (0, 0))],
        out_specs=[pl.BlockSpec((2 * LANES, TK), lambda i: (0, i)),
                   pl.BlockSpec((TK, 2 * LANES), lambda i: (i, 0))],
        out_shape=[jax.ShapeDtypeStruct((2 * LANES, T), BF16),
                   jax.ShapeDtypeStruct((T, 2 * LANES), BF16)],
        compiler_params=pltpu.CompilerParams(dimension_semantics=("arbitrary",),
                                             vmem_limit_bytes=VMEM_LIMIT_BYTES),
        name="kv",
    )(xp, xs, mod3, cos_t, sin_t, w_kv, b_kv)


def _layer_norm(x, g, b):
    mu = jnp.mean(x, axis=-1, keepdims=True)
    xc = x - mu
    var = jnp.mean(xc * xc, axis=-1, keepdims=True)
    return xc * lax.rsqrt(var + LN_EPS) * g + b


def _attention(q, kfull, vfull, valids, sink_ref):
    lane = lax.broadcasted_iota(I32, (BLOCK, LANES), 1)
    lo = lane < HEAD_DIM
    ones = jnp.ones((3 * BLOCK, LANES), BF16)
    units = [(jb, hk) for jb in range(MT // BLOCK) for hk in range(N_KV_HEADS)]
    scores, sinks = [], []
    for jb, hk in units:
        parts = []
        for p in range(2):
            qp = q[jb * BLOCK:(jb + 1) * BLOCK, (2 * hk + p) * LANES:(2 * hk + p + 1) * LANES]
            parts.append(jnp.where(lo, qp, 0.0).astype(BF16))
            parts.append(jnp.where(lo, 0.0, qp).astype(BF16))
        kwin = kfull[hk * LANES:(hk + 1) * LANES, jb * BLOCK:(jb + 3) * BLOCK]
        s = jnp.dot(jnp.concatenate(parts, axis=0), kwin, preferred_element_type=F32)
        scores.append(jnp.where(jnp.concatenate([valids[jb]] * 4, axis=0), s, NEG_INF))
        sinks.extend(jnp.full((BLOCK, 1), sink_ref[hk * 4 + g], F32) for g in range(4))
    s = jnp.concatenate(scores, axis=0)
    sk = jnp.concatenate(sinks, axis=0)
    m = jnp.maximum(jnp.max(s, axis=-1, keepdims=True), sk)
    p = jnp.exp(s - m).astype(BF16)
    sink_term = jnp.exp(sk - m)
    rows = []
    for ui, (jb, hk) in enumerate(units):
        vwin = jnp.concatenate([vfull[jb * BLOCK:(jb + 3) * BLOCK, hk * LANES:(hk + 1) * LANES], ones], axis=1)
        r0 = ui * 4 * BLOCK
        ov = jnp.dot(p[r0:r0 + 4 * BLOCK], vwin, preferred_element_type=F32)
        o = ov[:, 0:LANES] / (ov[:, LANES:2 * LANES] + sink_term[r0:r0 + 4 * BLOCK])
        pair = [jnp.where(lo, o[(2 * p2) * BLOCK:(2 * p2 + 1) * BLOCK],
                          o[(2 * p2 + 1) * BLOCK:(2 * p2 + 2) * BLOCK]) for p2 in range(2)]
        rows.append(jnp.concatenate(pair, axis=1))
    n_h = N_KV_HEADS
    return jnp.concatenate(
        [jnp.concatenate(rows[jb * n_h:(jb + 1) * n_h], axis=1) for jb in range(MT // BLOCK)], axis=0)


def _mixer_kernel(seqs, sink_ref, xp_ref, xs_ref, mod_ref, cos_ref, sin_ref,
                  ktp_ref, ktc_ref, ktn_ref, vp_ref, vc_ref, vn_ref,
                  wmix_ref, bmix_ref, wsp_ref, bspt_ref, sg_ref, sb_ref,
                  wba_ref, wbs_ref, wout_ref, l1g_ref, l1b_ref, wr_ref, br_ref,
                  x1_ref, h2_ref, logit_ref):
    i = pl.program_id(0)
    seq_len, pos0, _ = _tile_pos(seqs, i * MT)
    mod = mod_ref[0]
    sh1, sc1, g1 = (mod[:, j * D_MODEL:(j + 1) * D_MODEL] for j in range(3))
    sh2, sc2 = (mod[:, j * D_MODEL:(j + 1) * D_MODEL] for j in range(3, 5))
    x = jnp.where(i * MT >= seqs.n_prompt, xs_ref[...], xp_ref[...])
    h = (x * (1.0 + sc1) + sh1).astype(BF16)
    kv_end = ATT_W + 2 * KV_W
    zq = jnp.dot(h, wmix_ref[:, 0:ATT_W], preferred_element_type=F32) + bmix_ref[:, 0:ATT_W]
    z = jnp.dot(h, wmix_ref[:, kv_end:], preferred_element_type=F32) + bmix_ref[:, kv_end:]
    q = _rope(zq, cos_ref[...], sin_ref[...]) * (HEAD_DIM ** -0.5)
    u = jax.nn.gelu(z[:, 0:SGU_W])
    vs = _layer_norm(jax.nn.gelu(z[:, SGU_W:2 * SGU_W]), sg_ref[...], sb_ref[...])
    ga = z[:, 2 * SGU_W:2 * SGU_W + D_MODEL]
    gs = z[:, 2 * SGU_W + D_MODEL:]

    kfull = jnp.concatenate([ktp_ref[...], ktc_ref[...], ktn_ref[...]], axis=1)
    vfull = jnp.concatenate([vp_ref[...], vc_ref[...], vn_ref[...]], axis=0)
    qi = lax.broadcasted_iota(I32, (BLOCK, 3 * BLOCK), 0)
    ki = lax.broadcasted_iota(I32, (BLOCK, 3 * BLOCK), 1)
    band = (ki >= qi) & (ki <= qi + 2 * BLOCK)
    vs_b = vs.astype(BF16)
    valids, sgu_rows = [], []
    for jb in range(MT // BLOCK):
        posb = pos0 + jb * BLOCK
        valids.append(band & (ki >= jnp.where(posb == 0, BLOCK, 0))
                      & (ki < jnp.where(posb + BLOCK == seq_len, 2 * BLOCK, 3 * BLOCK)))
        groups = []
        for g in range(SGU_GROUPS):
            vg = vs_b[jb * BLOCK:(jb + 1) * BLOCK, g * LANES:(g + 1) * LANES]
            sv = jnp.dot(wsp_ref[g], vg, preferred_element_type=F32) + bspt_ref[:, g:g + 1]
            groups.append(sv)
        sgu_rows.append(jnp.concatenate(groups, axis=1))
    attn = _attention(q, kfull, vfull, valids, sink_ref)
    sgu = u * jnp.concatenate(sgu_rows, axis=0)

    a1 = jnp.dot(attn.astype(BF16), wba_ref[...], preferred_element_type=F32)
    a2 = jnp.dot(sgu.astype(BF16), wbs_ref[...], preferred_element_type=F32)
    merged = jax.nn.sigmoid(ga) * a1 + jax.nn.sigmoid(gs) * a2
    mix = jnp.dot(merged.astype(BF16), wout_ref[...], preferred_element_type=F32)
    x1 = _layer_norm(DN_ALPHA * x + g1 * mix, l1g_ref[...], l1b_ref[...])
    x1_ref[...] = x1
    h2 = x1 * (1.0 + sc2) + sh2
    hi = h2.astype(BF16)
    h2_ref[...] = hi

    lo_part = (h2 - hi.astype(F32)).astype(BF16)
    l1 = jnp.dot(hi, wr_ref[...], preferred_element_type=F32)
    l2 = jnp.dot(lo_part, wr_ref[:, 0:N_EXPERTS], preferred_element_type=F32)
    logit_ref[...] = l1[:, 0:N_EXPERTS] + l1[:, N_EXPERTS:2 * N_EXPERTS] + l2 + br_ref[...]


PACK = 4096.0
assert DISP_ROWS <= PACK and COMB_ROWS <= PACK and DISP_ROWS * PACK < 2 ** 24


def _router_kernel(l_ref, rcol_ref, rrow_ref, cnt_ref):
    rt = l_ref.shape[0]
    ns = rt // TM
    padded = jnp.concatenate([l_ref[...], jnp.zeros((rt, LANES - N_EXPERTS), F32)], axis=1)
    work = padded.T[0:N_EXPERTS]
    eidx = lax.broadcasted_iota(I32, (N_EXPERTS, rt), 0)
    idxs, vals = [], []
    for _ in range(TOP_K):
        m = jnp.max(work, axis=0, keepdims=True)
        ix = jnp.min(jnp.where(work == m, eidx, N_EXPERTS), axis=0, keepdims=True)
        idxs.append(ix)
        vals.append(m)
        work = jnp.where(eidx == ix, NEG_INF, work)
    exps = [jnp.exp(v - vals[0]) for v in vals]
    esum = exps[0] + exps[1] + exps[2] + exps[3]
    wts = [e / esum for e in exps]

    sel = jnp.zeros((N_EXPERTS, rt), F32)
    for ix in idxs:
        sel = sel + jnp.where(eidx == ix, 1.0, 0.0)
    sel_b = sel.astype(BF16)
    ti = lax.broadcasted_iota(I32, (TM, TM), 0)
    tj = lax.broadcasted_iota(I32, (TM, TM), 1)
    earlier = jnp.where(ti < tj, 1.0, 0.0).astype(BF16)
    tiles = [slice(s * TM, (s + 1) * TM) for s in range(ns)]
    ranks = [jnp.dot(sel_b[:, t], earlier, preferred_element_type=F32) for t in tiles]
    cnts, nchs = [], []
    for t in tiles:
        cnt = jnp.sum(sel[:, t], axis=1, keepdims=True)
        cnt = cnt + (cnt - 2.0 * jnp.floor(cnt * 0.5))
        cnts.append(cnt)
        nchs.append(jnp.maximum(jnp.floor((cnt + (COMB_CHUNK - 0.5)) * (1.0 / COMB_CHUNK)), 1.0))
    lane = lax.broadcasted_iota(I32, (N_EXPERTS, LANES), 1)
    pre = jnp.zeros((N_EXPERTS, LANES), F32)
    for j, col in enumerate(cnts + nchs):
        pre = jnp.where(lane == j, col, pre)
    ei = lax.broadcasted_iota(I32, (N_EXPERTS, N_EXPERTS), 0)
    ej = lax.broadcasted_iota(I32, (N_EXPERTS, N_EXPERTS), 1)
    before = jnp.where(ej < ei, 1.0, 0.0).astype(BF16)
    base = jnp.dot(before, pre.astype(BF16), preferred_element_type=F32)
    both = jnp.concatenate(
        [(ranks[s] + base[:, s:s + 1]) * PACK + (ranks[s] + base[:, ns + s:ns + s + 1] * COMB_CHUNK)
         for s in range(ns)], axis=1)

    packed = [jnp.sum(jnp.where(eidx == ix, both, 0.0), axis=0, keepdims=True) for ix in idxs]
    drow = [jnp.floor(v * (1.0 / PACK)) for v in packed]
    table = [ix.astype(F32) for ix in idxs] + drow + [v - d * PACK for v, d in zip(packed, drow)] + wts
    sub = lax.broadcasted_iota(I32, (LANES, rt), 0)
    rr = jnp.zeros((LANES, rt), F32)
    for j, row in enumerate(table):
        rr = jnp.where(sub == j, row, rr)
    rcol_ref[...] = rr.T
    for s in range(ns):
        rrow_ref[s] = rr[0:16, tiles[s]]
        cnt_ref[s] = jnp.broadcast_to(cnts[s], (N_EXPERTS, LANES)).astype(I32)


def _router(logits):
    T = logits.shape[0]
    rt = math.gcd(T, ROUTER_TOKENS)
    nt = T // TM
    assert 2 * (rt // TM) <= LANES
    return pl.pallas_call(
        _router_kernel,
        grid=(T // rt,),
        in_specs=[pl.BlockSpec((rt, N_EXPERTS), lambda i: (i, 0))],
        out_specs=[pl.BlockSpec((rt, LANES), lambda i: (i, 0)),
                   pl.BlockSpec((rt // TM, 16, TM), lambda i: (i, 0, 0)),
                   pl.BlockSpec((rt // TM, N_EXPERTS, LANES), lambda i: (i, 0, 0))],
        out_shape=[jax.ShapeDtypeStruct((T, LANES), F32),
                   jax.ShapeDtypeStruct((nt, 16, TM), F32),
                   jax.ShapeDtypeStruct((nt, N_EXPERTS, LANES), I32)],
        compiler_params=pltpu.CompilerParams(dimension_semantics=("arbitrary",),
                                             vmem_limit_bytes=VMEM_LIMIT_BYTES),
        name="router",
    )(logits)


def _mixer(seqs, sink, xp, xs, mod3, cos_t, sin_t, kt, v, p):
    T = seqs.n_tokens
    nt = T // MT
    nb = T // BLOCK
    r = MT // BLOCK

    def mod_map(i, s):
        return (_tile_pos(seqs, i * MT)[2], 0, 0)

    def rope_map(i, s):
        return (_tile_pos(seqs, i * MT)[1] // MT, 0)

    const2 = lambda i, s: (0, 0)
    once = dict(pipeline_mode=pl.Buffered(1))
    in_specs = _group_x_specs(seqs, MT) + [
        pl.BlockSpec((1, 1, 6 * D_MODEL), mod_map),
        pl.BlockSpec((MT, LANES), rope_map),
        pl.BlockSpec((MT, LANES), rope_map),
        pl.BlockSpec((2 * LANES, BLOCK), lambda i, s: (0, jnp.maximum(i * r - 1, 0))),
        pl.BlockSpec((2 * LANES, MT), lambda i, s: (0, i)),
        pl.BlockSpec((2 * LANES, BLOCK), lambda i, s: (0, jnp.minimum(i * r + r, nb - 1))),
        pl.BlockSpec((BLOCK, 2 * LANES), lambda i, s: (jnp.maximum(i * r - 1, 0), 0)),
        pl.BlockSpec((MT, 2 * LANES), lambda i, s: (i, 0)),
        pl.BlockSpec((BLOCK, 2 * LANES), lambda i, s: (jnp.minimum(i * r + r, nb - 1), 0)),
        pl.BlockSpec(p["w_mix"].shape, const2, **once),
        pl.BlockSpec(p["b_mix"].shape, const2, **once),
        pl.BlockSpec(p["w_sp"].shape, lambda i, s: (0, 0, 0), **once),
        pl.BlockSpec(p["b_spt"].shape, const2, **once),
        pl.BlockSpec(p["sgu_g"].shape, const2, **once),
        pl.BlockSpec(p["sgu_b"].shape, const2, **once),
        pl.BlockSpec(p["w_ba"].shape, const2, **once),
        pl.BlockSpec(p["w_bs"].shape, const2, **once),
        pl.BlockSpec(p["w_out"].shape, const2, **once),
        pl.BlockSpec(p["ln1_g"].shape, const2, **once),
        pl.BlockSpec(p["ln1_b"].shape, const2, **once),
        pl.BlockSpec(p["w_r"].shape, const2, **once),
        pl.BlockSpec(p["b_r"].shape, const2, **once),
    ]
    out_specs = [
        pl.BlockSpec((MT, D_MODEL), lambda i, s: (i, 0)),
        pl.BlockSpec((MT, D_MODEL), lambda i, s: (i, 0)),
        pl.BlockSpec((MT, N_EXPERTS), lambda i, s: (i, 0)),
    ]
    out_shape = [
        jax.ShapeDtypeStruct((T, D_MODEL), F32),
        jax.ShapeDtypeStruct((T, D_MODEL), BF16),
        jax.ShapeDtypeStruct((T, N_EXPERTS), F32),
    ]
    return pl.pallas_call(
        functools.partial(_mixer_kernel, seqs),
        grid_spec=pltpu.PrefetchScalarGridSpec(
            num_scalar_prefetch=1, grid=(nt,), in_specs=in_specs, out_specs=out_specs),
        out_shape=out_shape,
        compiler_params=pltpu.CompilerParams(dimension_semantics=("arbitrary",),
                                             vmem_limit_bytes=VMEM_LIMIT_BYTES),
        name="mixer",
    )(sink, xp, xs, mod3, cos_t, sin_t, kt, kt, kt, v, v, v,
      p["w_mix"], p["b_mix"], p["w_sp"], p["b_spt"], p["sgu_g"], p["sgu_b"],
      p["w_ba"], p["w_bs"], p["w_out"], p["ln1_g"], p["ln1_b"], p["w_r"], p["b_r"])


def _pair_rows(row):
    return pl.multiple_of((row // PAIR) * ROW_TILES, ROW_TILES)


def _to_row_tiles(dst_ref, rows, n):
    words = pltpu.bitcast(rows, U32)
    for c in range(ROW_TILES):
        dst_ref[pl.ds(c, n // PAIR, stride=ROW_TILES), :] = words[:, c * LANES:(c + 1) * LANES]


def _from_row_tiles(src_ref, start, n):
    base = start // PAIR * ROW_TILES
    words = jnp.concatenate(
        [src_ref[pl.ds(base + c, n // PAIR, stride=ROW_TILES), :] for c in range(ROW_TILES)], axis=1)
    return pltpu.bitcast(words, BF16)


WAIT_GROUP = 16


def _issue_tile(copy, k0, src_ref, dst_ref, extra_ref, n_extra, step):
    for e in range(N_EXPERTS):
        copy(src_ref[k0 + e], dst_ref[k0 + e]).start()

    @pl.when(n_extra > 0)
    def _():
        def per_expert(e, c):
            def per_chunk(j, c2):
                copy(src_ref[k0 + e] + j * step, dst_ref[k0 + e] + j * step).start()
                return c2
            lax.fori_loop(1, extra_ref[k0 + e] + 1, per_chunk, 0)
            return c
        lax.fori_loop(0, N_EXPERTS, per_expert, 0)


def _wait_tile(group_copy, chunk_copy, n_extra):
    for _ in range(N_EXPERTS // WAIT_GROUP):
        group_copy.wait()

    def body(_, c):
        chunk_copy.wait()
        return c
    lax.fori_loop(0, n_extra, body, 0)


def _dispatch_kernel(b_ref, s_ref, x_ref, nx_ref, tail_ref, h2_ref, rrow_ref, xin_ref, stg_ref, zero_ref, sem):
    i = pl.program_id(0)
    nt = pl.num_programs(0)
    slot = i % 2
    rr = rrow_ref[0]
    rho = lax.broadcasted_iota(I32, (DISP_ROWS, TM), 0)
    pt = jnp.zeros((DISP_ROWS, TM), F32)
    for k in range(TOP_K):
        pt = pt + jnp.where(rho == rr[TOP_K + k:TOP_K + k + 1].astype(I32), 1.0, 0.0)
    rows = jnp.dot(pt.astype(BF16), h2_ref[...], preferred_element_type=F32).astype(BF16)

    chunk = DISP_CHUNK // PAIR * ROW_TILES

    def copy(src_row, dst_row, sl):
        return pltpu.make_async_copy(stg_ref.at[sl, pl.ds(_pair_rows(src_row), chunk)],
                                     xin_ref.at[pl.ds(_pair_rows(dst_row), chunk)], sem)

    def wait_tile(tile):
        group = pltpu.make_async_copy(stg_ref.at[0, pl.ds(0, WAIT_GROUP * chunk)],
                                      xin_ref.at[pl.ds(0, WAIT_GROUP * chunk)], sem)
        _wait_tile(group, copy(0, 0, 0), nx_ref[tile])

    for sl in range(2):
        @pl.when(slot == sl)
        def _():
            _to_row_tiles(stg_ref.at[sl], rows, DISP_ROWS)

            @pl.when(i > 0)
            def _():
                wait_tile(i - 1)
            _issue_tile(lambda src, dst: copy(src, dst, sl), i * N_EXPERTS, b_ref, s_ref, x_ref, nx_ref[i],
                        DISP_CHUNK)

    @pl.when(i == nt - 1)
    def _():
        wait_tile(i)
        zero_ref[...] = jnp.zeros_like(zero_ref)

        def zcopy(dst_row):
            return pltpu.make_async_copy(zero_ref, xin_ref.at[pl.ds(_pair_rows(dst_row), chunk)], sem)

        def zwait(count):
            def body(_, c):
                zcopy(0).wait()
                return c
            lax.fori_loop(0, count, body, 0)

        def per_expert(e, total):
            lo = tail_ref[e]
            nz = (tail_ref[N_EXPERTS + e] - lo) // DISP_CHUNK

            def per_chunk(j, c):
                zcopy(lo + j * DISP_CHUNK).start()
                return c
            lax.fori_loop(0, nz, per_chunk, 0)
            return total + nz
        zwait(lax.fori_loop(0, N_EXPERTS, per_expert, 0))

        def last_chunk(e, c):
            zcopy(tail_ref[N_EXPERTS + e] - DISP_CHUNK).start()
            return c
        lax.fori_loop(0, N_EXPERTS, last_chunk, 0)
        zwait(N_EXPERTS)


def _dispatch(seqs, b_te, s_te, x_te, nx_t, tails, h2, rrow, n_rows):
    nt = seqs.n_tokens // TM
    return pl.pallas_call(
        _dispatch_kernel,
        grid_spec=pltpu.PrefetchScalarGridSpec(
            num_scalar_prefetch=5, grid=(nt,),
            in_specs=[pl.BlockSpec((TM, D_MODEL), lambda i, *_: (i, 0)),
                      pl.BlockSpec((1, 16, TM), lambda i, *_: (i, 0, 0))],
            out_specs=pl.BlockSpec(memory_space=pl.ANY),
            scratch_shapes=[pltpu.VMEM((2, DISP_ROWS // PAIR * ROW_TILES, LANES), U32),
                            pltpu.VMEM((DISP_CHUNK // PAIR * ROW_TILES, LANES), U32),
                            pltpu.SemaphoreType.DMA]),
        out_shape=jax.ShapeDtypeStruct((n_rows // PAIR * ROW_TILES, LANES), U32),
        compiler_params=pltpu.CompilerParams(dimension_semantics=("arbitrary",),
                                             vmem_limit_bytes=VMEM_LIMIT_BYTES),
        name="dispatch",
    )(b_te, s_te, x_te, nx_t, tails, h2, rrow)


def _expert_kernel(be_ref, bv_ref, x_ref, wup_hbm, bup_ref, wdn_hbm, bdn_ref, o_ref,
                   wup_f, wdn_f, wup_b, wdn_b, sem):
    i = pl.program_id(0)
    e = be_ref[i]
    valid = bv_ref[i]
    first = jnp.logical_or(i == 0, e != be_ref[jnp.maximum(i - 1, 0)])

    def fetch(ex):
        slot = ex % 2
        return (pltpu.make_async_copy(wup_hbm.at[ex], wup_f.at[slot], sem.at[0, slot]),
                pltpu.make_async_copy(wdn_hbm.at[ex], wdn_f.at[slot], sem.at[1, slot]))

    @pl.when(i == 0)
    def _():
        for d in fetch(e):
            d.start()

    @pl.when(first)
    def _():
        for d in fetch(e):
            d.wait()

        @pl.when(e + 1 < N_EXPERTS)
        def _():
            for d in fetch(e + 1):
                d.start()

        @pl.when(valid > 0)
        def _():
            wup_b[...] = wup_f[e % 2].astype(BF16)
            wdn_b[...] = wdn_f[e % 2].astype(BF16)

    def ffn(n):
        x = _from_row_tiles(x_ref, 0, n)
        row = lax.broadcasted_iota(I32, (n, 1), 0)
        xb = jnp.where(row < valid, x, jnp.zeros_like(x))
        hu = jnp.dot(xb, wup_b[...], preferred_element_type=F32) + bup_ref[0]
        gate = jnp.minimum(hu[:, 0:D_FF], SWIGLU_LIMIT)
        lin = jnp.clip(hu[:, D_FF:], -SWIGLU_LIMIT, SWIGLU_LIMIT)
        act = gate * jax.nn.sigmoid(SWIGLU_ALPHA * gate) * (lin + 1.0)
        y = jnp.dot(act.astype(BF16), wdn_b[...], preferred_element_type=F32) + bdn_ref[0]
        _to_row_tiles(o_ref, y.astype(BF16), n)

    half = BM // 2

    @pl.when(valid > half)
    def _():
        ffn(BM)

    @pl.when(jnp.logical_and(valid > 0, valid <= half))
    def _():
        ffn(half)
        o_ref[half // PAIR * ROW_TILES:, :] = jnp.zeros((half // PAIR * ROW_TILES, LANES), U32)

    @pl.when(valid == 0)
    def _():
        o_ref[...] = jnp.zeros_like(o_ref)


def _experts(blk_e, blk_v, xin, w_up, b_up, w_down, b_down, n_blk):
    return pl.pallas_call(
        _expert_kernel,
        grid_spec=pltpu.PrefetchScalarGridSpec(
            num_scalar_prefetch=2, grid=(n_blk,),
            in_specs=[pl.BlockSpec((BM // PAIR * ROW_TILES, LANES), lambda i, be, bv: (i, 0)),
                      pl.BlockSpec(memory_space=pl.ANY),
                      pl.BlockSpec((1, 1, 2 * D_FF), lambda i, be, bv: (be[i], 0, 0)),
                      pl.BlockSpec(memory_space=pl.ANY),
                      pl.BlockSpec((1, 1, D_MODEL), lambda i, be, bv: (be[i], 0, 0))],
            out_specs=pl.BlockSpec((BM // PAIR * ROW_TILES, LANES), lambda i, be, bv: (i, 0)),
            scratch_shapes=[pltpu.VMEM((2, D_MODEL, 2 * D_FF), F32),
                            pltpu.VMEM((2, D_FF, D_MODEL), F32),
                            pltpu.VMEM((D_MODEL, 2 * D_FF), BF16),
                            pltpu.VMEM((D_FF, D_MODEL), BF16),
                            pltpu.SemaphoreType.DMA((2, 2))]),
        out_shape=jax.ShapeDtypeStruct((n_blk * BM // PAIR * ROW_TILES, LANES), U32),
        compiler_params=pltpu.CompilerParams(dimension_semantics=("arbitrary",),
                                             vmem_limit_bytes=VMEM_LIMIT_BYTES),
        name="experts",
    )(blk_e, blk_v, xin, w_up, b_up.reshape(N_EXPERTS, 1, 2 * D_FF), w_down,
      b_down.reshape(N_EXPERTS, 1, D_MODEL))


COMB_KC = COMB_ROWS // 256
COMB_KC_MIN = N_EXPERTS * COMB_CHUNK // 256


def _combine_kernel(tile0, s_ref, b_ref, x_ref, nx_ref, rows_ref, x1_ref, rcol_ref, mod_ref, g_ref, bb_ref,
                    eo_ref, y_ref, stg_ref, acc_ref, sem):
    i = pl.program_id(0)
    nt = pl.num_programs(0)
    slot = i % 2
    chunk = COMB_CHUNK // PAIR * ROW_TILES

    def copy(src_row, dst_row, sl):
        return pltpu.make_async_copy(eo_ref.at[pl.ds(_pair_rows(src_row), chunk)],
                                     stg_ref.at[sl, pl.ds(_pair_rows(dst_row), chunk)], sem.at[sl])

    def issue(tile, sl):
        _issue_tile(lambda src, dst: copy(src, dst, sl), tile * N_EXPERTS, s_ref, b_ref, x_ref, nx_ref[tile],
                    COMB_CHUNK)

    @pl.when(i == 0)
    def _():
        stg_ref[...] = jnp.zeros_like(stg_ref)
        issue(tile0, 0)

    rc = rcol_ref[...]
    col_k = [rc[:, 2 * TOP_K + k:2 * TOP_K + k + 1].astype(I32) for k in range(TOP_K)]
    w_k = [rc[:, 3 * TOP_K + k:3 * TOP_K + k + 1] for k in range(TOP_K)]

    def chunk_dot(sl, c):
        jl = lax.broadcasted_iota(I32, (TM, 256), 1) + c * 256
        pm = jnp.zeros((TM, 256), F32)
        for k in range(TOP_K):
            pm = pm + jnp.where(jl == col_k[k], w_k[k], 0.0)
        rows = _from_row_tiles(stg_ref.at[sl], c * 256, 256)
        return jnp.dot(pm.astype(BF16), rows, preferred_element_type=F32)

    for sl in range(2):
        @pl.when(slot == sl)
        def _():
            @pl.when(i + 1 < nt)
            def _():
                issue(tile0 + i + 1, 1 - sl)

            group = pltpu.make_async_copy(eo_ref.at[pl.ds(0, WAIT_GROUP * chunk)],
                                          stg_ref.at[sl, pl.ds(0, WAIT_GROUP * chunk)], sem.at[sl])
            _wait_tile(group, copy(0, 0, sl), nx_ref[tile0 + i])

            y = chunk_dot(sl, 0)
            for c in range(1, COMB_KC_MIN):
                y = y + chunk_dot(sl, c)
            acc_ref[...] = y
            used = (rows_ref[tile0 + i] + 255) // 256
            for c in range(COMB_KC_MIN, COMB_KC):
                @pl.when(c < used)
                def _():
                    acc_ref[...] += chunk_dot(sl, c)

    mod = mod_ref[0]
    g2 = mod[:, 5 * D_MODEL:6 * D_MODEL]
    y_ref[...] = _layer_norm(DN_ALPHA * x1_ref[...] + g2 * acc_ref[...], g_ref[...], bb_ref[...])


def _combine(seqs, tile0, n_tiles, s_te, b_te, x_te, nx_t, rows_t, x1, rcol, mod3, ln2_g, ln2_b, eo):
    def mod_map(i, *_):
        return (_tile_pos(seqs, (i + tile0) * TM)[2], 0, 0)

    return pl.pallas_call(
        functools.partial(_combine_kernel, tile0),
        grid_spec=pltpu.PrefetchScalarGridSpec(
            num_scalar_prefetch=5, grid=(n_tiles,),
            in_specs=[pl.BlockSpec((TM, D_MODEL), lambda i, *_: (i + tile0, 0)),
                      pl.BlockSpec((TM, LANES), lambda i, *_: (i + tile0, 0)),
                      pl.BlockSpec((1, 1, 6 * D_MODEL), mod_map),
                      pl.BlockSpec((1, D_MODEL), lambda i, *_: (0, 0)),
                      pl.BlockSpec((1, D_MODEL), lambda i, *_: (0, 0)),
                      pl.BlockSpec(memory_space=pl.ANY)],
            out_specs=pl.BlockSpec((TM, D_MODEL), lambda i, *_: (i, 0)),
            scratch_shapes=[pltpu.VMEM((2, COMB_ROWS // PAIR * ROW_TILES, LANES), U32),
                            pltpu.VMEM((TM, D_MODEL), F32),
                            pltpu.SemaphoreType.DMA((2,))]),
        out_shape=jax.ShapeDtypeStruct((n_tiles * TM, D_MODEL), F32),
        compiler_params=pltpu.CompilerParams(dimension_semantics=("arbitrary",),
                                             vmem_limit_bytes=VMEM_LIMIT_BYTES),
        name="combine",
    )(s_te, b_te, x_te, nx_t, rows_t, x1, rcol, mod3, ln2_g, ln2_b, eo)


def _prep_params(w_in, b_in, w_spatial, b_spatial, sgu_ln_g, sgu_ln_b, w_br_attn, w_br_sgu, w_out,
                 ln1_g, ln1_b, w_router, b_router):
    q_end, k_end, v_end = ATT_W, ATT_W + KV_W, ATT_W + 2 * KV_W

    def dup(w, lo):
        h0, h1 = w[..., lo:lo + HEAD_DIM], w[..., lo + HEAD_DIM:lo + 2 * HEAD_DIM]
        return jnp.concatenate([h0, h0, h1, h1], axis=-1)

    w_kv = jnp.concatenate([dup(w_in, q_end), dup(w_in, k_end)], axis=1).astype(BF16)
    b_kv = jnp.concatenate([dup(b_in, q_end), dup(b_in, k_end)], axis=0).reshape(1, -1)
    w_mix = w_in.astype(BF16)
    b_mix = b_in.reshape(1, -1)
    w_hi = w_router.astype(BF16)
    w_lo = (w_router - w_hi.astype(F32)).astype(BF16)
    p = dict(
        w_mix=w_mix, b_mix=b_mix,
        w_sp=w_spatial.astype(BF16), b_spt=b_spatial.T,
        sgu_g=sgu_ln_g.reshape(1, -1), sgu_b=sgu_ln_b.reshape(1, -1),
        w_ba=w_br_attn.astype(BF16), w_bs=w_br_sgu.astype(BF16), w_out=w_out.astype(BF16),
        ln1_g=ln1_g.reshape(1, -1), ln1_b=ln1_b.reshape(1, -1),
        w_r=jnp.concatenate([w_hi, w_lo], axis=1), b_r=b_router.reshape(1, -1),
    )
    return w_kv, b_kv, p


def _layer(seqs, xp, xs, c_all, w_ada, b_ada, w_in, b_in, sink, sgu_ln_g, sgu_ln_b, w_spatial, b_spatial,
           w_br_attn, w_br_sgu, w_out, ln1_g, ln1_b, w_router, b_router, w_up, b_up, w_down, b_down,
           ln2_g, ln2_b):
    T = seqs.n_tokens
    nt = T // TM
    ntp = seqs.n_prompt // TM
    c_pad = jnp.zeros((8, D_MODEL), F32).at[:c_all.shape[0]].set(c_all)
    mod3 = _ada(c_pad, w_ada, b_ada).reshape(8, 1, 6 * D_MODEL)
    cos_t, sin_t = _rope_tables(max(seqs.prompt_len, seqs.sample_len))
    w_kv, b_kv, p = _prep_params(w_in, b_in, w_spatial, b_spatial, sgu_ln_g, sgu_ln_b, w_br_attn,
                                 w_br_sgu, w_out, ln1_g, ln1_b, w_router, b_router)
    kt, v = _kv(seqs, xp, xs, mod3, cos_t, sin_t, w_kv, b_kv)
    x1, h2, logits = _mixer(seqs, sink, xp, xs, mod3, cos_t, sin_t, kt, v, p)
    rcol, rrow, cnt3 = _router(logits)

    cnt = cnt3[:, :, 0]
    count = cnt.sum(0)
    reg = (count + REGION_SLACK + BM - 1) // BM * BM
    pad_end = jnp.cumsum(reg)
    pad_start = pad_end - reg
    flat = lambda a: a.reshape(-1).astype(I32)
    s_te = flat(pad_start[None, :] + jnp.cumsum(cnt, axis=0) - cnt)
    d_nch = jnp.maximum((cnt + DISP_CHUNK - 1) // DISP_CHUNK, 1)
    c_nch = jnp.maximum((cnt + COMB_CHUNK - 1) // COMB_CHUNK, 1)
    db_te = flat(jnp.cumsum(cnt, axis=1) - cnt)
    cb_te = flat(COMB_CHUNK * (jnp.cumsum(c_nch, axis=1) - c_nch))
    rows_t = flat(COMB_CHUNK * c_nch.sum(1))
    n_blk = (nt * TILE_ROWS + N_EXPERTS * (REGION_SLACK + BM - 1)) // BM + 1
    tails = jnp.concatenate([pad_start + count, pad_end[:-1], jnp.array([n_blk * BM])]).astype(I32)
    blk_start = jnp.arange(n_blk, dtype=I32) * BM
    blk_e = jnp.minimum((blk_start[:, None] >= pad_end[None, :]).sum(1), N_EXPERTS - 1).astype(I32)
    owner = blk_e[:, None] == jnp.arange(N_EXPERTS)[None, :]
    blk_rows = (owner * (pad_start + count)[None, :]).sum(1) - blk_start
    blk_v = jnp.where(blk_start < pad_end[-1], jnp.clip(blk_rows, 0, BM), 0).astype(I32)

    xin = _dispatch(seqs, db_te, s_te, flat(d_nch - 1), flat((d_nch - 1).sum(1)), tails, h2, rrow, n_blk * BM)
    eo = _experts(blk_e, blk_v, xin, w_up, b_up, w_down, b_down, n_blk)
    comb = (s_te, cb_te, flat(c_nch - 1), flat((c_nch - 1).sum(1)), rows_t,
            x1, rcol, mod3, ln2_g.reshape(1, -1), ln2_b.reshape(1, -1), eo)
    return _combine(seqs, 0, ntp, *comb), _combine(seqs, ntp, nt - ntp, *comb)


def kernel(x_prompt, x_sample, c_prompt, c_sample, w_ada, b_ada, w_in, b_in, sink, sgu_ln_g, sgu_ln_b, w_spatial, b_spatial, w_br_attn, w_br_sgu, w_out, ln1_g, ln1_b, w_router, b_router, w_up, b_up, w_down, b_down, ln2_g, ln2_b):
    assert w_ada.shape[0] == DEPTH == 1
    bp, sp, d = x_prompt.shape
    bs, ss, _ = x_sample.shape
    seqs = _Seqs(n_prompt=bp * sp, prompt_len=sp, sample_len=ss, n_tokens=bp * sp + bs * ss)
    c_all = jnp.concatenate([c_prompt, c_sample], axis=0)
    yp, ys = _layer(seqs, x_prompt.reshape(bp * sp, d), x_sample.reshape(bs * ss, d), c_all,
                    w_ada[0], b_ada[0], w_in[0], b_in[0], sink[0], sgu_ln_g[0], sgu_ln_b[0],
                    w_spatial[0], b_spatial[0], w_br_attn[0], w_br_sgu[0], w_out[0], ln1_g[0], ln1_b[0],
                    w_router[0], b_router[0], w_up[0], b_up[0], w_down[0], b_down[0], ln2_g[0], ln2_b[0])
    return (yp.reshape(bp, sp, d), ys.reshape(bs, ss, d))
```

```python
import functools
import math
from typing import NamedTuple

import jax
import jax.numpy as jnp
from jax import lax
from jax.experimental import pallas as pl
from jax.experimental.pallas import tpu as pltpu

F32 = jnp.float32
BF16 = jnp.bfloat16
I32 = jnp.int32

D_MODEL = 1024
N_HEADS = 8
N_KV_HEADS = 2
HEAD_DIM = 64
ATT_W = N_HEADS * HEAD_DIM
KV_W = N_KV_HEADS * HEAD_DIM
BLOCK = 128
ROPE_THETA = 500000.0
ROT_DIM = HEAD_DIM // 4
ROT_HALF = ROT_DIM // 2
SGU_W = D_MODEL // 2
SGU_GROUPS = 4
N_EXPERTS = 32
TOP_K = 4
D_FF = D_MODEL
SWIGLU_LIMIT = 7.0
SWIGLU_ALPHA = 1.702
LN_EPS = 1e-5
DEPTH = 1
DN_ALPHA = (2 * DEPTH) ** 0.25

LANES = 128
SUBLANES = 8
ROW_TILES = D_MODEL // LANES
VMEM_LIMIT_BYTES = 56 * 1024 * 1024
U32 = jnp.uint32
PAIR = 2

TM = 256
MT = 512
TK = 1024
BM = 512
ROUTER_TOKENS = 2048
DISP_CHUNK = 64
COMB_CHUNK = 48
REGION_SLACK = DISP_CHUNK
TILE_ROWS = TM * TOP_K + N_EXPERTS
DISP_ROWS = TILE_ROWS + DISP_CHUNK
DISP_BLOCK = DISP_ROWS // 5
assert DISP_BLOCK * 5 == DISP_ROWS and DISP_BLOCK % 16 == 0
COMB_ROWS = (TILE_ROWS + N_EXPERTS * COMB_CHUNK + 255) // 256 * 256
NEG_INF = float("-inf")


class _Seqs(NamedTuple):
    n_prompt: int
    prompt_len: int
    sample_len: int
    n_tokens: int


def _tile_pos(seqs, t0):
    is_s = t0 >= seqs.n_prompt
    seq_len = jnp.where(is_s, seqs.sample_len, seqs.prompt_len)
    off = jnp.where(is_s, t0 - seqs.n_prompt, t0)
    pos0 = off % seq_len
    row = jnp.where(is_s, seqs.n_prompt // seqs.prompt_len + off // seq_len, off // seq_len)
    return seq_len, pos0, row


def _ada_kernel(c_ref, w_ref, b_ref, o_ref):
    c = c_ref[...]
    a = c * jax.nn.sigmoid(c)
    o_ref[...] = jnp.dot(a, w_ref[...], preferred_element_type=F32,
                         precision=lax.Precision.HIGHEST) + b_ref[...]


def _ada(c_pad, w_ada, b_ada):
    n = w_ada.shape[1]
    bn = 1536
    return pl.pallas_call(
        _ada_kernel,
        grid=(n // bn,),
        in_specs=[pl.BlockSpec((8, D_MODEL), lambda j: (0, 0)),
                  pl.BlockSpec((D_MODEL, bn), lambda j: (0, j)),
                  pl.BlockSpec((1, bn), lambda j: (0, j))],
        out_specs=pl.BlockSpec((8, bn), lambda j: (0, j)),
        out_shape=jax.ShapeDtypeStruct((8, n), F32),
        compiler_params=pltpu.CompilerParams(vmem_limit_bytes=VMEM_LIMIT_BYTES),
        name="ada",
    )(c_pad, w_ada, b_ada.reshape(1, n))


def _rope_tables(length):
    inv = ROPE_THETA ** (-jnp.arange(ROT_HALF, dtype=F32) * 2.0 / ROT_DIM)
    ang = jnp.arange(length, dtype=F32)[:, None] * inv[None, :]
    lane = jnp.arange(LANES) % HEAD_DIM
    spread = ((lane[None, :] % ROT_HALF == jnp.arange(ROT_HALF)[:, None]) & (lane[None, :] < ROT_DIM)).astype(F32)
    sign = jnp.where(lane < ROT_HALF, -1.0, 1.0)
    exact = dict(precision=lax.Precision.HIGHEST, preferred_element_type=F32)
    cos = jnp.dot(jnp.cos(ang), spread, **exact) + (lane >= ROT_DIM).astype(F32)[None, :]
    sin = jnp.dot(jnp.sin(ang), spread * sign[None, :], **exact)
    return cos, sin


def _rope(x, cos, sin):
    n = x.shape[1]
    reps = n // LANES
    c = jnp.concatenate([cos] * reps, axis=1)
    s = jnp.concatenate([sin] * reps, axis=1)
    lane = lax.broadcasted_iota(I32, x.shape, 1)
    first = (lane & (HEAD_DIM - 1)) < ROT_HALF
    partner = jnp.where(first, pltpu.roll(x, n - ROT_HALF, 1), pltpu.roll(x, ROT_HALF, 1))
    return x * c + partner * s


def _kv_kernel(seqs, xp_ref, xs_ref, mod_ref, cos_ref, sin_ref, w_ref, b_ref, kt_ref, v_ref):
    mod = mod_ref[0]
    sh1 = mod[:, 0:D_MODEL]
    sc1 = mod[:, D_MODEL:2 * D_MODEL]
    x = jnp.where(pl.program_id(0) * TK >= seqs.n_prompt, xs_ref[...], xp_ref[...])
    h = (x * (1.0 + sc1) + sh1).astype(BF16)
    kv = jnp.dot(h, w_ref[...], preferred_element_type=F32) + b_ref[...]
    k = _rope(kv[:, 0:2 * LANES], cos_ref[...], sin_ref[...])
    kt_ref[...] = k.T.astype(BF16)
    v_ref[...] = kv[:, 2 * LANES:4 * LANES].astype(BF16)


def _group_x_specs(seqs, tile):
    ntp = seqs.n_prompt // tile
    return [pl.BlockSpec((tile, D_MODEL), lambda i, *_: (jnp.minimum(i, ntp - 1), 0)),
            pl.BlockSpec((tile, D_MODEL), lambda i, *_: (jnp.maximum(i - ntp, 0), 0))]


def _kv(seqs, xp, xs, mod3, cos_t, sin_t, w_kv, b_kv):
    T = seqs.n_tokens

    def mod_map(i):
        return (_tile_pos(seqs, i * TK)[2], 0, 0)

    def rope_map(i):
        return (_tile_pos(seqs, i * TK)[1] // TK, 0)

    return pl.pallas_call(
        functools.partial(_kv_kernel, seqs),
        grid=(T // TK,),
        in_specs=_group_x_specs(seqs, TK) + [
                  pl.BlockSpec((1, 1, 6 * D_MODEL), mod_map),
                  pl.BlockSpec((TK, LANES), rope_map),
                  pl.BlockSpec((TK, LANES), rope_map),
                  pl.BlockSpec((D_MODEL, 4 * LANES), lambda i: (0, 0)),
                  pl.BlockSpec((1, 4 * LANES), lambda i: (0, 0))],
        out_specs=[pl.BlockSpec((2 * LANES, TK), lambda i: (0, i)),
                   pl.BlockSpec((TK, 2 * LANES), lambda i: (i, 0))],
        out_shape=[jax.ShapeDtypeStruct((2 * LANES, T), BF16),
                   jax.ShapeDtypeStruct((T, 2 * LANES), BF16)],
        compiler_params=pltpu.CompilerParams(dimension_semantics=("arbitrary",),
                                             vmem_limit_bytes=VMEM_LIMIT_BYTES),
        name="kv",
    )(xp, xs, mod3, cos_t, sin_t, w_kv, b_kv)


def _layer_norm(x, g, b):
    mu = jnp.mean(x, axis=-1, keepdims=True)
    xc = x - mu
    var = jnp.mean(xc * xc, axis=-1, keepdims=True)
    return xc * lax.rsqrt(var + LN_EPS) * g + b


def _attention(q, kfull, vfull, valids, sink_ref):
    lane = lax.broadcasted_iota(I32, (BLOCK, LANES), 1)
    lo = lane < HEAD_DIM
    ones = jnp.ones((3 * BLOCK, LANES), BF16)
    units = [(jb, hk) for jb in range(MT // BLOCK) for hk in range(N_KV_HEADS)]
    scores, sinks = [], []
    for jb, hk in units:
        parts = []
        for p in range(2):
            qp = q[jb * BLOCK:(jb + 1) * BLOCK, (2 * hk + p) * LANES:(2 * hk + p + 1) * LANES]
            parts.append(jnp.where(lo, qp, 0.0).astype(BF16))
            parts.append(jnp.where(lo, 0.0, qp).astype(BF16))
        kwin = kfull[hk * LANES:(hk + 1) * LANES, jb * BLOCK:(jb + 3) * BLOCK]
        s = jnp.dot(jnp.concatenate(parts, axis=0), kwin, preferred_element_type=F32)
        scores.append(jnp.where(jnp.concatenate([valids[jb]] * 4, axis=0), s, NEG_INF))
        sinks.extend(jnp.full((BLOCK, 1), sink_ref[hk * 4 + g], F32) for g in range(4))
    s = jnp.concatenate(scores, axis=0)
    sk = jnp.concatenate(sinks, axis=0)
    m = jnp.maximum(jnp.max(s, axis=-1, keepdims=True), sk)
    p = jnp.exp(s - m).astype(BF16)
    sink_term = jnp.exp(sk - m)
    rows = []
    for ui, (jb, hk) in enumerate(units):
        vwin = jnp.concatenate([vfull[jb * BLOCK:(jb + 3) * BLOCK, hk * LANES:(hk + 1) * LANES], ones], axis=1)
        r0 = ui * 4 * BLOCK
        ov = jnp.dot(p[r0:r0 + 4 * BLOCK], vwin, preferred_element_type=F32)
        o = ov[:, 0:LANES] / (ov[:, LANES:2 * LANES] + sink_term[r0:r0 + 4 * BLOCK])
        pair = [jnp.where(lo, o[(2 * p2) * BLOCK:(2 * p2 + 1) * BLOCK],
                          o[(2 * p2 + 1) * BLOCK:(2 * p2 + 2) * BLOCK]) for p2 in range(2)]
        rows.append(jnp.concatenate(pair, axis=1))
    n_h = N_KV_HEADS
    return jnp.concatenate(
        [jnp.concatenate(rows[jb * n_h:(jb + 1) * n_h], axis=1) for jb in range(MT // BLOCK)], axis=0)


def _mixer_kernel(seqs, sink_ref, xp_ref, xs_ref, mod_ref, cos_ref, sin_ref,
                  ktp_ref, ktc_ref, ktn_ref, vp_ref, vc_ref, vn_ref,
                  wmix_ref, bmix_ref, wsp_ref, bspt_ref, sg_ref, sb_ref,
                  wba_ref, wbs_ref, wout_ref, l1g_ref, l1b_ref, wr_ref, br_ref,
                  x1_ref, h2_ref, logit_ref):
    i = pl.program_id(0)
    seq_len, pos0, _ = _tile_pos(seqs, i * MT)
    mod = mod_ref[0]
    sh1, sc1, g1 = (mod[:, j * D_MODEL:(j + 1) * D_MODEL] for j in range(3))
    sh2, sc2 = (mod[:, j * D_MODEL:(j + 1) * D_MODEL] for j in range(3, 5))
    x = jnp.where(i * MT >= seqs.n_prompt, xs_ref[...], xp_ref[...])
    h = (x * (1.0 + sc1) + sh1).astype(BF16)
    kv_end = ATT_W + 2 * KV_W
    zq = jnp.dot(h, wmix_ref[:, 0:ATT_W], preferred_element_type=F32) + bmix_ref[:, 0:ATT_W]
    z = jnp.dot(h, wmix_ref[:, kv_end:], preferred_element_type=F32) + bmix_ref[:, kv_end:]
    q = _rope(zq, cos_ref[...], sin_ref[...]) * (HEAD_DIM ** -0.5)
    u = jax.nn.gelu(z[:, 0:SGU_W])
    vs = _layer_norm(jax.nn.gelu(z[:, SGU_W:2 * SGU_W]), sg_ref[...], sb_ref[...])
    ga = z[:, 2 * SGU_W:2 * SGU_W + D_MODEL]
    gs = z[:, 2 * SGU_W + D_MODEL:]

    kfull = jnp.concatenate([ktp_ref[...], ktc_ref[...], ktn_ref[...]], axis=1)
    vfull = jnp.concatenate([vp_ref[...], vc_ref[...], vn_ref[...]], axis=0)
    qi = lax.broadcasted_iota(I32, (BLOCK, 3 * BLOCK), 0)
    ki = lax.broadcasted_iota(I32, (BLOCK, 3 * BLOCK), 1)
    band = (ki >= qi) & (ki <= qi + 2 * BLOCK)
    vs_b = vs.astype(BF16)
    valids, sgu_rows = [], []
    for jb in range(MT // BLOCK):
        posb = pos0 + jb * BLOCK
        valids.append(band & (ki >= jnp.where(posb == 0, BLOCK, 0))
                      & (ki < jnp.where(posb + BLOCK == seq_len, 2 * BLOCK, 3 * BLOCK)))
        groups = []
        for g in range(SGU_GROUPS):
            vg = vs_b[jb * BLOCK:(jb + 1) * BLOCK, g * LANES:(g + 1) * LANES]
            sv = jnp.dot(wsp_ref[g], vg, preferred_element_type=F32) + bspt_ref[:, g:g + 1]
            groups.append(sv)
        sgu_rows.append(jnp.concatenate(groups, axis=1))
    attn = _attention(q, kfull, vfull, valids, sink_ref)
    sgu = u * jnp.concatenate(sgu_rows, axis=0)

    a1 = jnp.dot(attn.astype(BF16), wba_ref[...], preferred_element_type=F32)
    a2 = jnp.dot(sgu.astype(BF16), wbs_ref[...], preferred_element_type=F32)
    merged = jax.nn.sigmoid(ga) * a1 + jax.nn.sigmoid(gs) * a2
    mix = jnp.dot(merged.astype(BF16), wout_ref[...], preferred_element_type=F32)
    x1 = _layer_norm(DN_ALPHA * x + g1 * mix, l1g_ref[...], l1b_ref[...])
    x1_ref[...] = x1
    h2 = x1 * (1.0 + sc2) + sh2
    hi = h2.astype(BF16)
    h2_ref[...] = hi

    lo_part = (h2 - hi.astype(F32)).astype(BF16)
    l1 = jnp.dot(hi, wr_ref[...], preferred_element_type=F32)
    l2 = jnp.dot(lo_part, wr_ref[:, 0:N_EXPERTS], preferred_element_type=F32)
    logit_ref[...] = l1[:, 0:N_EXPERTS] + l1[:, N_EXPERTS:2 * N_EXPERTS] + l2 + br_ref[...]


PACK = 4096.0
assert DISP_ROWS <= PACK and COMB_ROWS <= PACK and DISP_ROWS * PACK < 2 ** 24


def _router_kernel(l_ref, rcol_ref, rrow_ref, cnt_ref):
    rt = l_ref.shape[0]
    ns = rt // TM
    padded = jnp.concatenate([l_ref[...], jnp.zeros((rt, LANES - N_EXPERTS), F32)], axis=1)
    work = padded.T[0:N_EXPERTS]
    eidx = lax.broadcasted_iota(I32, (N_EXPERTS, rt), 0)
    idxs, vals = [], []
    for _ in range(TOP_K):
        m = jnp.max(work, axis=0, keepdims=True)
        ix = jnp.min(jnp.where(work == m, eidx, N_EXPERTS), axis=0, keepdims=True)
        idxs.append(ix)
        vals.append(m)
        work = jnp.where(eidx == ix, NEG_INF, work)
    exps = [jnp.exp(v - vals[0]) for v in vals]
    esum = exps[0] + exps[1] + exps[2] + exps[3]
    wts = [e / esum for e in exps]

    sel = jnp.zeros((N_EXPERTS, rt), F32)
    for ix in idxs:
        sel = sel + jnp.where(eidx == ix, 1.0, 0.0)
    sel_b = sel.astype(BF16)
    ti = lax.broadcasted_iota(I32, (TM, TM), 0)
    tj = lax.broadcasted_iota(I32, (TM, TM), 1)
    earlier = jnp.where(ti < tj, 1.0, 0.0).astype(BF16)
    tiles = [slice(s * TM, (s + 1) * TM) for s in range(ns)]
    ranks = [jnp.dot(sel_b[:, t], earlier, preferred_element_type=F32) for t in tiles]
    cnts, nchs = [], []
    for t in tiles:
        cnt = jnp.sum(sel[:, t], axis=1, keepdims=True)
        cnt = cnt + (cnt - 2.0 * jnp.floor(cnt * 0.5))
        cnts.append(cnt)
        nchs.append(jnp.maximum(jnp.floor((cnt + (COMB_CHUNK - 0.5)) * (1.0 / COMB_CHUNK)), 1.0))
    lane = lax.broadcasted_iota(I32, (N_EXPERTS, LANES), 1)
    pre = jnp.zeros((N_EXPERTS, LANES), F32)
    for j, col in enumerate(cnts + nchs):
        pre = jnp.where(lane == j, col, pre)
    ei = lax.broadcasted_iota(I32, (N_EXPERTS, N_EXPERTS), 0)
    ej = lax.broadcasted_iota(I32, (N_EXPERTS, N_EXPERTS), 1)
    before = jnp.where(ej < ei, 1.0, 0.0).astype(BF16)
    base = jnp.dot(before, pre.astype(BF16), preferred_element_type=F32)
    both = jnp.concatenate(
        [(ranks[s] + base[:, s:s + 1]) * PACK + (ranks[s] + base[:, ns + s:ns + s + 1] * COMB_CHUNK)
         for s in range(ns)], axis=1)

    packed = [jnp.sum(jnp.where(eidx == ix, both, 0.0), axis=0, keepdims=True) for ix in idxs]
    drow = [jnp.floor(v * (1.0 / PACK)) for v in packed]
    table = [ix.astype(F32) for ix in idxs] + drow + [v - d * PACK for v, d in zip(packed, drow)] + wts
    sub = lax.broadcasted_iota(I32, (LANES, rt), 0)
    rr = jnp.zeros((LANES, rt), F32)
    for j, row in enumerate(table):
        rr = jnp.where(sub == j, row, rr)
    rcol_ref[...] = rr.T
    for s in range(ns):
        rrow_ref[s] = rr[0:16, tiles[s]]
        cnt_ref[s] = jnp.broadcast_to(cnts[s], (N_EXPERTS, LANES)).astype(I32)


def _router(logits):
    T = logits.shape[0]
    rt = math.gcd(T, ROUTER_TOKENS)
    nt = T // TM
    assert 2 * (rt // TM) <= LANES
    return pl.pallas_call(
        _router_kernel,
        grid=(T // rt,),
        in_specs=[pl.BlockSpec((rt, N_EXPERTS), lambda i: (i, 0))],
        out_specs=[pl.BlockSpec((rt, LANES), lambda i: (i, 0)),
                   pl.BlockSpec((rt // TM, 16, TM), lambda i: (i, 0, 0)),
                   pl.BlockSpec((rt // TM, N_EXPERTS, LANES), lambda i: (i, 0, 0))],
        out_shape=[jax.ShapeDtypeStruct((T, LANES), F32),
                   jax.ShapeDtypeStruct((nt, 16, TM), F32),
                   jax.ShapeDtypeStruct((nt, N_EXPERTS, LANES), I32)],
        compiler_params=pltpu.CompilerParams(dimension_semantics=("arbitrary",),
                                             vmem_limit_bytes=VMEM_LIMIT_BYTES),
        name="router",
    )(logits)


def _mixer(seqs, sink, xp, xs, mod3, cos_t, sin_t, kt, v, p):
    T = seqs.n_tokens
    nt = T // MT
    nb = T // BLOCK
    r = MT // BLOCK

    def mod_map(i, s):
        return (_tile_pos(seqs, i * MT)[2], 0, 0)

    def rope_map(i, s):
        return (_tile_pos(seqs, i * MT)[1] // MT, 0)

    const2 = lambda i, s: (0, 0)
    once = dict(pipeline_mode=pl.Buffered(1))
    in_specs = _group_x_specs(seqs, MT) + [
        pl.BlockSpec((1, 1, 6 * D_MODEL), mod_map),
        pl.BlockSpec((MT, LANES), rope_map),
        pl.BlockSpec((MT, LANES), rope_map),
        pl.BlockSpec((2 * LANES, BLOCK), lambda i, s: (0, jnp.maximum(i * r - 1, 0))),
        pl.BlockSpec((2 * LANES, MT), lambda i, s: (0, i)),
        pl.BlockSpec((2 * LANES, BLOCK), lambda i, s: (0, jnp.minimum(i * r + r, nb - 1))),
        pl.BlockSpec((BLOCK, 2 * LANES), lambda i, s: (jnp.maximum(i * r - 1, 0), 0)),
        pl.BlockSpec((MT, 2 * LANES), lambda i, s: (i, 0)),
        pl.BlockSpec((BLOCK, 2 * LANES), lambda i, s: (jnp.minimum(i * r + r, nb - 1), 0)),
        pl.BlockSpec(p["w_mix"].shape, const2, **once),
        pl.BlockSpec(p["b_mix"].shape, const2, **once),
        pl.BlockSpec(p["w_sp"].shape, lambda i, s: (0, 0, 0), **once),
        pl.BlockSpec(p["b_spt"].shape, const2, **once),
        pl.BlockSpec(p["sgu_g"].shape, const2, **once),
        pl.BlockSpec(p["sgu_b"].shape, const2, **once),
        pl.BlockSpec(p["w_ba"].shape, const2, **once),
        pl.BlockSpec(p["w_bs"].shape, const2, **once),
        pl.BlockSpec(p["w_out"].shape, const2, **once),
        pl.BlockSpec(p["ln1_g"].shape, const2, **once),
        pl.BlockSpec(p["ln1_b"].shape, const2, **once),
        pl.BlockSpec(p["w_r"].shape, const2, **once),
        pl.BlockSpec(p["b_r"].shape, const2, **once),
    ]
    out_specs = [
        pl.BlockSpec((MT, D_MODEL), lambda i, s: (i, 0)),
        pl.BlockSpec((MT, D_MODEL), lambda i, s: (i, 0)),
        pl.BlockSpec((MT, N_EXPERTS), lambda i, s: (i, 0)),
    ]
    out_shape = [
        jax.ShapeDtypeStruct((T, D_MODEL), F32),
        jax.ShapeDtypeStruct((T, D_MODEL), BF16),
        jax.ShapeDtypeStruct((T, N_EXPERTS), F32),
    ]
    return pl.pallas_call(
        functools.partial(_mixer_kernel, seqs),
        grid_spec=pltpu.PrefetchScalarGridSpec(
            num_scalar_prefetch=1, grid=(nt,), in_specs=in_specs, out_specs=out_specs),
        out_shape=out_shape,
        compiler_params=pltpu.CompilerParams(dimension_semantics=("arbitrary",),
                                             vmem_limit_bytes=VMEM_LIMIT_BYTES),
        name="mixer",
    )(sink, xp, xs, mod3, cos_t, sin_t, kt, kt, kt, v, v, v,
      p["w_mix"], p["b_mix"], p["w_sp"], p["b_spt"], p["sgu_g"], p["sgu_b"],
      p["w_ba"], p["w_bs"], p["w_out"], p["ln1_g"], p["ln1_b"], p["w_r"], p["b_r"])


def _pair_rows(row):
    return pl.multiple_of((row // PAIR) * ROW_TILES, ROW_TILES)


def _to_row_tiles(dst_ref, rows, n, start=0):
    words = pltpu.bitcast(rows, U32)
    base = start // PAIR * ROW_TILES
    for c in range(ROW_TILES):
        dst_ref[pl.ds(base + c, n // PAIR, stride=ROW_TILES), :] = words[:, c * LANES:(c + 1) * LANES]


def _from_row_tiles(src_ref, start, n):
    base = start // PAIR * ROW_TILES
    words = jnp.concatenate(
        [src_ref[pl.ds(base + c, n // PAIR, stride=ROW_TILES), :] for c in range(ROW_TILES)], axis=1)
    return pltpu.bitcast(words, BF16)


WAIT_GROUP = 16


def _issue_tile(copy, k0, src_ref, dst_ref, extra_ref, n_extra, step):
    for e in range(N_EXPERTS):
        copy(src_ref[k0 + e], dst_ref[k0 + e]).start()

    @pl.when(n_extra > 0)
    def _():
        def per_expert(e, c):
            def per_chunk(j, c2):
                copy(src_ref[k0 + e] + j * step, dst_ref[k0 + e] + j * step).start()
                return c2
            lax.fori_loop(1, extra_ref[k0 + e] + 1, per_chunk, 0)
            return c
        lax.fori_loop(0, N_EXPERTS, per_expert, 0)


def _wait_tile(group_copy, chunk_copy, n_extra):
    for _ in range(N_EXPERTS // WAIT_GROUP):
        group_copy.wait()

    def body(_, c):
        chunk_copy.wait()
        return c
    lax.fori_loop(0, n_extra, body, 0)


def _dispatch_kernel(b_ref, s_ref, x_ref, nx_ref, tail_ref, h2_ref, rrow_ref, xin_ref, stg_ref, zero_ref, sem):
    i = pl.program_id(0)
    nt = pl.num_programs(0)
    slot = i % 2
    rr = rrow_ref[0]
    pos = [rr[TOP_K + k:TOP_K + k + 1].astype(I32) for k in range(TOP_K)]
    for a in range(DISP_ROWS // DISP_BLOCK):
        rho = lax.broadcasted_iota(I32, (DISP_BLOCK, TM), 0) + a * DISP_BLOCK
        pt = jnp.zeros((DISP_BLOCK, TM), F32)
        for k in range(TOP_K):
            pt = pt + jnp.where(rho == pos[k], 1.0, 0.0)
        rows = jnp.dot(pt.astype(BF16), h2_ref[...], preferred_element_type=F32).astype(BF16)
        _to_row_tiles(stg_ref.at[slot], rows, DISP_BLOCK, a * DISP_BLOCK)

    chunk = DISP_CHUNK // PAIR * ROW_TILES

    def copy(src_row, dst_row, sl):
        return pltpu.make_async_copy(stg_ref.at[sl, pl.ds(_pair_rows(src_row), chunk)],
                                     xin_ref.at[pl.ds(_pair_rows(dst_row), chunk)], sem)

    def wait_tile(tile):
        group = pltpu.make_async_copy(stg_ref.at[0, pl.ds(0, WAIT_GROUP * chunk)],
                                      xin_ref.at[pl.ds(0, WAIT_GROUP * chunk)], sem)
        _wait_tile(group, copy(0, 0, 0), nx_ref[tile])

    @pl.when(i > 0)
    def _():
        wait_tile(i - 1)
    _issue_tile(lambda src, dst: copy(src, dst, slot), i * N_EXPERTS, b_ref, s_ref, x_ref, nx_ref[i], DISP_CHUNK)

    @pl.when(i == nt - 1)
    def _():
        wait_tile(i)
        zero_ref[...] = jnp.zeros_like(zero_ref)

        def zcopy(dst_row):
            return pltpu.make_async_copy(zero_ref, xin_ref.at[pl.ds(_pair_rows(dst_row), chunk)], sem)

        def zwait(count):
            def body(_, c):
                zcopy(0).wait()
                return c
            lax.fori_loop(0, count, body, 0)

        def per_expert(e, total):
            lo = tail_ref[e]
            nz = (tail_ref[N_EXPERTS + e] - lo) // DISP_CHUNK

            def per_chunk(j, c):
                zcopy(lo + j * DISP_CHUNK).start()
                return c
            lax.fori_loop(0, nz, per_chunk, 0)
            return total + nz
        zwait(lax.fori_loop(0, N_EXPERTS, per_expert, 0))

        def last_chunk(e, c):
            zcopy(tail_ref[N_EXPERTS + e] - DISP_CHUNK).start()
            return c
        lax.fori_loop(0, N_EXPERTS, last_chunk, 0)
        zwait(N_EXPERTS)


def _dispatch(seqs, b_te, s_te, x_te, nx_t, tails, h2, rrow, n_rows):
    nt = seqs.n_tokens // TM
    return pl.pallas_call(
        _dispatch_kernel,
        grid_spec=pltpu.PrefetchScalarGridSpec(
            num_scalar_prefetch=5, grid=(nt,),
            in_specs=[pl.BlockSpec((TM, D_MODEL), lambda i, *_: (i, 0)),
                      pl.BlockSpec((1, 16, TM), lambda i, *_: (i, 0, 0))],
            out_specs=pl.BlockSpec(memory_space=pl.ANY),
            scratch_shapes=[pltpu.VMEM((2, DISP_ROWS // PAIR * ROW_TILES, LANES), U32),
                            pltpu.VMEM((DISP_CHUNK // PAIR * ROW_TILES, LANES), U32),
                            pltpu.SemaphoreType.DMA]),
        out_shape=jax.ShapeDtypeStruct((n_rows // PAIR * ROW_TILES, LANES), U32),
        compiler_params=pltpu.CompilerParams(dimension_semantics=("arbitrary",),
                                             vmem_limit_bytes=VMEM_LIMIT_BYTES),
        name="dispatch",
    )(b_te, s_te, x_te, nx_t, tails, h2, rrow)


def _expert_kernel(be_ref, bv_ref, x_ref, wup_hbm, bup_ref, wdn_hbm, bdn_ref, o_ref,
                   wup_f, wdn_f, wup_b, wdn_b, sem):
    i = pl.program_id(0)
    e = be_ref[i]
    valid = bv_ref[i]
    first = jnp.logical_or(i == 0, e != be_ref[jnp.maximum(i - 1, 0)])

    def fetch(ex):
        slot = ex % 2
        return (pltpu.make_async_copy(wup_hbm.at[ex], wup_f.at[slot], sem.at[0, slot]),
                pltpu.make_async_copy(wdn_hbm.at[ex], wdn_f.at[slot], sem.at[1, slot]))

    @pl.when(i == 0)
    def _():
        for d in fetch(e):
            d.start()

    @pl.when(first)
    def _():
        for d in fetch(e):
            d.wait()

        @pl.when(e + 1 < N_EXPERTS)
        def _():
            for d in fetch(e + 1):
                d.start()

        @pl.when(valid > 0)
        def _():
            wup_b[...] = wup_f[e % 2].astype(BF16)
            wdn_b[...] = wdn_f[e % 2].astype(BF16)

    def ffn(n):
        x = _from_row_tiles(x_ref, 0, n)
        row = lax.broadcasted_iota(I32, (n, 1), 0)
        xb = jnp.where(row < valid, x, jnp.zeros_like(x))
        hu = jnp.dot(xb, wup_b[...], preferred_element_type=F32) + bup_ref[0]
        gate = jnp.minimum(hu[:, 0:D_FF], SWIGLU_LIMIT)
        lin = jnp.clip(hu[:, D_FF:], -SWIGLU_LIMIT, SWIGLU_LIMIT)
        act = gate * jax.nn.sigmoid(SWIGLU_ALPHA * gate) * (lin + 1.0)
        y = jnp.dot(act.astype(BF16), wdn_b[...], preferred_element_type=F32) + bdn_ref[0]
        _to_row_tiles(o_ref, y.astype(BF16), n)

    half = BM // 2

    @pl.when(valid > half)
    def _():
        ffn(BM)

    @pl.when(jnp.logical_and(valid > 0, valid <= half))
    def _():
        ffn(half)
        o_ref[half // PAIR * ROW_TILES:, :] = jnp.zeros((half // PAIR * ROW_TILES, LANES), U32)

    @pl.when(valid == 0)
    def _():
        o_ref[...] = jnp.zeros_like(o_ref)


def _experts(blk_e, blk_v, xin, w_up, b_up, w_down, b_down, n_blk):
    return pl.pallas_call(
        _expert_kernel,
        grid_spec=pltpu.PrefetchScalarGridSpec(
            num_scalar_prefetch=2, grid=(n_blk,),
            in_specs=[pl.BlockSpec((BM // PAIR * ROW_TILES, LANES), lambda i, be, bv: (i, 0)),
                      pl.BlockSpec(memory_space=pl.ANY),
                      pl.BlockSpec((1, 1, 2 * D_FF), lambda i, be, bv: (be[i], 0, 0)),
                      pl.BlockSpec(memory_space=pl.ANY),
                      pl.BlockSpec((1, 1, D_MODEL), lambda i, be, bv: (be[i], 0, 0))],
            out_specs=pl.BlockSpec((BM // PAIR * ROW_TILES, LANES), lambda i, be, bv: (i, 0)),
            scratch_shapes=[pltpu.VMEM((2, D_MODEL, 2 * D_FF), F32),
                            pltpu.VMEM((2, D_FF, D_MODEL), F32),
                            pltpu.VMEM((D_MODEL, 2 * D_FF), BF16),
                            pltpu.VMEM((D_FF, D_MODEL), BF16),
                            pltpu.SemaphoreType.DMA((2, 2))]),
        out_shape=jax.ShapeDtypeStruct((n_blk * BM // PAIR * ROW_TILES, LANES), U32),
        compiler_params=pltpu.CompilerParams(dimension_semantics=("arbitrary",),
                                             vmem_limit_bytes=VMEM_LIMIT_BYTES),
        name="experts",
    )(blk_e, blk_v, xin, w_up, b_up.reshape(N_EXPERTS, 1, 2 * D_FF), w_down,
      b_down.reshape(N_EXPERTS, 1, D_MODEL))


COMB_KC = COMB_ROWS // 256
COMB_KC_MIN = N_EXPERTS * COMB_CHUNK // 256


def _combine_kernel(tile0, s_ref, b_ref, x_ref, nx_ref, rows_ref, x1_ref, rcol_ref, mod_ref, g_ref, bb_ref,
                    eo_ref, y_ref, stg_ref, acc_ref, sem):
    i = pl.program_id(0)
    nt = pl.num_programs(0)
    slot = i % 2
    chunk = COMB_CHUNK // PAIR * ROW_TILES

    def copy(src_row, dst_row, sl):
        return pltpu.make_async_copy(eo_ref.at[pl.ds(_pair_rows(src_row), chunk)],
                                     stg_ref.at[sl, pl.ds(_pair_rows(dst_row), chunk)], sem.at[sl])

    def issue(tile, sl):
        _issue_tile(lambda src, dst: copy(src, dst, sl), tile * N_EXPERTS, s_ref, b_ref, x_ref, nx_ref[tile],
                    COMB_CHUNK)

    @pl.when(i == 0)
    def _():
        stg_ref[...] = jnp.zeros_like(stg_ref)
        issue(tile0, 0)

    rc = rcol_ref[...]
    col_k = [rc[:, 2 * TOP_K + k:2 * TOP_K + k + 1].astype(I32) for k in range(TOP_K)]
    w_k = [rc[:, 3 * TOP_K + k:3 * TOP_K + k + 1] for k in range(TOP_K)]

    def weights(c):
        jl = lax.broadcasted_iota(I32, (TM, 256), 1) + c * 256
        pm = jnp.zeros((TM, 256), F32)
        for k in range(TOP_K):
            pm = pm + jnp.where(jl == col_k[k], w_k[k], 0.0)
        return pm.astype(BF16)

    @pl.when(i + 1 < nt)
    def _():
        issue(tile0 + i + 1, 1 - slot)

    stg = stg_ref.at[slot]
    group = pltpu.make_async_copy(eo_ref.at[pl.ds(0, WAIT_GROUP * chunk)],
                                  stg.at[pl.ds(0, WAIT_GROUP * chunk)], sem.at[slot])
    _wait_tile(group, copy(0, 0, slot), nx_ref[tile0 + i])

    pm = jnp.concatenate([weights(c) for c in range(COMB_KC_MIN)], axis=1)
    acc_ref[...] = jnp.dot(pm, _from_row_tiles(stg, 0, COMB_KC_MIN * 256), preferred_element_type=F32)
    used = (rows_ref[tile0 + i] + 255) // 256
    for c in range(COMB_KC_MIN, COMB_KC):
        @pl.when(c < used)
        def _():
            acc_ref[...] += jnp.dot(weights(c), _from_row_tiles(stg, c * 256, 256), preferred_element_type=F32)

    mod = mod_ref[0]
    g2 = mod[:, 5 * D_MODEL:6 * D_MODEL]
    y_ref[...] = _layer_norm(DN_ALPHA * x1_ref[...] + g2 * acc_ref[...], g_ref[...], bb_ref[...])


def _combine(seqs, tile0, n_tiles, s_te, b_te, x_te, nx_t, rows_t, x1, rcol, mod3, ln2_g, ln2_b, eo):
    def mod_map(i, *_):
        return (_tile_pos(seqs, (i + tile0) * TM)[2], 0, 0)

    return pl.pallas_call(
        functools.partial(_combine_kernel, tile0),
        grid_spec=pltpu.PrefetchScalarGridSpec(
            num_scalar_prefetch=5, grid=(n_tiles,),
            in_specs=[pl.BlockSpec((TM, D_MODEL), lambda i, *_: (i + tile0, 0)),
                      pl.BlockSpec((TM, LANES), lambda i, *_: (i + tile0, 0)),
                      pl.BlockSpec((1, 1, 6 * D_MODEL), mod_map),
                      pl.BlockSpec((1, D_MODEL), lambda i, *_: (0, 0)),
                      pl.BlockSpec((1, D_MODEL), lambda i, *_: (0, 0)),
                      pl.BlockSpec(memory_space=pl.ANY)],
            out_specs=pl.BlockSpec((TM, D_MODEL), lambda i, *_: (i, 0)),
            scratch_shapes=[pltpu.VMEM((2, COMB_ROWS // PAIR * ROW_TILES, LANES), U32),
                            pltpu.VMEM((TM, D_MODEL), F32),
                            pltpu.SemaphoreType.DMA((2,))]),
        out_shape=jax.ShapeDtypeStruct((n_tiles * TM, D_MODEL), F32),
        compiler_params=pltpu.CompilerParams(dimension_semantics=("arbitrary",),
                                             vmem_limit_bytes=VMEM_LIMIT_BYTES),
        name="combine",
    )(s_te, b_te, x_te, nx_t, rows_t, x1, rcol, mod3, ln2_g, ln2_b, eo)


def _prep_params(w_in, b_in, w_spatial, b_spatial, sgu_ln_g, sgu_ln_b, w_br_attn, w_br_sgu, w_out,
                 ln1_g, ln1_b, w_router, b_router):
    q_end, k_end, v_end = ATT_W, ATT_W + KV_W, ATT_W + 2 * KV_W

    def dup(w, lo):
        h0, h1 = w[..., lo:lo + HEAD_DIM], w[..., lo + HEAD_DIM:lo + 2 * HEAD_DIM]
        return jnp.concatenate([h0, h0, h1, h1], axis=-1)

    w_kv = jnp.concatenate([dup(w_in, q_end), dup(w_in, k_end)], axis=1).astype(BF16)
    b_kv = jnp.concatenate([dup(b_in, q_end), dup(b_in, k_end)], axis=0).reshape(1, -1)
    w_mix = w_in.astype(BF16)
    b_mix = b_in.reshape(1, -1)
    w_hi = w_router.astype(BF16)
    w_lo = (w_router - w_hi.astype(F32)).astype(BF16)
    p = dict(
        w_mix=w_mix, b_mix=b_mix,
        w_sp=w_spatial.astype(BF16), b_spt=b_spatial.T,
        sgu_g=sgu_ln_g.reshape(1, -1), sgu_b=sgu_ln_b.reshape(1, -1),
        w_ba=w_br_attn.astype(BF16), w_bs=w_br_sgu.astype(BF16), w_out=w_out.astype(BF16),
        ln1_g=ln1_g.reshape(1, -1), ln1_b=ln1_b.reshape(1, -1),
        w_r=jnp.concatenate([w_hi, w_lo], axis=1), b_r=b_router.reshape(1, -1),
    )
    return w_kv, b_kv, p


def _layer(seqs, xp, xs, c_all, w_ada, b_ada, w_in, b_in, sink, sgu_ln_g, sgu_ln_b, w_spatial, b_spatial,
           w_br_attn, w_br_sgu, w_out, ln1_g, ln1_b, w_router, b_router, w_up, b_up, w_down, b_down,
           ln2_g, ln2_b):
    T = seqs.n_tokens
    nt = T // TM
    ntp = seqs.n_prompt // TM
    c_pad = jnp.zeros((8, D_MODEL), F32).at[:c_all.shape[0]].set(c_all)
    mod3 = _ada(c_pad, w_ada, b_ada).reshape(8, 1, 6 * D_MODEL)
    cos_t, sin_t = _rope_tables(max(seqs.prompt_len, seqs.sample_len))
    w_kv, b_kv, p = _prep_params(w_in, b_in, w_spatial, b_spatial, sgu_ln_g, sgu_ln_b, w_br_attn,
                                 w_br_sgu, w_out, ln1_g, ln1_b, w_router, b_router)
    kt, v = _kv(seqs, xp, xs, mod3, cos_t, sin_t, w_kv, b_kv)
    x1, h2, logits = _mixer(seqs, sink, xp, xs, mod3, cos_t, sin_t, kt, v, p)
    rcol, rrow, cnt3 = _router(logits)

    cnt = cnt3[:, :, 0]
    count = cnt.sum(0)
    reg = (count + REGION_SLACK + BM - 1) // BM * BM
    pad_end = jnp.cumsum(reg)
    pad_start = pad_end - reg
    flat = lambda a: a.reshape(-1).astype(I32)
    s_te = flat(pad_start[None, :] + jnp.cumsum(cnt, axis=0) - cnt)
    d_nch = jnp.maximum((cnt + DISP_CHUNK - 1) // DISP_CHUNK, 1)
    c_nch = jnp.maximum((cnt + COMB_CHUNK - 1) // COMB_CHUNK, 1)
    db_te = flat(jnp.cumsum(cnt, axis=1) - cnt)
    cb_te = flat(COMB_CHUNK * (jnp.cumsum(c_nch, axis=1) - c_nch))
    rows_t = flat(COMB_CHUNK * c_nch.sum(1))
    n_blk = (nt * TILE_ROWS + N_EXPERTS * (REGION_SLACK + BM - 1)) // BM + 1
    tails = jnp.concatenate([pad_start + count, pad_end[:-1], jnp.array([n_blk * BM])]).astype(I32)
    blk_start = jnp.arange(n_blk, dtype=I32) * BM
    blk_e = jnp.minimum((blk_start[:, None] >= pad_end[None, :]).sum(1), N_EXPERTS - 1).astype(I32)
    owner = blk_e[:, None] == jnp.arange(N_EXPERTS)[None, :]
    blk_rows = (owner * (pad_start + count)[None, :]).sum(1) - blk_start
    blk_v = jnp.where(blk_start < pad_end[-1], jnp.clip(blk_rows, 0, BM), 0).astype(I32)

    xin = _dispatch(seqs, db_te, s_te, flat(d_nch - 1), flat((d_nch - 1).sum(1)), tails, h2, rrow, n_blk * BM)
    eo = _experts(blk_e, blk_v, xin, w_up, b_up, w_down, b_down, n_blk)
    comb = (s_te, cb_te, flat(c_nch - 1), flat((c_nch - 1).sum(1)), rows_t,
            x1, rcol, mod3, ln2_g.reshape(1, -1), ln2_b.reshape(1, -1), eo)
    return _combine(seqs, 0, ntp, *comb), _combine(seqs, ntp, nt - ntp, *comb)


def kernel(x_prompt, x_sample, c_prompt, c_sample, w_ada, b_ada, w_in, b_in, sink, sgu_ln_g, sgu_ln_b, w_spatial, b_spatial, w_br_attn, w_br_sgu, w_out, ln1_g, ln1_b, w_router, b_router, w_up, b_up, w_down, b_down, ln2_g, ln2_b):
    assert w_ada.shape[0] == DEPTH == 1
    bp, sp, d = x_prompt.shape
    bs, ss, _ = x_sample.shape
    seqs = _Seqs(n_prompt=bp * sp, prompt_len=sp, sample_len=ss, n_tokens=bp * sp + bs * ss)
    c_all = jnp.concatenate([c_prompt, c_sample], axis=0)
    yp, ys = _layer(seqs, x_prompt.reshape(bp * sp, d), x_sample.reshape(bs * ss, d), c_all,
                    w_ada[0], b_ada[0], w_in[0], b_in[0], sink[0], sgu_ln_g[0], sgu_ln_b[0],
                    w_spatial[0], b_spatial[0], w_br_attn[0], w_br_sgu[0], w_out[0], ln1_g[0], ln1_b[0],
                    w_router[0], b_router[0], w_up[0], b_up[0], w_down[0], b_down[0], ln2_g[0], ln2_b[0])
    return (yp.reshape(bp, sp, d), ys.reshape(bs, ss, d))
```

```python
import functools
import math
from typing import NamedTuple

import jax
import jax.numpy as jnp
from jax import lax
from jax.experimental import pallas as pl
from jax.experimental.pallas import tpu as pltpu

F32 = jnp.float32
BF16 = jnp.bfloat16
I32 = jnp.int32

D_MODEL = 1024
N_HEADS = 8
N_KV_HEADS = 2
HEAD_DIM = 64
ATT_W = N_HEADS * HEAD_DIM
KV_W = N_KV_HEADS * HEAD_DIM
BLOCK = 128
ROPE_THETA = 500000.0
ROT_DIM = HEAD_DIM // 4
ROT_HALF = ROT_DIM // 2
SGU_W = D_MODEL // 2
SGU_GROUPS = 4
N_EXPERTS = 32
TOP_K = 4
D_FF = D_MODEL
SWIGLU_LIMIT = 7.0
SWIGLU_ALPHA = 1.702
LN_EPS = 1e-5
DEPTH = 1
DN_ALPHA = (2 * DEPTH) ** 0.25

LANES = 128
SUBLANES = 8
ROW_TILES = D_MODEL // LANES
VMEM_LIMIT_BYTES = 56 * 1024 * 1024
U32 = jnp.uint32
PAIR = 2

TM = 256
MT = 512
TK = 1024
BM = 512
EXPERT_SUBS = 2
ROUTER_TOKENS = 2048
DISP_CHUNK = 48
COMB_CHUNK = 48
REGION_SLACK = DISP_CHUNK
TILE_ROWS = TM * TOP_K + N_EXPERTS
DISP_ROWS = TILE_ROWS + DISP_CHUNK
DISP_BLOCK = DISP_ROWS // 3
assert DISP_BLOCK * 3 == DISP_ROWS and DISP_BLOCK % 16 == 0
COMB_ROWS = (TILE_ROWS + N_EXPERTS * COMB_CHUNK + 255) // 256 * 256
NEG_INF = float("-inf")


class _Seqs(NamedTuple):
    n_prompt: int
    prompt_len: int
    sample_len: int
    n_tokens: int


def _tile_pos(seqs, t0):
    is_s = t0 >= seqs.n_prompt
    seq_len = jnp.where(is_s, seqs.sample_len, seqs.prompt_len)
    off = jnp.where(is_s, t0 - seqs.n_prompt, t0)
    pos0 = off % seq_len
    row = jnp.where(is_s, seqs.n_prompt // seqs.prompt_len + off // seq_len, off // seq_len)
    return seq_len, pos0, row


def _ada_kernel(c_ref, w_ref, b_ref, o_ref):
    c = c_ref[...]
    a = c * jax.nn.sigmoid(c)
    o_ref[...] = jnp.dot(a, w_ref[...], preferred_element_type=F32,
                         precision=lax.Precision.HIGHEST) + b_ref[...]


def _ada(c_pad, w_ada, b_ada):
    n = w_ada.shape[1]
    bn = 1536
    return pl.pallas_call(
        _ada_kernel,
        grid=(n // bn,),
        in_specs=[pl.BlockSpec((8, D_MODEL), lambda j: (0, 0)),
                  pl.BlockSpec((D_MODEL, bn), lambda j: (0, j)),
                  pl.BlockSpec((1, bn), lambda j: (0, j))],
        out_specs=pl.BlockSpec((8, bn), lambda j: (0, j)),
        out_shape=jax.ShapeDtypeStruct((8, n), F32),
        compiler_params=pltpu.CompilerParams(vmem_limit_bytes=VMEM_LIMIT_BYTES),
        name="ada",
    )(c_pad, w_ada, b_ada.reshape(1, n))


def _rope_tables(length):
    inv = ROPE_THETA ** (-jnp.arange(ROT_HALF, dtype=F32) * 2.0 / ROT_DIM)
    ang = jnp.arange(length, dtype=F32)[:, None] * inv[None, :]
    lane = jnp.arange(LANES) % HEAD_DIM
    spread = ((lane[None, :] % ROT_HALF == jnp.arange(ROT_HALF)[:, None]) & (lane[None, :] < ROT_DIM)).astype(F32)
    sign = jnp.where(lane < ROT_HALF, -1.0, 1.0)
    exact = dict(precision=lax.Precision.HIGHEST, preferred_element_type=F32)
    cos = jnp.dot(jnp.cos(ang), spread, **exact) + (lane >= ROT_DIM).astype(F32)[None, :]
    sin = jnp.dot(jnp.sin(ang), spread * sign[None, :], **exact)
    return cos, sin


def _rope(x, cos, sin):
    n = x.shape[1]
    reps = n // LANES
    c = jnp.concatenate([cos] * reps, axis=1)
    s = jnp.concatenate([sin] * reps, axis=1)
    lane = lax.broadcasted_iota(I32, x.shape, 1)
    first = (lane & (HEAD_DIM - 1)) < ROT_HALF
    partner = jnp.where(first, pltpu.roll(x, n - ROT_HALF, 1), pltpu.roll(x, ROT_HALF, 1))
    return x * c + partner * s


def _kv_kernel(seqs, xp_ref, xs_ref, mod_ref, cos_ref, sin_ref, w_ref, b_ref, kt_ref, v_ref):
    mod = mod_ref[0]
    sh1 = mod[:, 0:D_MODEL]
    sc1 = mod[:, D_MODEL:2 * D_MODEL]
    x = jnp.where(pl.program_id(0) * TK >= seqs.n_prompt, xs_ref[...], xp_ref[...])
    h = (x * (1.0 + sc1) + sh1).astype(BF16)
    kv = jnp.dot(h, w_ref[...], preferred_element_type=F32) + b_ref[...]
    k = _rope(kv[:, 0:2 * LANES], cos_ref[...], sin_ref[...])
    kt_ref[...] = k.T.astype(BF16)
    v_ref[...] = kv[:, 2 * LANES:4 * LANES].astype(BF16)


def _group_x_specs(seqs, tile):
    ntp = seqs.n_prompt // tile
    return [pl.BlockSpec((tile, D_MODEL), lambda i, *_: (jnp.minimum(i, ntp - 1), 0)),
            pl.BlockSpec((tile, D_MODEL), lambda i, *_: (jnp.maximum(i - ntp, 0), 0))]


def _kv(seqs, xp, xs, mod3, cos_t, sin_t, w_kv, b_kv):
    T = seqs.n_tokens

    def mod_map(i):
        return (_tile_pos(seqs, i * TK)[2], 0, 0)

    def rope_map(i):
        return (_tile_pos(seqs, i * TK)[1] // TK, 0)

    return pl.pallas_call(
        functools.partial(_kv_kernel, seqs),
        grid=(T // TK,),
        in_specs=_group_x_specs(seqs, TK) + [
                  pl.BlockSpec((1, 1, 6 * D_MODEL), mod_map),
                  pl.BlockSpec((TK, LANES), rope_map),
                  pl.BlockSpec((TK, LANES), rope_map),
                  pl.BlockSpec((D_MODEL, 4 * LANES), lambda i: (0, 0)),
                  pl.BlockSpec((1, 4 * LANES), lambda i: (0, 0))],
        out_specs=[pl.BlockSpec((2 * LANES, TK), lambda i: (0, i)),
                   pl.BlockSpec((TK, 2 * LANES), lambda i: (i, 0))],
        out_shape=[jax.ShapeDtypeStruct((2 * LANES, T), BF16),
                   jax.ShapeDtypeStruct((T, 2 * LANES), BF16)],
        compiler_params=pltpu.CompilerParams(dimension_semantics=("arbitrary",),
                                             vmem_limit_bytes=VMEM_LIMIT_BYTES),
        name="kv",
    )(xp, xs, mod3, cos_t, sin_t, w_kv, b_kv)


def _layer_norm(x, g, b):
    mu = jnp.mean(x, axis=-1, keepdims=True)
    xc = x - mu
    var = jnp.mean(xc * xc, axis=-1, keepdims=True)
    return xc * lax.rsqrt(var + LN_EPS) * g + b


def _attention(q, kfull, vfull, valids, sink_ref):
    lane = lax.broadcasted_iota(I32, (BLOCK, LANES), 1)
    lo = lane < HEAD_DIM
    ones = jnp.ones((3 * BLOCK, LANES), BF16)
    units = [(jb, hk) for jb in range(MT // BLOCK) for hk in range(N_KV_HEADS)]
    scores, sinks = [], []
    for jb, hk in units:
        parts = []
        for p in range(2):
            qp = q[jb * BLOCK:(jb + 1) * BLOCK, (2 * hk + p) * LANES:(2 * hk + p + 1) * LANES]
            parts.append(jnp.where(lo, qp, 0.0).astype(BF16))
            parts.append(jnp.where(lo, 0.0, qp).astype(BF16))
        kwin = kfull[hk * LANES:(hk + 1) * LANES, jb * BLOCK:(jb + 3) * BLOCK]
        s = jnp.dot(jnp.concatenate(parts, axis=0), kwin, preferred_element_type=F32)
        scores.append(jnp.where(jnp.concatenate([valids[jb]] * 4, axis=0), s, NEG_INF))
        sinks.extend(jnp.full((BLOCK, 1), sink_ref[hk * 4 + g], F32) for g in range(4))
    s = jnp.concatenate(scores, axis=0)
    sk = jnp.concatenate(sinks, axis=0)
    m = jnp.maximum(jnp.max(s, axis=-1, keepdims=True), sk)
    p = jnp.exp(s - m).astype(BF16)
    sink_term = jnp.exp(sk - m)
    rows = []
    for ui, (jb, hk) in enumerate(units):
        vwin = jnp.concatenate([vfull[jb * BLOCK:(jb + 3) * BLOCK, hk * LANES:(hk + 1) * LANES], ones], axis=1)
        r0 = ui * 4 * BLOCK
        ov = jnp.dot(p[r0:r0 + 4 * BLOCK], vwin, preferred_element_type=F32)
        o = ov[:, 0:LANES] / (ov[:, LANES:2 * LANES] + sink_term[r0:r0 + 4 * BLOCK])
        pair = [jnp.where(lo, o[(2 * p2) * BLOCK:(2 * p2 + 1) * BLOCK],
                          o[(2 * p2 + 1) * BLOCK:(2 * p2 + 2) * BLOCK]) for p2 in range(2)]
        rows.append(jnp.concatenate(pair, axis=1))
    n_h = N_KV_HEADS
    return jnp.concatenate(
        [jnp.concatenate(rows[jb * n_h:(jb + 1) * n_h], axis=1) for jb in range(MT // BLOCK)], axis=0)


def _mixer_kernel(seqs, sink_ref, xp_ref, xs_ref, mod_ref, cos_ref, sin_ref,
                  ktp_ref, ktc_ref, ktn_ref, vp_ref, vc_ref, vn_ref,
                  wmix_ref, bmix_ref, wsp_ref, bspt_ref, sg_ref, sb_ref,
                  wba_ref, wbs_ref, wout_ref, l1g_ref, l1b_ref, wr_ref, br_ref,
                  x1_ref, h2_ref, logit_ref):
    i = pl.program_id(0)
    seq_len, pos0, _ = _tile_pos(seqs, i * MT)
    mod = mod_ref[0]
    sh1, sc1, g1 = (mod[:, j * D_MODEL:(j + 1) * D_MODEL] for j in range(3))
    sh2, sc2 = (mod[:, j * D_MODEL:(j + 1) * D_MODEL] for j in range(3, 5))
    x = jnp.where(i * MT >= seqs.n_prompt, xs_ref[...], xp_ref[...])
    h = (x * (1.0 + sc1) + sh1).astype(BF16)
    kv_end = ATT_W + 2 * KV_W
    zq = jnp.dot(h, wmix_ref[:, 0:ATT_W], preferred_element_type=F32) + bmix_ref[:, 0:ATT_W]
    z = jnp.dot(h, wmix_ref[:, kv_end:], preferred_element_type=F32) + bmix_ref[:, kv_end:]
    q = _rope(zq, cos_ref[...], sin_ref[...]) * (HEAD_DIM ** -0.5)
    u = jax.nn.gelu(z[:, 0:SGU_W])
    vs = _layer_norm(jax.nn.gelu(z[:, SGU_W:2 * SGU_W]), sg_ref[...], sb_ref[...])
    ga = z[:, 2 * SGU_W:2 * SGU_W + D_MODEL]
    gs = z[:, 2 * SGU_W + D_MODEL:]

    kfull = jnp.concatenate([ktp_ref[...], ktc_ref[...], ktn_ref[...]], axis=1)
    vfull = jnp.concatenate([vp_ref[...], vc_ref[...], vn_ref[...]], axis=0)
    qi = lax.broadcasted_iota(I32, (BLOCK, 3 * BLOCK), 0)
    ki = lax.broadcasted_iota(I32, (BLOCK, 3 * BLOCK), 1)
    band = (ki >= qi) & (ki <= qi + 2 * BLOCK)
    vs_b = vs.astype(BF16)
    valids, sgu_rows = [], []
    for jb in range(MT // BLOCK):
        posb = pos0 + jb * BLOCK
        valids.append(band & (ki >= jnp.where(posb == 0, BLOCK, 0))
                      & (ki < jnp.where(posb + BLOCK == seq_len, 2 * BLOCK, 3 * BLOCK)))
        groups = []
        for g in range(SGU_GROUPS):
            vg = vs_b[jb * BLOCK:(jb + 1) * BLOCK, g * LANES:(g + 1) * LANES]
            sv = jnp.dot(wsp_ref[g], vg, preferred_element_type=F32) + bspt_ref[:, g:g + 1]
            groups.append(sv)
        sgu_rows.append(jnp.concatenate(groups, axis=1))
    attn = _attention(q, kfull, vfull, valids, sink_ref)
    sgu = u * jnp.concatenate(sgu_rows, axis=0)

    a1 = jnp.dot(attn.astype(BF16), wba_ref[...], preferred_element_type=F32)
    a2 = jnp.dot(sgu.astype(BF16), wbs_ref[...], preferred_element_type=F32)
    merged = jax.nn.sigmoid(ga) * a1 + jax.nn.sigmoid(gs) * a2
    mix = jnp.dot(merged.astype(BF16), wout_ref[...], preferred_element_type=F32)
    x1 = _layer_norm(DN_ALPHA * x + g1 * mix, l1g_ref[...], l1b_ref[...])
    x1_ref[...] = x1
    h2 = x1 * (1.0 + sc2) + sh2
    hi = h2.astype(BF16)
    h2_ref[...] = hi

    lo_part = (h2 - hi.astype(F32)).astype(BF16)
    l1 = jnp.dot(hi, wr_ref[...], preferred_element_type=F32)
    l2 = jnp.dot(lo_part, wr_ref[:, 0:N_EXPERTS], preferred_element_type=F32)
    logit_ref[...] = l1[:, 0:N_EXPERTS] + l1[:, N_EXPERTS:2 * N_EXPERTS] + l2 + br_ref[...]


PACK = 4096.0
assert DISP_ROWS <= PACK and COMB_ROWS <= PACK and DISP_ROWS * PACK < 2 ** 24


def _router_kernel(l_ref, rcol_ref, rrow_ref, cnt_ref):
    rt = l_ref.shape[0]
    ns = rt // TM
    padded = jnp.concatenate([l_ref[...], jnp.zeros((rt, LANES - N_EXPERTS), F32)], axis=1)
    work = padded.T[0:N_EXPERTS]
    eidx = lax.broadcasted_iota(I32, (N_EXPERTS, rt), 0)
    idxs, vals = [], []
    for _ in range(TOP_K):
        m = jnp.max(work, axis=0, keepdims=True)
        ix = jnp.min(jnp.where(work == m, eidx, N_EXPERTS), axis=0, keepdims=True)
        idxs.append(ix)
        vals.append(m)
        work = jnp.where(eidx == ix, NEG_INF, work)
    exps = [jnp.exp(v - vals[0]) for v in vals]
    esum = exps[0] + exps[1] + exps[2] + exps[3]
    wts = [e / esum for e in exps]

    sel = jnp.zeros((N_EXPERTS, rt), F32)
    for ix in idxs:
        sel = sel + jnp.where(eidx == ix, 1.0, 0.0)
    sel_b = sel.astype(BF16)
    ti = lax.broadcasted_iota(I32, (TM, TM), 0)
    tj = lax.broadcasted_iota(I32, (TM, TM), 1)
    earlier = jnp.where(ti < tj, 1.0, 0.0).astype(BF16)
    tiles = [slice(s * TM, (s + 1) * TM) for s in range(ns)]
    ranks = [jnp.dot(sel_b[:, t], earlier, preferred_element_type=F32) for t in tiles]
    cnts, nchs = [], []
    for t in tiles:
        cnt = jnp.sum(sel[:, t], axis=1, keepdims=True)
        cnt = cnt + (cnt - 2.0 * jnp.floor(cnt * 0.5))
        cnts.append(cnt)
        nchs.append(jnp.maximum(jnp.floor((cnt + (COMB_CHUNK - 0.5)) * (1.0 / COMB_CHUNK)), 1.0))
    lane = lax.broadcasted_iota(I32, (N_EXPERTS, LANES), 1)
    pre = jnp.zeros((N_EXPERTS, LANES), F32)
    for j, col in enumerate(cnts + nchs):
        pre = jnp.where(lane == j, col, pre)
    ei = lax.broadcasted_iota(I32, (N_EXPERTS, N_EXPERTS), 0)
    ej = lax.broadcasted_iota(I32, (N_EXPERTS, N_EXPERTS), 1)
    before = jnp.where(ej < ei, 1.0, 0.0).astype(BF16)
    base = jnp.dot(before, pre.astype(BF16), preferred_element_type=F32)
    both = jnp.concatenate(
        [(ranks[s] + base[:, s:s + 1]) * PACK + (ranks[s] + base[:, ns + s:ns + s + 1] * COMB_CHUNK)
         for s in range(ns)], axis=1)

    packed = [jnp.sum(jnp.where(eidx == ix, both, 0.0), axis=0, keepdims=True) for ix in idxs]
    drow = [jnp.floor(v * (1.0 / PACK)) for v in packed]
    table = [ix.astype(F32) for ix in idxs] + drow + [v - d * PACK for v, d in zip(packed, drow)] + wts
    sub = lax.broadcasted_iota(I32, (LANES, rt), 0)
    rr = jnp.zeros((LANES, rt), F32)
    for j, row in enumerate(table):
        rr = jnp.where(sub == j, row, rr)
    rcol_ref[...] = rr.T
    for s in range(ns):
        rrow_ref[s] = rr[0:16, tiles[s]]
        cnt_ref[s] = jnp.broadcast_to(cnts[s], (N_EXPERTS, LANES)).astype(I32)


def _router(logits):
    T = logits.shape[0]
    rt = math.gcd(T, ROUTER_TOKENS)
    nt = T // TM
    assert 2 * (rt // TM) <= LANES
    return pl.pallas_call(
        _router_kernel,
        grid=(T // rt,),
        in_specs=[pl.BlockSpec((rt, N_EXPERTS), lambda i: (i, 0))],
        out_specs=[pl.BlockSpec((rt, LANES), lambda i: (i, 0)),
                   pl.BlockSpec((rt // TM, 16, TM), lambda i: (i, 0, 0)),
                   pl.BlockSpec((rt // TM, N_EXPERTS, LANES), lambda i: (i, 0, 0))],
        out_shape=[jax.ShapeDtypeStruct((T, LANES), F32),
                   jax.ShapeDtypeStruct((nt, 16, TM), F32),
                   jax.ShapeDtypeStruct((nt, N_EXPERTS, LANES), I32)],
        compiler_params=pltpu.CompilerParams(dimension_semantics=("arbitrary",),
                                             vmem_limit_bytes=VMEM_LIMIT_BYTES),
        name="router",
    )(logits)


def _mixer(seqs, sink, xp, xs, mod3, cos_t, sin_t, kt, v, p):
    T = seqs.n_tokens
    nt = T // MT
    nb = T // BLOCK
    r = MT // BLOCK

    def mod_map(i, s):
        return (_tile_pos(seqs, i * MT)[2], 0, 0)

    def rope_map(i, s):
        return (_tile_pos(seqs, i * MT)[1] // MT, 0)

    const2 = lambda i, s: (0, 0)
    once = dict(pipeline_mode=pl.Buffered(1))
    in_specs = _group_x_specs(seqs, MT) + [
        pl.BlockSpec((1, 1, 6 * D_MODEL), mod_map),
        pl.BlockSpec((MT, LANES), rope_map),
        pl.BlockSpec((MT, LANES), rope_map),
        pl.BlockSpec((2 * LANES, BLOCK), lambda i, s: (0, jnp.maximum(i * r - 1, 0))),
        pl.BlockSpec((2 * LANES, MT), lambda i, s: (0, i)),
        pl.BlockSpec((2 * LANES, BLOCK), lambda i, s: (0, jnp.minimum(i * r + r, nb - 1))),
        pl.BlockSpec((BLOCK, 2 * LANES), lambda i, s: (jnp.maximum(i * r - 1, 0), 0)),
        pl.BlockSpec((MT, 2 * LANES), lambda i, s: (i, 0)),
        pl.BlockSpec((BLOCK, 2 * LANES), lambda i, s: (jnp.minimum(i * r + r, nb - 1), 0)),
        pl.BlockSpec(p["w_mix"].shape, const2, **once),
        pl.BlockSpec(p["b_mix"].shape, const2, **once),
        pl.BlockSpec(p["w_sp"].shape, lambda i, s: (0, 0, 0), **once),
        pl.BlockSpec(p["b_spt"].shape, const2, **once),
        pl.BlockSpec(p["sgu_g"].shape, const2, **once),
        pl.BlockSpec(p["sgu_b"].shape, const2, **once),
        pl.BlockSpec(p["w_ba"].shape, const2, **once),
        pl.BlockSpec(p["w_bs"].shape, const2, **once),
        pl.BlockSpec(p["w_out"].shape, const2, **once),
        pl.BlockSpec(p["ln1_g"].shape, const2, **once),
        pl.BlockSpec(p["ln1_b"].shape, const2, **once),
        pl.BlockSpec(p["w_r"].shape, const2, **once),
        pl.BlockSpec(p["b_r"].shape, const2, **once),
    ]
    out_specs = [
        pl.BlockSpec((MT, D_MODEL), lambda i, s: (i, 0)),
        pl.BlockSpec((MT, D_MODEL), lambda i, s: (i, 0)),
        pl.BlockSpec((MT, N_EXPERTS), lambda i, s: (i, 0)),
    ]
    out_shape = [
        jax.ShapeDtypeStruct((T, D_MODEL), F32),
        jax.ShapeDtypeStruct((T, D_MODEL), BF16),
        jax.ShapeDtypeStruct((T, N_EXPERTS), F32),
    ]
    return pl.pallas_call(
        functools.partial(_mixer_kernel, seqs),
        grid_spec=pltpu.PrefetchScalarGridSpec(
            num_scalar_prefetch=1, grid=(nt,), in_specs=in_specs, out_specs=out_specs),
        out_shape=out_shape,
        compiler_params=pltpu.CompilerParams(dimension_semantics=("arbitrary",),
                                             vmem_limit_bytes=VMEM_LIMIT_BYTES),
        name="mixer",
    )(sink, xp, xs, mod3, cos_t, sin_t, kt, kt, kt, v, v, v,
      p["w_mix"], p["b_mix"], p["w_sp"], p["b_spt"], p["sgu_g"], p["sgu_b"],
      p["w_ba"], p["w_bs"], p["w_out"], p["ln1_g"], p["ln1_b"], p["w_r"], p["b_r"])


def _pair_rows(row):
    return pl.multiple_of((row // PAIR) * ROW_TILES, ROW_TILES)


def _to_row_tiles(dst_ref, rows, n, start=0):
    words = pltpu.bitcast(rows, U32)
    base = start // PAIR * ROW_TILES
    for c in range(ROW_TILES):
        dst_ref[pl.ds(base + c, n // PAIR, stride=ROW_TILES), :] = words[:, c * LANES:(c + 1) * LANES]


def _from_row_tiles(src_ref, start, n):
    base = start // PAIR * ROW_TILES
    words = jnp.concatenate(
        [src_ref[pl.ds(base + c, n // PAIR, stride=ROW_TILES), :] for c in range(ROW_TILES)], axis=1)
    return pltpu.bitcast(words, BF16)


WAIT_GROUP = 16


def _issue_tile(copy, k0, src_ref, dst_ref, extra_ref, n_extra, step):
    for e in range(N_EXPERTS):
        copy(src_ref[k0 + e], dst_ref[k0 + e]).start()

    @pl.when(n_extra > 0)
    def _():
        def per_expert(e, c):
            def per_chunk(j, c2):
                copy(src_ref[k0 + e] + j * step, dst_ref[k0 + e] + j * step).start()
                return c2
            lax.fori_loop(1, extra_ref[k0 + e] + 1, per_chunk, 0)
            return c
        lax.fori_loop(0, N_EXPERTS, per_expert, 0)


def _wait_tile(group_copy, chunk_copy, n_extra):
    for _ in range(N_EXPERTS // WAIT_GROUP):
        group_copy.wait()

    def body(_, c):
        chunk_copy.wait()
        return c
    lax.fori_loop(0, n_extra, body, 0)


def _dispatch_kernel(b_ref, s_ref, x_ref, nx_ref, tail_ref, h2_ref, rrow_ref, xin_ref, stg_ref, zero_ref, sem):
    i = pl.program_id(0)
    nt = pl.num_programs(0)
    slot = i % 2
    rr = rrow_ref[0]
    pos = [rr[TOP_K + k:TOP_K + k + 1].astype(I32) for k in range(TOP_K)]
    for a in range(DISP_ROWS // DISP_BLOCK):
        rho = lax.broadcasted_iota(I32, (DISP_BLOCK, TM), 0) + a * DISP_BLOCK
        pt = jnp.zeros((DISP_BLOCK, TM), F32)
        for k in range(TOP_K):
            pt = pt + jnp.where(rho == pos[k], 1.0, 0.0)
        rows = jnp.dot(pt.astype(BF16), h2_ref[...], preferred_element_type=F32).astype(BF16)
        _to_row_tiles(stg_ref.at[slot], rows, DISP_BLOCK, a * DISP_BLOCK)

    chunk = DISP_CHUNK // PAIR * ROW_TILES

    def copy(src_row, dst_row, sl):
        return pltpu.make_async_copy(stg_ref.at[sl, pl.ds(_pair_rows(src_row), chunk)],
                                     xin_ref.at[pl.ds(_pair_rows(dst_row), chunk)], sem)

    def wait_tile(tile):
        group = pltpu.make_async_copy(stg_ref.at[0, pl.ds(0, WAIT_GROUP * chunk)],
                                      xin_ref.at[pl.ds(0, WAIT_GROUP * chunk)], sem)
        _wait_tile(group, copy(0, 0, 0), nx_ref[tile])

    @pl.when(i > 0)
    def _():
        wait_tile(i - 1)
    _issue_tile(lambda src, dst: copy(src, dst, slot), i * N_EXPERTS, b_ref, s_ref, x_ref, nx_ref[i], DISP_CHUNK)

    @pl.when(i == nt - 1)
    def _():
        wait_tile(i)
        zero_ref[...] = jnp.zeros_like(zero_ref)

        def zcopy(dst_row):
            return pltpu.make_async_copy(zero_ref, xin_ref.at[pl.ds(_pair_rows(dst_row), chunk)], sem)

        def zwait(count):
            def body(_, c):
                zcopy(0).wait()
                return c
            lax.fori_loop(0, count, body, 0)

        def per_expert(e, total):
            lo = tail_ref[e]
            nz = (tail_ref[N_EXPERTS + e] - lo) // DISP_CHUNK

            def per_chunk(j, c):
                zcopy(lo + j * DISP_CHUNK).start()
                return c
            lax.fori_loop(0, nz, per_chunk, 0)
            return total + nz
        zwait(lax.fori_loop(0, N_EXPERTS, per_expert, 0))

        def last_chunk(e, c):
            zcopy(tail_ref[N_EXPERTS + e] - DISP_CHUNK).start()
            return c
        lax.fori_loop(0, N_EXPERTS, last_chunk, 0)
        zwait(N_EXPERTS)


def _dispatch(seqs, b_te, s_te, x_te, nx_t, tails, h2, rrow, n_rows):
    nt = seqs.n_tokens // TM
    return pl.pallas_call(
        _dispatch_kernel,
        grid_spec=pltpu.PrefetchScalarGridSpec(
            num_scalar_prefetch=5, grid=(nt,),
            in_specs=[pl.BlockSpec((TM, D_MODEL), lambda i, *_: (i, 0)),
                      pl.BlockSpec((1, 16, TM), lambda i, *_: (i, 0, 0))],
            out_specs=pl.BlockSpec(memory_space=pl.ANY),
            scratch_shapes=[pltpu.VMEM((2, DISP_ROWS // PAIR * ROW_TILES, LANES), U32),
                            pltpu.VMEM((DISP_CHUNK // PAIR * ROW_TILES, LANES), U32),
                            pltpu.SemaphoreType.DMA]),
        out_shape=jax.ShapeDtypeStruct((n_rows // PAIR * ROW_TILES, LANES), U32),
        compiler_params=pltpu.CompilerParams(dimension_semantics=("arbitrary",),
                                             vmem_limit_bytes=VMEM_LIMIT_BYTES),
        name="dispatch",
    )(b_te, s_te, x_te, nx_t, tails, h2, rrow)


def _expert_kernel(be_ref, bv_ref, x_ref, wup_hbm, bup_ref, wdn_hbm, bdn_ref, o_ref,
                   wup_f, wdn_f, wup_b, wdn_b, sem):
    i = pl.program_id(0)
    e = be_ref[i]
    first = jnp.logical_or(i == 0, e != be_ref[jnp.maximum(i - 1, 0)])

    def fetch(ex):
        slot = ex % 2
        return (pltpu.make_async_copy(wup_hbm.at[ex], wup_f.at[slot], sem.at[0, slot]),
                pltpu.make_async_copy(wdn_hbm.at[ex], wdn_f.at[slot], sem.at[1, slot]))

    @pl.when(i == 0)
    def _():
        for d in fetch(e):
            d.start()

    @pl.when(first)
    def _():
        for d in fetch(e):
            d.wait()

        @pl.when(e + 1 < N_EXPERTS)
        def _():
            for d in fetch(e + 1):
                d.start()

        @pl.when(bv_ref[i * EXPERT_SUBS] > 0)
        def _():
            wup_b[...] = wup_f[e % 2].astype(BF16)
            wdn_b[...] = wdn_f[e % 2].astype(BF16)

    def ffn(start, n, valid):
        x = _from_row_tiles(x_ref, start, n)
        row = lax.broadcasted_iota(I32, (n, 1), 0)
        xb = jnp.where(row < valid, x, jnp.zeros_like(x))
        hu = jnp.dot(xb, wup_b[...], preferred_element_type=F32) + bup_ref[0]
        gate = jnp.minimum(hu[:, 0:D_FF], SWIGLU_LIMIT)
        lin = jnp.clip(hu[:, D_FF:], -SWIGLU_LIMIT, SWIGLU_LIMIT)
        act = gate * jax.nn.sigmoid(SWIGLU_ALPHA * gate) * (lin + 1.0)
        y = jnp.dot(act.astype(BF16), wdn_b[...], preferred_element_type=F32) + bdn_ref[0]
        _to_row_tiles(o_ref, y.astype(BF16), n, start)

    def zero(start, n):
        o_ref[start // PAIR * ROW_TILES:(start + n) // PAIR * ROW_TILES, :] = jnp.zeros(
            (n // PAIR * ROW_TILES, LANES), U32)

    half = BM // 2
    for sub in range(EXPERT_SUBS):
        valid = bv_ref[i * EXPERT_SUBS + sub]
        start = sub * BM

        @pl.when(valid > half)
        def _():
            ffn(start, BM, valid)

        @pl.when(jnp.logical_and(valid > 0, valid <= half))
        def _():
            ffn(start, half, valid)
            zero(start + half, half)

        @pl.when(valid == 0)
        def _():
            zero(start, BM)


def _experts(blk_e, blk_v, xin, w_up, b_up, w_down, b_down, n_blk):
    gb = EXPERT_SUBS * BM
    return pl.pallas_call(
        _expert_kernel,
        grid_spec=pltpu.PrefetchScalarGridSpec(
            num_scalar_prefetch=2, grid=(n_blk,),
            in_specs=[pl.BlockSpec((gb // PAIR * ROW_TILES, LANES), lambda i, be, bv: (i, 0)),
                      pl.BlockSpec(memory_space=pl.ANY),
                      pl.BlockSpec((1, 1, 2 * D_FF), lambda i, be, bv: (be[i], 0, 0)),
                      pl.BlockSpec(memory_space=pl.ANY),
                      pl.BlockSpec((1, 1, D_MODEL), lambda i, be, bv: (be[i], 0, 0))],
            out_specs=pl.BlockSpec((gb // PAIR * ROW_TILES, LANES), lambda i, be, bv: (i, 0)),
            scratch_shapes=[pltpu.VMEM((2, D_MODEL, 2 * D_FF), F32),
                            pltpu.VMEM((2, D_FF, D_MODEL), F32),
                            pltpu.VMEM((D_MODEL, 2 * D_FF), BF16),
                            pltpu.VMEM((D_FF, D_MODEL), BF16),
                            pltpu.SemaphoreType.DMA((2, 2))]),
        out_shape=jax.ShapeDtypeStruct((n_blk * gb // PAIR * ROW_TILES, LANES), U32),
        compiler_params=pltpu.CompilerParams(dimension_semantics=("arbitrary",),
                                             vmem_limit_bytes=VMEM_LIMIT_BYTES),
        name="experts",
    )(blk_e, blk_v, xin, w_up, b_up.reshape(N_EXPERTS, 1, 2 * D_FF), w_down,
      b_down.reshape(N_EXPERTS, 1, D_MODEL))


COMB_KC = COMB_ROWS // 256
COMB_KC_MIN = N_EXPERTS * COMB_CHUNK // 256


def _combine_kernel(tile0, s_ref, b_ref, x_ref, nx_ref, rows_ref, x1_ref, rcol_ref, mod_ref, g_ref, bb_ref,
                    eo_ref, y_ref, stg_ref, acc_ref, sem):
    i = pl.program_id(0)
    nt = pl.num_programs(0)
    slot = i % 2
    chunk = COMB_CHUNK // PAIR * ROW_TILES

    def copy(src_row, dst_row, sl):
        return pltpu.make_async_copy(eo_ref.at[pl.ds(_pair_rows(src_row), chunk)],
                                     stg_ref.at[sl, pl.ds(_pair_rows(dst_row), chunk)], sem.at[sl])

    def issue(tile, sl):
        _issue_tile(lambda src, dst: copy(src, dst, sl), tile * N_EXPERTS, s_ref, b_ref, x_ref, nx_ref[tile],
                    COMB_CHUNK)

    @pl.when(i == 0)
    def _():
        stg_ref[...] = jnp.zeros_like(stg_ref)
        issue(tile0, 0)

    rc = rcol_ref[...]
    col_k = [rc[:, 2 * TOP_K + k:2 * TOP_K + k + 1].astype(I32) for k in range(TOP_K)]
    w_k = [rc[:, 3 * TOP_K + k:3 * TOP_K + k + 1] for k in range(TOP_K)]

    def weights(c):
        jl = lax.broadcasted_iota(I32, (TM, 256), 1) + c * 256
        pm = jnp.zeros((TM, 256), F32)
        for k in range(TOP_K):
            pm = pm + jnp.where(jl == col_k[k], w_k[k], 0.0)
        return pm.astype(BF16)

    @pl.when(i + 1 < nt)
    def _():
        issue(tile0 + i + 1, 1 - slot)

    stg = stg_ref.at[slot]
    group = pltpu.make_async_copy(eo_ref.at[pl.ds(0, WAIT_GROUP * chunk)],
                                  stg.at[pl.ds(0, WAIT_GROUP * chunk)], sem.at[slot])
    _wait_tile(group, copy(0, 0, slot), nx_ref[tile0 + i])

    pm = jnp.concatenate([weights(c) for c in range(COMB_KC_MIN)], axis=1)
    acc_ref[...] = jnp.dot(pm, _from_row_tiles(stg, 0, COMB_KC_MIN * 256), preferred_element_type=F32)
    used = (rows_ref[tile0 + i] + 255) // 256
    for c in range(COMB_KC_MIN, COMB_KC):
        @pl.when(c < used)
        def _():
            acc_ref[...] += jnp.dot(weights(c), _from_row_tiles(stg, c * 256, 256), preferred_element_type=F32)

    mod = mod_ref[0]
    g2 = mod[:, 5 * D_MODEL:6 * D_MODEL]
    y_ref[...] = _layer_norm(DN_ALPHA * x1_ref[...] + g2 * acc_ref[...], g_ref[...], bb_ref[...])


def _combine(seqs, tile0, n_tiles, s_te, b_te, x_te, nx_t, rows_t, x1, rcol, mod3, ln2_g, ln2_b, eo):
    def mod_map(i, *_):
        return (_tile_pos(seqs, (i + tile0) * TM)[2], 0, 0)

    return pl.pallas_call(
        functools.partial(_combine_kernel, tile0),
        grid_spec=pltpu.PrefetchScalarGridSpec(
            num_scalar_prefetch=5, grid=(n_tiles,),
            in_specs=[pl.BlockSpec((TM, D_MODEL), lambda i, *_: (i + tile0, 0)),
                      pl.BlockSpec((TM, LANES), lambda i, *_: (i + tile0, 0)),
                      pl.BlockSpec((1, 1, 6 * D_MODEL), mod_map),
                      pl.BlockSpec((1, D_MODEL), lambda i, *_: (0, 0)),
                      pl.BlockSpec((1, D_MODEL), lambda i, *_: (0, 0)),
                      pl.BlockSpec(memory_space=pl.ANY)],
            out_specs=pl.BlockSpec((TM, D_MODEL), lambda i, *_: (i, 0)),
            scratch_shapes=[pltpu.VMEM((2, COMB_ROWS // PAIR * ROW_TILES, LANES), U32),
                            pltpu.VMEM((TM, D_MODEL), F32),
                            pltpu.SemaphoreType.DMA((2,))]),
        out_shape=jax.ShapeDtypeStruct((n_tiles * TM, D_MODEL), F32),
        compiler_params=pltpu.CompilerParams(dimension_semantics=("arbitrary",),
                                             vmem_limit_bytes=VMEM_LIMIT_BYTES),
        name="combine",
    )(s_te, b_te, x_te, nx_t, rows_t, x1, rcol, mod3, ln2_g, ln2_b, eo)


def _prep_params(w_in, b_in, w_spatial, b_spatial, sgu_ln_g, sgu_ln_b, w_br_attn, w_br_sgu, w_out,
                 ln1_g, ln1_b, w_router, b_router):
    q_end, k_end, v_end = ATT_W, ATT_W + KV_W, ATT_W + 2 * KV_W

    def dup(w, lo):
        h0, h1 = w[..., lo:lo + HEAD_DIM], w[..., lo + HEAD_DIM:lo + 2 * HEAD_DIM]
        return jnp.concatenate([h0, h0, h1, h1], axis=-1)

    w_kv = jnp.concatenate([dup(w_in, q_end), dup(w_in, k_end)], axis=1).astype(BF16)
    b_kv = jnp.concatenate([dup(b_in, q_end), dup(b_in, k_end)], axis=0).reshape(1, -1)
    w_mix = w_in.astype(BF16)
    b_mix = b_in.reshape(1, -1)
    w_hi = w_router.astype(BF16)
    w_lo = (w_router - w_hi.astype(F32)).astype(BF16)
    p = dict(
        w_mix=w_mix, b_mix=b_mix,
        w_sp=w_spatial.astype(BF16), b_spt=b_spatial.T,
        sgu_g=sgu_ln_g.reshape(1, -1), sgu_b=sgu_ln_b.reshape(1, -1),
        w_ba=w_br_attn.astype(BF16), w_bs=w_br_sgu.astype(BF16), w_out=w_out.astype(BF16),
        ln1_g=ln1_g.reshape(1, -1), ln1_b=ln1_b.reshape(1, -1),
        w_r=jnp.concatenate([w_hi, w_lo], axis=1), b_r=b_router.reshape(1, -1),
    )
    return w_kv, b_kv, p


def _layer(seqs, xp, xs, c_all, w_ada, b_ada, w_in, b_in, sink, sgu_ln_g, sgu_ln_b, w_spatial, b_spatial,
           w_br_attn, w_br_sgu, w_out, ln1_g, ln1_b, w_router, b_router, w_up, b_up, w_down, b_down,
           ln2_g, ln2_b):
    T = seqs.n_tokens
    nt = T // TM
    ntp = seqs.n_prompt // TM
    c_pad = jnp.zeros((8, D_MODEL), F32).at[:c_all.shape[0]].set(c_all)
    mod3 = _ada(c_pad, w_ada, b_ada).reshape(8, 1, 6 * D_MODEL)
    cos_t, sin_t = _rope_tables(max(seqs.prompt_len, seqs.sample_len))
    w_kv, b_kv, p = _prep_params(w_in, b_in, w_spatial, b_spatial, sgu_ln_g, sgu_ln_b, w_br_attn,
                                 w_br_sgu, w_out, ln1_g, ln1_b, w_router, b_router)
    kt, v = _kv(seqs, xp, xs, mod3, cos_t, sin_t, w_kv, b_kv)
    x1, h2, logits = _mixer(seqs, sink, xp, xs, mod3, cos_t, sin_t, kt, v, p)
    rcol, rrow, cnt3 = _router(logits)

    cnt = cnt3[:, :, 0]
    count = cnt.sum(0)
    gb = EXPERT_SUBS * BM
    reg = (count + REGION_SLACK + gb - 1) // gb * gb
    pad_end = jnp.cumsum(reg)
    pad_start = pad_end - reg
    flat = lambda a: a.reshape(-1).astype(I32)
    s_te = flat(pad_start[None, :] + jnp.cumsum(cnt, axis=0) - cnt)
    d_nch = jnp.maximum((cnt + DISP_CHUNK - 1) // DISP_CHUNK, 1)
    c_nch = jnp.maximum((cnt + COMB_CHUNK - 1) // COMB_CHUNK, 1)
    db_te = flat(jnp.cumsum(cnt, axis=1) - cnt)
    cb_te = flat(COMB_CHUNK * (jnp.cumsum(c_nch, axis=1) - c_nch))
    rows_t = flat(COMB_CHUNK * c_nch.sum(1))
    n_blk = (nt * TILE_ROWS + N_EXPERTS * (REGION_SLACK + gb - 1)) // gb + 1
    tails = jnp.concatenate([pad_start + count, pad_end[:-1], jnp.array([n_blk * gb])]).astype(I32)
    blk_e = jnp.minimum(((jnp.arange(n_blk, dtype=I32) * gb)[:, None] >= pad_end[None, :]).sum(1),
                        N_EXPERTS - 1).astype(I32)
    sub_start = jnp.arange(n_blk * EXPERT_SUBS, dtype=I32) * BM
    owner = jnp.repeat(blk_e, EXPERT_SUBS)[:, None] == jnp.arange(N_EXPERTS)[None, :]
    sub_rows = (owner * (pad_start + count)[None, :]).sum(1) - sub_start
    blk_v = jnp.where(sub_start < pad_end[-1], jnp.clip(sub_rows, 0, BM), 0).astype(I32)

    xin = _dispatch(seqs, db_te, s_te, flat(d_nch - 1), flat((d_nch - 1).sum(1)), tails, h2, rrow, n_blk * gb)
    eo = _experts(blk_e, blk_v, xin, w_up, b_up, w_down, b_down, n_blk)
    comb = (s_te, cb_te, flat(c_nch - 1), flat((c_nch - 1).sum(1)), rows_t,
            x1, rcol, mod3, ln2_g.reshape(1, -1), ln2_b.reshape(1, -1), eo)
    return _combine(seqs, 0, ntp, *comb), _combine(seqs, ntp, nt - ntp, *comb)


def kernel(x_prompt, x_sample, c_prompt, c_sample, w_ada, b_ada, w_in, b_in, sink, sgu_ln_g, sgu_ln_b, w_spatial, b_spatial, w_br_attn, w_br_sgu, w_out, ln1_g, ln1_b, w_router, b_router, w_up, b_up, w_down, b_down, ln2_g, ln2_b):
    assert w_ada.shape[0] == DEPTH == 1
    bp, sp, d = x_prompt.shape
    bs, ss, _ = x_sample.shape
    seqs = _Seqs(n_prompt=bp * sp, prompt_len=sp, sample_len=ss, n_tokens=bp * sp + bs * ss)
    c_all = jnp.concatenate([c_prompt, c_sample], axis=0)
    yp, ys = _layer(seqs, x_prompt.reshape(bp * sp, d), x_sample.reshape(bs * ss, d), c_all,
                    w_ada[0], b_ada[0], w_in[0], b_in[0], sink[0], sgu_ln_g[0], sgu_ln_b[0],
                    w_spatial[0], b_spatial[0], w_br_attn[0], w_br_sgu[0], w_out[0], ln1_g[0], ln1_b[0],
                    w_router[0], b_router[0], w_up[0], b_up[0], w_down[0], b_down[0], ln2_g[0], ln2_b[0])
    return (yp.reshape(bp, sp, d), ys.reshape(bs, ss, d))
```

```python
import functools
import math
from typing import NamedTuple

import jax
import jax.numpy as jnp
from jax import lax
from jax.experimental import pallas as pl
from jax.experimental.pallas import tpu as pltpu

F32 = jnp.float32
BF16 = jnp.bfloat16
I32 = jnp.int32

D_MODEL = 1024
N_HEADS = 8
N_KV_HEADS = 2
HEAD_DIM = 64
ATT_W = N_HEADS * HEAD_DIM
KV_W = N_KV_HEADS * HEAD_DIM
BLOCK = 128
ROPE_THETA = 500000.0
ROT_DIM = HEAD_DIM // 4
ROT_HALF = ROT_DIM // 2
SGU_W = D_MODEL // 2
SGU_GROUPS = 4
N_EXPERTS = 32
TOP_K = 4
D_FF = D_MODEL
SWIGLU_LIMIT = 7.0
SWIGLU_ALPHA = 1.702
LN_EPS = 1e-5
DEPTH = 1
DN_ALPHA = (2 * DEPTH) ** 0.25

LANES = 128
MXU_DEPTH = 256
ROW_TILES = D_MODEL // LANES
VMEM_LIMIT_BYTES = 56 * 1024 * 1024
U32 = jnp.uint32
PAIR = 2

TM = 256
MT = 512
TK = 1024
BM = 512
EXPERT_SUBS = 2
ROUTER_TOKENS = 2048
DISP_CHUNK = 48
COMB_CHUNK = 48
REGION_SLACK = DISP_CHUNK
TILE_ROWS = TM * TOP_K + N_EXPERTS
DISP_ROWS = TILE_ROWS + DISP_CHUNK
DISP_BLOCK = DISP_ROWS // 3
assert DISP_BLOCK * 3 == DISP_ROWS and DISP_BLOCK % 16 == 0
COMB_ROWS = -(-(TILE_ROWS + N_EXPERTS * COMB_CHUNK) // MXU_DEPTH) * MXU_DEPTH
NEG_INF = float("-inf")


class _Seqs(NamedTuple):
    n_prompt: int
    prompt_len: int
    sample_len: int
    n_tokens: int


def _tile_pos(seqs, t0):
    is_s = t0 >= seqs.n_prompt
    seq_len = jnp.where(is_s, seqs.sample_len, seqs.prompt_len)
    off = jnp.where(is_s, t0 - seqs.n_prompt, t0)
    pos0 = off % seq_len
    row = jnp.where(is_s, seqs.n_prompt // seqs.prompt_len + off // seq_len, off // seq_len)
    return seq_len, pos0, row


def _ada_kernel(c_ref, w_ref, b_ref, o_ref):
    c = c_ref[...]
    a = c * jax.nn.sigmoid(c)
    o_ref[...] = jnp.dot(a, w_ref[...], preferred_element_type=F32,
                         precision=lax.Precision.HIGHEST) + b_ref[...]


def _ada(c_pad, w_ada, b_ada):
    n = w_ada.shape[1]
    bn = n // 4
    return pl.pallas_call(
        _ada_kernel,
        grid=(n // bn,),
        in_specs=[pl.BlockSpec((8, D_MODEL), lambda j: (0, 0)),
                  pl.BlockSpec((D_MODEL, bn), lambda j: (0, j)),
                  pl.BlockSpec((1, bn), lambda j: (0, j))],
        out_specs=pl.BlockSpec((8, bn), lambda j: (0, j)),
        out_shape=jax.ShapeDtypeStruct((8, n), F32),
        compiler_params=pltpu.CompilerParams(vmem_limit_bytes=VMEM_LIMIT_BYTES),
        name="ada",
    )(c_pad, w_ada, b_ada.reshape(1, n))


def _rope_tables(length):
    inv = ROPE_THETA ** (-jnp.arange(ROT_HALF, dtype=F32) * 2.0 / ROT_DIM)
    ang = jnp.arange(length, dtype=F32)[:, None] * inv[None, :]
    lane = jnp.arange(LANES) % HEAD_DIM
    spread = ((lane[None, :] % ROT_HALF == jnp.arange(ROT_HALF)[:, None]) & (lane[None, :] < ROT_DIM)).astype(F32)
    sign = jnp.where(lane < ROT_HALF, -1.0, 1.0)
    exact = dict(precision=lax.Precision.HIGHEST, preferred_element_type=F32)
    cos = jnp.dot(jnp.cos(ang), spread, **exact) + (lane >= ROT_DIM).astype(F32)[None, :]
    sin = jnp.dot(jnp.sin(ang), spread * sign[None, :], **exact)
    return cos, sin


def _rope(x, cos, sin):
    n = x.shape[1]
    reps = n // LANES
    c = jnp.concatenate([cos] * reps, axis=1)
    s = jnp.concatenate([sin] * reps, axis=1)
    lane = lax.broadcasted_iota(I32, x.shape, 1)
    first = (lane & (HEAD_DIM - 1)) < ROT_HALF
    partner = jnp.where(first, pltpu.roll(x, n - ROT_HALF, 1), pltpu.roll(x, ROT_HALF, 1))
    return x * c + partner * s


def _kv_kernel(seqs, xp_ref, xs_ref, mod_ref, cos_ref, sin_ref, w_ref, b_ref, kt_ref, v_ref):
    mod = mod_ref[0]
    sh1 = mod[:, 0:D_MODEL]
    sc1 = mod[:, D_MODEL:2 * D_MODEL]
    x = jnp.where(pl.program_id(0) * TK >= seqs.n_prompt, xs_ref[...], xp_ref[...])
    h = (x * (1.0 + sc1) + sh1).astype(BF16)
    kv = jnp.dot(h, w_ref[...], preferred_element_type=F32) + b_ref[...]
    k = _rope(kv[:, 0:2 * LANES], cos_ref[...], sin_ref[...])
    kt_ref[...] = k.T.astype(BF16)
    v_ref[...] = kv[:, 2 * LANES:4 * LANES].astype(BF16)


def _group_x_specs(seqs, tile):
    ntp = seqs.n_prompt // tile
    return [pl.BlockSpec((tile, D_MODEL), lambda i, *_: (jnp.minimum(i, ntp - 1), 0)),
            pl.BlockSpec((tile, D_MODEL), lambda i, *_: (jnp.maximum(i - ntp, 0), 0))]


def _kv(seqs, xp, xs, mod3, cos_t, sin_t, w_kv, b_kv):
    T = seqs.n_tokens

    def mod_map(i):
        return (_tile_pos(seqs, i * TK)[2], 0, 0)

    def rope_map(i):
        return (_tile_pos(seqs, i * TK)[1] // TK, 0)

    return pl.pallas_call(
        functools.partial(_kv_kernel, seqs),
        grid=(T // TK,),
        in_specs=_group_x_specs(seqs, TK) + [
                  pl.BlockSpec((1, 1, 6 * D_MODEL), mod_map),
                  pl.BlockSpec((TK, LANES), rope_map),
                  pl.BlockSpec((TK, LANES), rope_map),
                  pl.BlockSpec((D_MODEL, 4 * LANES), lambda i: (0, 0)),
                  pl.BlockSpec((1, 4 * LANES), lambda i: (0, 0))],
        out_specs=[pl.BlockSpec((2 * LANES, TK), lambda i: (0, i)),
                   pl.BlockSpec((TK, 2 * LANES), lambda i: (i, 0))],
        out_shape=[jax.ShapeDtypeStruct((2 * LANES, T), BF16),
                   jax.ShapeDtypeStruct((T, 2 * LANES), BF16)],
        compiler_params=pltpu.CompilerParams(dimension_semantics=("arbitrary",),
                                             vmem_limit_bytes=VMEM_LIMIT_BYTES),
        name="kv",
    )(xp, xs, mod3, cos_t, sin_t, w_kv, b_kv)


def _layer_norm(x, g, b):
    mu = jnp.mean(x, axis=-1, keepdims=True)
    xc = x - mu
    var = jnp.mean(xc * xc, axis=-1, keepdims=True)
    return xc * lax.rsqrt(var + LN_EPS) * g + b


def _attention(q, kfull, vfull, valids, sink_ref):
    lane = lax.broadcasted_iota(I32, (BLOCK, LANES), 1)
    lo = lane < HEAD_DIM
    ones = jnp.ones((3 * BLOCK, LANES), BF16)
    units = [(jb, hk) for jb in range(MT // BLOCK) for hk in range(N_KV_HEADS)]
    scores, sinks = [], []
    for jb, hk in units:
        parts = []
        for p in range(2):
            qp = q[jb * BLOCK:(jb + 1) * BLOCK, (2 * hk + p) * LANES:(2 * hk + p + 1) * LANES]
            parts.append(jnp.where(lo, qp, 0.0).astype(BF16))
            parts.append(jnp.where(lo, 0.0, qp).astype(BF16))
        kwin = kfull[hk * LANES:(hk + 1) * LANES, jb * BLOCK:(jb + 3) * BLOCK]
        s = jnp.dot(jnp.concatenate(parts, axis=0), kwin, preferred_element_type=F32)
        scores.append(jnp.where(jnp.concatenate([valids[jb]] * 4, axis=0), s, NEG_INF))
        sinks.extend(jnp.full((BLOCK, 1), sink_ref[hk * 4 + g], F32) for g in range(4))
    s = jnp.concatenate(scores, axis=0)
    sk = jnp.concatenate(sinks, axis=0)
    m = jnp.maximum(jnp.max(s, axis=-1, keepdims=True), sk)
    p = jnp.exp(s - m).astype(BF16)
    sink_term = jnp.exp(sk - m)
    rows = []
    for ui, (jb, hk) in enumerate(units):
        vwin = jnp.concatenate([vfull[jb * BLOCK:(jb + 3) * BLOCK, hk * LANES:(hk + 1) * LANES], ones], axis=1)
        r0 = ui * 4 * BLOCK
        ov = jnp.dot(p[r0:r0 + 4 * BLOCK], vwin, preferred_element_type=F32)
        o = ov[:, 0:LANES] / (ov[:, LANES:2 * LANES] + sink_term[r0:r0 + 4 * BLOCK])
        pair = [jnp.where(lo, o[(2 * p2) * BLOCK:(2 * p2 + 1) * BLOCK],
                          o[(2 * p2 + 1) * BLOCK:(2 * p2 + 2) * BLOCK]) for p2 in range(2)]
        rows.append(jnp.concatenate(pair, axis=1))
    n_h = N_KV_HEADS
    return jnp.concatenate(
        [jnp.concatenate(rows[jb * n_h:(jb + 1) * n_h], axis=1) for jb in range(MT // BLOCK)], axis=0)


def _mixer_kernel(seqs, sink_ref, xp_ref, xs_ref, mod_ref, cos_ref, sin_ref,
                  ktp_ref, ktc_ref, ktn_ref, vp_ref, vc_ref, vn_ref,
                  wmix_ref, bmix_ref, wsp_ref, bspt_ref, sg_ref, sb_ref,
                  wba_ref, wbs_ref, wout_ref, l1g_ref, l1b_ref, wr_ref, br_ref,
                  x1_ref, h2_ref, logit_ref):
    i = pl.program_id(0)
    seq_len, pos0, _ = _tile_pos(seqs, i * MT)
    mod = mod_ref[0]
    sh1, sc1, g1 = (mod[:, j * D_MODEL:(j + 1) * D_MODEL] for j in range(3))
    sh2, sc2 = (mod[:, j * D_MODEL:(j + 1) * D_MODEL] for j in range(3, 5))
    x = jnp.where(i * MT >= seqs.n_prompt, xs_ref[...], xp_ref[...])
    h = (x * (1.0 + sc1) + sh1).astype(BF16)
    kv_end = ATT_W + 2 * KV_W
    zq = jnp.dot(h, wmix_ref[:, 0:ATT_W], preferred_element_type=F32) + bmix_ref[:, 0:ATT_W]
    z = jnp.dot(h, wmix_ref[:, kv_end:], preferred_element_type=F32) + bmix_ref[:, kv_end:]
    q = _rope(zq, cos_ref[...], sin_ref[...]) * (HEAD_DIM ** -0.5)
    u = jax.nn.gelu(z[:, 0:SGU_W])
    vs = _layer_norm(jax.nn.gelu(z[:, SGU_W:2 * SGU_W]), sg_ref[...], sb_ref[...])
    ga = z[:, 2 * SGU_W:2 * SGU_W + D_MODEL]
    gs = z[:, 2 * SGU_W + D_MODEL:]

    kfull = jnp.concatenate([ktp_ref[...], ktc_ref[...], ktn_ref[...]], axis=1)
    vfull = jnp.concatenate([vp_ref[...], vc_ref[...], vn_ref[...]], axis=0)
    qi = lax.broadcasted_iota(I32, (BLOCK, 3 * BLOCK), 0)
    ki = lax.broadcasted_iota(I32, (BLOCK, 3 * BLOCK), 1)
    band = (ki >= qi) & (ki <= qi + 2 * BLOCK)
    vs_b = vs.astype(BF16)
    valids, sgu_rows = [], []
    for jb in range(MT // BLOCK):
        posb = pos0 + jb * BLOCK
        valids.append(band & (ki >= jnp.where(posb == 0, BLOCK, 0))
                      & (ki < jnp.where(posb + BLOCK == seq_len, 2 * BLOCK, 3 * BLOCK)))
        groups = []
        for g in range(SGU_GROUPS):
            vg = vs_b[jb * BLOCK:(jb + 1) * BLOCK, g * LANES:(g + 1) * LANES]
            sv = jnp.dot(wsp_ref[g], vg, preferred_element_type=F32) + bspt_ref[:, g:g + 1]
            groups.append(sv)
        sgu_rows.append(jnp.concatenate(groups, axis=1))
    attn = _attention(q, kfull, vfull, valids, sink_ref)
    sgu = u * jnp.concatenate(sgu_rows, axis=0)

    a1 = jnp.dot(attn.astype(BF16), wba_ref[...], preferred_element_type=F32)
    a2 = jnp.dot(sgu.astype(BF16), wbs_ref[...], preferred_element_type=F32)
    merged = jax.nn.sigmoid(ga) * a1 + jax.nn.sigmoid(gs) * a2
    mix = jnp.dot(merged.astype(BF16), wout_ref[...], preferred_element_type=F32)
    x1 = _layer_norm(DN_ALPHA * x + g1 * mix, l1g_ref[...], l1b_ref[...])
    x1_ref[...] = x1
    h2 = x1 * (1.0 + sc2) + sh2
    hi = h2.astype(BF16)
    h2_ref[...] = hi

    lo_part = (h2 - hi.astype(F32)).astype(BF16)
    l1 = jnp.dot(hi, wr_ref[...], preferred_element_type=F32)
    l2 = jnp.dot(lo_part, wr_ref[:, 0:N_EXPERTS], preferred_element_type=F32)
    logit_ref[...] = l1[:, 0:N_EXPERTS] + l1[:, N_EXPERTS:2 * N_EXPERTS] + l2 + br_ref[...]


PACK = 4096.0
assert DISP_ROWS <= PACK and COMB_ROWS <= PACK and DISP_ROWS * PACK < 2 ** 24


def _router_kernel(l_ref, rcol_ref, rrow_ref, cnt_ref):
    rt = l_ref.shape[0]
    ns = rt // TM
    padded = jnp.concatenate([l_ref[...], jnp.zeros((rt, LANES - N_EXPERTS), F32)], axis=1)
    work = padded.T[0:N_EXPERTS]
    eidx = lax.broadcasted_iota(I32, (N_EXPERTS, rt), 0)
    idxs, vals = [], []
    for _ in range(TOP_K):
        m = jnp.max(work, axis=0, keepdims=True)
        ix = jnp.min(jnp.where(work == m, eidx, N_EXPERTS), axis=0, keepdims=True)
        idxs.append(ix)
        vals.append(m)
        work = jnp.where(eidx == ix, NEG_INF, work)
    exps = [jnp.exp(v - vals[0]) for v in vals]
    esum = exps[0] + exps[1] + exps[2] + exps[3]
    wts = [e / esum for e in exps]

    sel = jnp.zeros((N_EXPERTS, rt), F32)
    for ix in idxs:
        sel = sel + jnp.where(eidx == ix, 1.0, 0.0)
    sel_b = sel.astype(BF16)
    ti = lax.broadcasted_iota(I32, (TM, TM), 0)
    tj = lax.broadcasted_iota(I32, (TM, TM), 1)
    earlier = jnp.where(ti < tj, 1.0, 0.0).astype(BF16)
    tiles = [slice(s * TM, (s + 1) * TM) for s in range(ns)]
    ranks = [jnp.dot(sel_b[:, t], earlier, preferred_element_type=F32) for t in tiles]
    cnts, nchs = [], []
    for t in tiles:
        cnt = jnp.sum(sel[:, t], axis=1, keepdims=True)
        cnt = cnt + (cnt - 2.0 * jnp.floor(cnt * 0.5))
        cnts.append(cnt)
        nchs.append(jnp.maximum(jnp.floor((cnt + (COMB_CHUNK - 0.5)) * (1.0 / COMB_CHUNK)), 1.0))
    lane = lax.broadcasted_iota(I32, (N_EXPERTS, LANES), 1)
    pre = jnp.zeros((N_EXPERTS, LANES), F32)
    for j, col in enumerate(cnts + nchs):
        pre = jnp.where(lane == j, col, pre)
    ei = lax.broadcasted_iota(I32, (N_EXPERTS, N_EXPERTS), 0)
    ej = lax.broadcasted_iota(I32, (N_EXPERTS, N_EXPERTS), 1)
    before = jnp.where(ej < ei, 1.0, 0.0).astype(BF16)
    base = jnp.dot(before, pre.astype(BF16), preferred_element_type=F32)
    both = jnp.concatenate(
        [(ranks[s] + base[:, s:s + 1]) * PACK + (ranks[s] + base[:, ns + s:ns + s + 1] * COMB_CHUNK)
         for s in range(ns)], axis=1)

    packed = [jnp.sum(jnp.where(eidx == ix, both, 0.0), axis=0, keepdims=True) for ix in idxs]
    drow = [jnp.floor(v * (1.0 / PACK)) for v in packed]
    table = [ix.astype(F32) for ix in idxs] + drow + [v - d * PACK for v, d in zip(packed, drow)] + wts
    sub = lax.broadcasted_iota(I32, (LANES, rt), 0)
    rr = jnp.zeros((LANES, rt), F32)
    for j, row in enumerate(table):
        rr = jnp.where(sub == j, row, rr)
    rcol_ref[...] = rr.T
    for s in range(ns):
        rrow_ref[s] = rr[0:16, tiles[s]]
        cnt_ref[s] = jnp.broadcast_to(cnts[s], (N_EXPERTS, LANES)).astype(I32)


def _router(logits):
    T = logits.shape[0]
    rt = math.gcd(T, ROUTER_TOKENS)
    nt = T // TM
    assert 2 * (rt // TM) <= LANES
    return pl.pallas_call(
        _router_kernel,
        grid=(T // rt,),
        in_specs=[pl.BlockSpec((rt, N_EXPERTS), lambda i: (i, 0))],
        out_specs=[pl.BlockSpec((rt, LANES), lambda i: (i, 0)),
                   pl.BlockSpec((rt // TM, 16, TM), lambda i: (i, 0, 0)),
                   pl.BlockSpec((rt // TM, N_EXPERTS, LANES), lambda i: (i, 0, 0))],
        out_shape=[jax.ShapeDtypeStruct((T, LANES), F32),
                   jax.ShapeDtypeStruct((nt, 16, TM), F32),
                   jax.ShapeDtypeStruct((nt, N_EXPERTS, LANES), I32)],
        compiler_params=pltpu.CompilerParams(dimension_semantics=("arbitrary",),
                                             vmem_limit_bytes=VMEM_LIMIT_BYTES),
        name="router",
    )(logits)


def _mixer(seqs, sink, xp, xs, mod3, cos_t, sin_t, kt, v, p):
    T = seqs.n_tokens
    nt = T // MT
    nb = T // BLOCK
    r = MT // BLOCK

    def mod_map(i, s):
        return (_tile_pos(seqs, i * MT)[2], 0, 0)

    def rope_map(i, s):
        return (_tile_pos(seqs, i * MT)[1] // MT, 0)

    const2 = lambda i, s: (0, 0)
    once = dict(pipeline_mode=pl.Buffered(1))
    in_specs = _group_x_specs(seqs, MT) + [
        pl.BlockSpec((1, 1, 6 * D_MODEL), mod_map),
        pl.BlockSpec((MT, LANES), rope_map),
        pl.BlockSpec((MT, LANES), rope_map),
        pl.BlockSpec((2 * LANES, BLOCK), lambda i, s: (0, jnp.maximum(i * r - 1, 0))),
        pl.BlockSpec((2 * LANES, MT), lambda i, s: (0, i)),
        pl.BlockSpec((2 * LANES, BLOCK), lambda i, s: (0, jnp.minimum(i * r + r, nb - 1))),
        pl.BlockSpec((BLOCK, 2 * LANES), lambda i, s: (jnp.maximum(i * r - 1, 0), 0)),
        pl.BlockSpec((MT, 2 * LANES), lambda i, s: (i, 0)),
        pl.BlockSpec((BLOCK, 2 * LANES), lambda i, s: (jnp.minimum(i * r + r, nb - 1), 0)),
        pl.BlockSpec(p["w_mix"].shape, const2, **once),
        pl.BlockSpec(p["b_mix"].shape, const2, **once),
        pl.BlockSpec(p["w_sp"].shape, lambda i, s: (0, 0, 0), **once),
        pl.BlockSpec(p["b_spt"].shape, const2, **once),
        pl.BlockSpec(p["sgu_g"].shape, const2, **once),
        pl.BlockSpec(p["sgu_b"].shape, const2, **once),
        pl.BlockSpec(p["w_ba"].shape, const2, **once),
        pl.BlockSpec(p["w_bs"].shape, const2, **once),
        pl.BlockSpec(p["w_out"].shape, const2, **once),
        pl.BlockSpec(p["ln1_g"].shape, const2, **once),
        pl.BlockSpec(p["ln1_b"].shape, const2, **once),
        pl.BlockSpec(p["w_r"].shape, const2, **once),
        pl.BlockSpec(p["b_r"].shape, const2, **once),
    ]
    out_specs = [
        pl.BlockSpec((MT, D_MODEL), lambda i, s: (i, 0)),
        pl.BlockSpec((MT, D_MODEL), lambda i, s: (i, 0)),
        pl.BlockSpec((MT, N_EXPERTS), lambda i, s: (i, 0)),
    ]
    out_shape = [
        jax.ShapeDtypeStruct((T, D_MODEL), F32),
        jax.ShapeDtypeStruct((T, D_MODEL), BF16),
        jax.ShapeDtypeStruct((T, N_EXPERTS), F32),
    ]
    return pl.pallas_call(
        functools.partial(_mixer_kernel, seqs),
        grid_spec=pltpu.PrefetchScalarGridSpec(
            num_scalar_prefetch=1, grid=(nt,), in_specs=in_specs, out_specs=out_specs),
        out_shape=out_shape,
        compiler_params=pltpu.CompilerParams(dimension_semantics=("arbitrary",),
                                             vmem_limit_bytes=VMEM_LIMIT_BYTES),
        name="mixer",
    )(sink, xp, xs, mod3, cos_t, sin_t, kt, kt, kt, v, v, v,
      p["w_mix"], p["b_mix"], p["w_sp"], p["b_spt"], p["sgu_g"], p["sgu_b"],
      p["w_ba"], p["w_bs"], p["w_out"], p["ln1_g"], p["ln1_b"], p["w_r"], p["b_r"])


def _pair_rows(row):
    return pl.multiple_of((row // PAIR) * ROW_TILES, ROW_TILES)


def _to_row_tiles(dst_ref, rows, n, start=0):
    words = pltpu.bitcast(rows, U32)
    base = start // PAIR * ROW_TILES
    for c in range(ROW_TILES):
        dst_ref[pl.ds(base + c, n // PAIR, stride=ROW_TILES), :] = words[:, c * LANES:(c + 1) * LANES]


def _from_row_tiles(src_ref, start, n):
    base = start // PAIR * ROW_TILES
    words = jnp.concatenate(
        [src_ref[pl.ds(base + c, n // PAIR, stride=ROW_TILES), :] for c in range(ROW_TILES)], axis=1)
    return pltpu.bitcast(words, BF16)


WAIT_GROUP = 16


def _issue_tile(copy, k0, src_ref, dst_ref, extra_ref, n_extra, step):
    for e in range(N_EXPERTS):
        copy(src_ref[k0 + e], dst_ref[k0 + e]).start()

    @pl.when(n_extra > 0)
    def _():
        def per_expert(e, c):
            def per_chunk(j, c2):
                copy(src_ref[k0 + e] + j * step, dst_ref[k0 + e] + j * step).start()
                return c2
            lax.fori_loop(1, extra_ref[k0 + e] + 1, per_chunk, 0)
            return c
        lax.fori_loop(0, N_EXPERTS, per_expert, 0)


def _wait_tile(group_copy, chunk_copy, n_extra):
    for _ in range(N_EXPERTS // WAIT_GROUP):
        group_copy.wait()

    def body(_, c):
        chunk_copy.wait()
        return c
    lax.fori_loop(0, n_extra, body, 0)


def _dispatch_kernel(b_ref, s_ref, x_ref, nx_ref, tail_ref, h2_ref, rrow_ref, xin_ref, stg_ref, zero_ref, sem):
    i = pl.program_id(0)
    nt = pl.num_programs(0)
    slot = i % 2
    rr = rrow_ref[0]
    pos = [rr[TOP_K + k:TOP_K + k + 1].astype(I32) for k in range(TOP_K)]
    for a in range(DISP_ROWS // DISP_BLOCK):
        rho = lax.broadcasted_iota(I32, (DISP_BLOCK, TM), 0) + a * DISP_BLOCK
        pt = jnp.zeros((DISP_BLOCK, TM), F32)
        for k in range(TOP_K):
            pt = pt + jnp.where(rho == pos[k], 1.0, 0.0)
        rows = jnp.dot(pt.astype(BF16), h2_ref[...], preferred_element_type=F32).astype(BF16)
        _to_row_tiles(stg_ref.at[slot], rows, DISP_BLOCK, a * DISP_BLOCK)

    chunk = DISP_CHUNK // PAIR * ROW_TILES

    def copy(src_row, dst_row, sl):
        return pltpu.make_async_copy(stg_ref.at[sl, pl.ds(_pair_rows(src_row), chunk)],
                                     xin_ref.at[pl.ds(_pair_rows(dst_row), chunk)], sem)

    def wait_tile(tile):
        group = pltpu.make_async_copy(stg_ref.at[0, pl.ds(0, WAIT_GROUP * chunk)],
                                      xin_ref.at[pl.ds(0, WAIT_GROUP * chunk)], sem)
        _wait_tile(group, copy(0, 0, 0), nx_ref[tile])

    @pl.when(i > 0)
    def _():
        wait_tile(i - 1)
    _issue_tile(lambda src, dst: copy(src, dst, slot), i * N_EXPERTS, b_ref, s_ref, x_ref, nx_ref[i], DISP_CHUNK)

    @pl.when(i == nt - 1)
    def _():
        wait_tile(i)
        zero_ref[...] = jnp.zeros_like(zero_ref)

        def zcopy(dst_row):
            return pltpu.make_async_copy(zero_ref, xin_ref.at[pl.ds(_pair_rows(dst_row), chunk)], sem)

        def zwait(count):
            def body(_, c):
                zcopy(0).wait()
                return c
            lax.fori_loop(0, count, body, 0)

        def per_expert(e, total):
            lo = tail_ref[e]
            nz = (tail_ref[N_EXPERTS + e] - lo) // DISP_CHUNK

            def per_chunk(j, c):
                zcopy(lo + j * DISP_CHUNK).start()
                return c
            lax.fori_loop(0, nz, per_chunk, 0)
            return total + nz
        zwait(lax.fori_loop(0, N_EXPERTS, per_expert, 0))

        def last_chunk(e, c):
            zcopy(tail_ref[N_EXPERTS + e] - DISP_CHUNK).start()
            return c
        lax.fori_loop(0, N_EXPERTS, last_chunk, 0)
        zwait(N_EXPERTS)


def _dispatch(seqs, b_te, s_te, x_te, nx_t, tails, h2, rrow, n_rows):
    nt = seqs.n_tokens // TM
    return pl.pallas_call(
        _dispatch_kernel,
        grid_spec=pltpu.PrefetchScalarGridSpec(
            num_scalar_prefetch=5, grid=(nt,),
            in_specs=[pl.BlockSpec((TM, D_MODEL), lambda i, *_: (i, 0)),
                      pl.BlockSpec((1, 16, TM), lambda i, *_: (i, 0, 0))],
            out_specs=pl.BlockSpec(memory_space=pl.ANY),
            scratch_shapes=[pltpu.VMEM((2, DISP_ROWS // PAIR * ROW_TILES, LANES), U32),
                            pltpu.VMEM((DISP_CHUNK // PAIR * ROW_TILES, LANES), U32),
                            pltpu.SemaphoreType.DMA]),
        out_shape=jax.ShapeDtypeStruct((n_rows // PAIR * ROW_TILES, LANES), U32),
        compiler_params=pltpu.CompilerParams(dimension_semantics=("arbitrary",),
                                             vmem_limit_bytes=VMEM_LIMIT_BYTES),
        name="dispatch",
    )(b_te, s_te, x_te, nx_t, tails, h2, rrow)


def _expert_kernel(be_ref, bv_ref, x_ref, wup_hbm, bup_ref, wdn_hbm, bdn_ref, o_ref,
                   wup_f, wdn_f, wup_b, wdn_b, sem):
    i = pl.program_id(0)
    e = be_ref[i]
    first = jnp.logical_or(i == 0, e != be_ref[jnp.maximum(i - 1, 0)])

    def fetch(ex):
        slot = ex % 2
        return (pltpu.make_async_copy(wup_hbm.at[ex], wup_f.at[slot], sem.at[0, slot]),
                pltpu.make_async_copy(wdn_hbm.at[ex], wdn_f.at[slot], sem.at[1, slot]))

    @pl.when(i == 0)
    def _():
        for d in fetch(e):
            d.start()

    @pl.when(first)
    def _():
        for d in fetch(e):
            d.wait()

        @pl.when(e + 1 < N_EXPERTS)
        def _():
            for d in fetch(e + 1):
                d.start()

        @pl.when(bv_ref[i * EXPERT_SUBS] > 0)
        def _():
            wup_b[...] = wup_f[e % 2].astype(BF16)
            wdn_b[...] = wdn_f[e % 2].astype(BF16)

    def ffn(start, n, valid):
        x = _from_row_tiles(x_ref, start, n)
        row = lax.broadcasted_iota(I32, (n, 1), 0)
        xb = jnp.where(row < valid, x, jnp.zeros_like(x))
        hu = jnp.dot(xb, wup_b[...], preferred_element_type=F32) + bup_ref[0]
        gate = jnp.minimum(hu[:, 0:D_FF], SWIGLU_LIMIT)
        lin = jnp.clip(hu[:, D_FF:], -SWIGLU_LIMIT, SWIGLU_LIMIT)
        act = gate * jax.nn.sigmoid(SWIGLU_ALPHA * gate) * (lin + 1.0)
        y = jnp.dot(act.astype(BF16), wdn_b[...], preferred_element_type=F32) + bdn_ref[0]
        _to_row_tiles(o_ref, y.astype(BF16), n, start)

    def zero(start, n):
        o_ref[start // PAIR * ROW_TILES:(start + n) // PAIR * ROW_TILES, :] = jnp.zeros(
            (n // PAIR * ROW_TILES, LANES), U32)

    half = BM // 2
    for sub in range(EXPERT_SUBS):
        valid = bv_ref[i * EXPERT_SUBS + sub]
        start = sub * BM

        @pl.when(valid > half)
        def _():
            ffn(start, BM, valid)

        @pl.when(jnp.logical_and(valid > 0, valid <= half))
        def _():
            ffn(start, half, valid)
            zero(start + half, half)

        @pl.when(valid == 0)
        def _():
            zero(start, BM)


def _experts(blk_e, blk_v, xin, w_up, b_up, w_down, b_down, n_blk):
    gb = EXPERT_SUBS * BM
    return pl.pallas_call(
        _expert_kernel,
        grid_spec=pltpu.PrefetchScalarGridSpec(
            num_scalar_prefetch=2, grid=(n_blk,),
            in_specs=[pl.BlockSpec((gb // PAIR * ROW_TILES, LANES), lambda i, be, bv: (i, 0)),
                      pl.BlockSpec(memory_space=pl.ANY),
                      pl.BlockSpec((1, 1, 2 * D_FF), lambda i, be, bv: (be[i], 0, 0)),
                      pl.BlockSpec(memory_space=pl.ANY),
                      pl.BlockSpec((1, 1, D_MODEL), lambda i, be, bv: (be[i], 0, 0))],
            out_specs=pl.BlockSpec((gb // PAIR * ROW_TILES, LANES), lambda i, be, bv: (i, 0)),
            scratch_shapes=[pltpu.VMEM((2, D_MODEL, 2 * D_FF), F32),
                            pltpu.VMEM((2, D_FF, D_MODEL), F32),
                            pltpu.VMEM((D_MODEL, 2 * D_FF), BF16),
                            pltpu.VMEM((D_FF, D_MODEL), BF16),
                            pltpu.SemaphoreType.DMA((2, 2))]),
        out_shape=jax.ShapeDtypeStruct((n_blk * gb // PAIR * ROW_TILES, LANES), U32),
        compiler_params=pltpu.CompilerParams(dimension_semantics=("arbitrary",),
                                             vmem_limit_bytes=VMEM_LIMIT_BYTES),
        name="experts",
    )(blk_e, blk_v, xin, w_up, b_up.reshape(N_EXPERTS, 1, 2 * D_FF), w_down,
      b_down.reshape(N_EXPERTS, 1, D_MODEL))


COMB_KC = COMB_ROWS // MXU_DEPTH
COMB_KC_MIN = N_EXPERTS * COMB_CHUNK // MXU_DEPTH
assert COMB_KC_MIN * MXU_DEPTH == N_EXPERTS * COMB_CHUNK


def _combine_kernel(tile0, s_ref, b_ref, x_ref, nx_ref, rows_ref, x1_ref, rcol_ref, mod_ref, g_ref, bb_ref,
                    eo_ref, y_ref, stg_ref, acc_ref, sem):
    i = pl.program_id(0)
    nt = pl.num_programs(0)
    slot = i % 2
    chunk = COMB_CHUNK // PAIR * ROW_TILES

    def copy(src_row, dst_row, sl):
        return pltpu.make_async_copy(eo_ref.at[pl.ds(_pair_rows(src_row), chunk)],
                                     stg_ref.at[sl, pl.ds(_pair_rows(dst_row), chunk)], sem.at[sl])

    def issue(tile, sl):
        _issue_tile(lambda src, dst: copy(src, dst, sl), tile * N_EXPERTS, s_ref, b_ref, x_ref, nx_ref[tile],
                    COMB_CHUNK)

    @pl.when(i == 0)
    def _():
        stg_ref[...] = jnp.zeros_like(stg_ref)
        issue(tile0, 0)

    rc = rcol_ref[...]
    col_k = [rc[:, 2 * TOP_K + k:2 * TOP_K + k + 1].astype(I32) for k in range(TOP_K)]
    w_k = [rc[:, 3 * TOP_K + k:3 * TOP_K + k + 1] for k in range(TOP_K)]

    def weights(c):
        jl = lax.broadcasted_iota(I32, (TM, MXU_DEPTH), 1) + c * MXU_DEPTH
        pm = jnp.zeros((TM, MXU_DEPTH), F32)
        for k in range(TOP_K):
            pm = pm + jnp.where(jl == col_k[k], w_k[k], 0.0)
        return pm.astype(BF16)

    @pl.when(i + 1 < nt)
    def _():
        issue(tile0 + i + 1, 1 - slot)

    stg = stg_ref.at[slot]
    group = pltpu.make_async_copy(eo_ref.at[pl.ds(0, WAIT_GROUP * chunk)],
                                  stg.at[pl.ds(0, WAIT_GROUP * chunk)], sem.at[slot])
    _wait_tile(group, copy(0, 0, slot), nx_ref[tile0 + i])

    pm = jnp.concatenate([weights(c) for c in range(COMB_KC_MIN)], axis=1)
    acc_ref[...] = jnp.dot(pm, _from_row_tiles(stg, 0, COMB_KC_MIN * MXU_DEPTH), preferred_element_type=F32)
    used = (rows_ref[tile0 + i] + MXU_DEPTH - 1) // MXU_DEPTH
    for c in range(COMB_KC_MIN, COMB_KC):
        @pl.when(c < used)
        def _():
            acc_ref[...] += jnp.dot(weights(c), _from_row_tiles(stg, c * MXU_DEPTH, MXU_DEPTH),
                                    preferred_element_type=F32)

    mod = mod_ref[0]
    g2 = mod[:, 5 * D_MODEL:6 * D_MODEL]
    y_ref[...] = _layer_norm(DN_ALPHA * x1_ref[...] + g2 * acc_ref[...], g_ref[...], bb_ref[...])


def _combine(seqs, tile0, n_tiles, s_te, b_te, x_te, nx_t, rows_t, x1, rcol, mod3, ln2_g, ln2_b, eo):
    def mod_map(i, *_):
        return (_tile_pos(seqs, (i + tile0) * TM)[2], 0, 0)

    return pl.pallas_call(
        functools.partial(_combine_kernel, tile0),
        grid_spec=pltpu.PrefetchScalarGridSpec(
            num_scalar_prefetch=5, grid=(n_tiles,),
            in_specs=[pl.BlockSpec((TM, D_MODEL), lambda i, *_: (i + tile0, 0)),
                      pl.BlockSpec((TM, LANES), lambda i, *_: (i + tile0, 0)),
                      pl.BlockSpec((1, 1, 6 * D_MODEL), mod_map),
                      pl.BlockSpec((1, D_MODEL), lambda i, *_: (0, 0)),
                      pl.BlockSpec((1, D_MODEL), lambda i, *_: (0, 0)),
                      pl.BlockSpec(memory_space=pl.ANY)],
            out_specs=pl.BlockSpec((TM, D_MODEL), lambda i, *_: (i, 0)),
            scratch_shapes=[pltpu.VMEM((2, COMB_ROWS // PAIR * ROW_TILES, LANES), U32),
                            pltpu.VMEM((TM, D_MODEL), F32),
                            pltpu.SemaphoreType.DMA((2,))]),
        out_shape=jax.ShapeDtypeStruct((n_tiles * TM, D_MODEL), F32),
        compiler_params=pltpu.CompilerParams(dimension_semantics=("arbitrary",),
                                             vmem_limit_bytes=VMEM_LIMIT_BYTES),
        name="combine",
    )(s_te, b_te, x_te, nx_t, rows_t, x1, rcol, mod3, ln2_g, ln2_b, eo)


def _prep_params(w_in, b_in, w_spatial, b_spatial, sgu_ln_g, sgu_ln_b, w_br_attn, w_br_sgu, w_out,
                 ln1_g, ln1_b, w_router, b_router):
    q_end, k_end, v_end = ATT_W, ATT_W + KV_W, ATT_W + 2 * KV_W

    def dup(w, lo):
        h0, h1 = w[..., lo:lo + HEAD_DIM], w[..., lo + HEAD_DIM:lo + 2 * HEAD_DIM]
        return jnp.concatenate([h0, h0, h1, h1], axis=-1)

    w_kv = jnp.concatenate([dup(w_in, q_end), dup(w_in, k_end)], axis=1).astype(BF16)
    b_kv = jnp.concatenate([dup(b_in, q_end), dup(b_in, k_end)], axis=0).reshape(1, -1)
    w_mix = w_in.astype(BF16)
    b_mix = b_in.reshape(1, -1)
    w_hi = w_router.astype(BF16)
    w_lo = (w_router - w_hi.astype(F32)).astype(BF16)
    p = dict(
        w_mix=w_mix, b_mix=b_mix,
        w_sp=w_spatial.astype(BF16), b_spt=b_spatial.T,
        sgu_g=sgu_ln_g.reshape(1, -1), sgu_b=sgu_ln_b.reshape(1, -1),
        w_ba=w_br_attn.astype(BF16), w_bs=w_br_sgu.astype(BF16), w_out=w_out.astype(BF16),
        ln1_g=ln1_g.reshape(1, -1), ln1_b=ln1_b.reshape(1, -1),
        w_r=jnp.concatenate([w_hi, w_lo], axis=1), b_r=b_router.reshape(1, -1),
    )
    return w_kv, b_kv, p


def _layer(seqs, xp, xs, c_all, w_ada, b_ada, w_in, b_in, sink, sgu_ln_g, sgu_ln_b, w_spatial, b_spatial,
           w_br_attn, w_br_sgu, w_out, ln1_g, ln1_b, w_router, b_router, w_up, b_up, w_down, b_down,
           ln2_g, ln2_b):
    T = seqs.n_tokens
    nt = T // TM
    ntp = seqs.n_prompt // TM
    c_pad = jnp.zeros((8, D_MODEL), F32).at[:c_all.shape[0]].set(c_all)
    mod3 = _ada(c_pad, w_ada, b_ada).reshape(8, 1, 6 * D_MODEL)
    cos_t, sin_t = _rope_tables(max(seqs.prompt_len, seqs.sample_len))
    w_kv, b_kv, p = _prep_params(w_in, b_in, w_spatial, b_spatial, sgu_ln_g, sgu_ln_b, w_br_attn,
                                 w_br_sgu, w_out, ln1_g, ln1_b, w_router, b_router)
    kt, v = _kv(seqs, xp, xs, mod3, cos_t, sin_t, w_kv, b_kv)
    x1, h2, logits = _mixer(seqs, sink, xp, xs, mod3, cos_t, sin_t, kt, v, p)
    rcol, rrow, cnt3 = _router(logits)

    cnt = cnt3[:, :, 0]
    count = cnt.sum(0)
    gb = EXPERT_SUBS * BM
    reg = (count + REGION_SLACK + gb - 1) // gb * gb
    pad_end = jnp.cumsum(reg)
    pad_start = pad_end - reg
    flat = lambda a: a.reshape(-1).astype(I32)
    s_te = flat(pad_start[None, :] + jnp.cumsum(cnt, axis=0) - cnt)
    d_nch = jnp.maximum((cnt + DISP_CHUNK - 1) // DISP_CHUNK, 1)
    c_nch = jnp.maximum((cnt + COMB_CHUNK - 1) // COMB_CHUNK, 1)
    db_te = flat(jnp.cumsum(cnt, axis=1) - cnt)
    cb_te = flat(COMB_CHUNK * (jnp.cumsum(c_nch, axis=1) - c_nch))
    rows_t = flat(COMB_CHUNK * c_nch.sum(1))
    n_blk = (nt * TILE_ROWS + N_EXPERTS * (REGION_SLACK + gb - 1)) // gb + 1
    tails = jnp.concatenate([pad_start + count, pad_end[:-1], jnp.array([n_blk * gb])]).astype(I32)
    blk_e = jnp.minimum(((jnp.arange(n_blk, dtype=I32) * gb)[:, None] >= pad_end[None, :]).sum(1),
                        N_EXPERTS - 1).astype(I32)
    sub_start = jnp.arange(n_blk * EXPERT_SUBS, dtype=I32) * BM
    owner = jnp.repeat(blk_e, EXPERT_SUBS)[:, None] == jnp.arange(N_EXPERTS)[None, :]
    sub_rows = (owner * (pad_start + count)[None, :]).sum(1) - sub_start
    blk_v = jnp.where(sub_start < pad_end[-1], jnp.clip(sub_rows, 0, BM), 0).astype(I32)

    xin = _dispatch(seqs, db_te, s_te, flat(d_nch - 1), flat((d_nch - 1).sum(1)), tails, h2, rrow, n_blk * gb)
    eo = _experts(blk_e, blk_v, xin, w_up, b_up, w_down, b_down, n_blk)
    comb = (s_te, cb_te, flat(c_nch - 1), flat((c_nch - 1).sum(1)), rows_t,
            x1, rcol, mod3, ln2_g.reshape(1, -1), ln2_b.reshape(1, -1), eo)
    return _combine(seqs, 0, ntp, *comb), _combine(seqs, ntp, nt - ntp, *comb)


def kernel(x_prompt, x_sample, c_prompt, c_sample, w_ada, b_ada, w_in, b_in, sink, sgu_ln_g, sgu_ln_b, w_spatial, b_spatial, w_br_attn, w_br_sgu, w_out, ln1_g, ln1_b, w_router, b_router, w_up, b_up, w_down, b_down, ln2_g, ln2_b):
    assert w_ada.shape[0] == DEPTH == 1
    bp, sp, d = x_prompt.shape
    bs, ss, _ = x_sample.shape
    seqs = _Seqs(n_prompt=bp * sp, prompt_len=sp, sample_len=ss, n_tokens=bp * sp + bs * ss)
    c_all = jnp.concatenate([c_prompt, c_sample], axis=0)
    yp, ys = _layer(seqs, x_prompt.reshape(bp * sp, d), x_sample.reshape(bs * ss, d), c_all,
                    w_ada[0], b_ada[0], w_in[0], b_in[0], sink[0], sgu_ln_g[0], sgu_ln_b[0],
                    w_spatial[0], b_spatial[0], w_br_attn[0], w_br_sgu[0], w_out[0], ln1_g[0], ln1_b[0],
                    w_router[0], b_router[0], w_up[0], b_up[0], w_down[0], b_down[0], ln2_g[0], ln2_b[0])
    return (yp.reshape(bp, sp, d), ys.reshape(bs, ss, d))
```

```python
import functools
import math
from typing import NamedTuple

import jax
import jax.numpy as jnp
from jax import lax
from jax.experimental import pallas as pl
from jax.experimental.pallas import tpu as pltpu

F32 = jnp.float32
BF16 = jnp.bfloat16
I32 = jnp.int32

D_MODEL = 1024
N_HEADS = 8
N_KV_HEADS = 2
HEAD_DIM = 64
ATT_W = N_HEADS * HEAD_DIM
KV_W = N_KV_HEADS * HEAD_DIM
BLOCK = 128
ROPE_THETA = 500000.0
ROT_DIM = HEAD_DIM // 4
ROT_HALF = ROT_DIM // 2
SGU_W = D_MODEL // 2
SGU_GROUPS = 4
N_EXPERTS = 32
TOP_K = 4
D_FF = D_MODEL
SWIGLU_LIMIT = 7.0
SWIGLU_ALPHA = 1.702
LN_EPS = 1e-5
DEPTH = 1
DN_ALPHA = (2 * DEPTH) ** 0.25

LANES = 128
MXU_DEPTH = 256
ROW_TILES = D_MODEL // LANES
VMEM_LIMIT_BYTES = 56 * 1024 * 1024
U32 = jnp.uint32
PAIR = 2

TM = 256
MT = 512
TK = 1024
BM = 512
EXPERT_SUBS = 2
ROUTER_TOKENS = 2048
DISP_CHUNK = 48
COMB_CHUNK = 48
REGION_SLACK = DISP_CHUNK
TILE_ROWS = TM * TOP_K + N_EXPERTS
DISP_ROWS = TILE_ROWS + DISP_CHUNK
DISP_BLOCK = DISP_ROWS // 3
assert DISP_BLOCK * 3 == DISP_ROWS and DISP_BLOCK % 16 == 0
COMB_ROWS = -(-(TILE_ROWS + N_EXPERTS * COMB_CHUNK) // MXU_DEPTH) * MXU_DEPTH
NEG_INF = float("-inf")


class _Seqs(NamedTuple):
    n_prompt: int
    prompt_len: int
    sample_len: int
    n_tokens: int


def _tile_pos(seqs, t0):
    is_s = t0 >= seqs.n_prompt
    seq_len = jnp.where(is_s, seqs.sample_len, seqs.prompt_len)
    off = jnp.where(is_s, t0 - seqs.n_prompt, t0)
    pos0 = off % seq_len
    row = jnp.where(is_s, seqs.n_prompt // seqs.prompt_len + off // seq_len, off // seq_len)
    return seq_len, pos0, row


def _ada_kernel(c_ref, w_ref, b_ref, o_ref):
    c = c_ref[...]
    a = c * jax.nn.sigmoid(c)
    o_ref[...] = jnp.dot(a, w_ref[...], preferred_element_type=F32,
                         precision=lax.Precision.HIGHEST) + b_ref[...]


def _ada(c_pad, w_ada, b_ada):
    n = w_ada.shape[1]
    bn = n // 4
    return pl.pallas_call(
        _ada_kernel,
        grid=(n // bn,),
        in_specs=[pl.BlockSpec((8, D_MODEL), lambda j: (0, 0)),
                  pl.BlockSpec((D_MODEL, bn), lambda j: (0, j)),
                  pl.BlockSpec((1, bn), lambda j: (0, j))],
        out_specs=pl.BlockSpec((8, bn), lambda j: (0, j)),
        out_shape=jax.ShapeDtypeStruct((8, n), F32),
        compiler_params=pltpu.CompilerParams(vmem_limit_bytes=VMEM_LIMIT_BYTES),
        name="ada",
    )(c_pad, w_ada, b_ada.reshape(1, n))


def _rope_tables(length):
    inv = ROPE_THETA ** (-jnp.arange(ROT_HALF, dtype=F32) * 2.0 / ROT_DIM)
    ang = jnp.arange(length, dtype=F32)[:, None] * inv[None, :]
    lane = jnp.arange(LANES) % HEAD_DIM
    spread = ((lane[None, :] % ROT_HALF == jnp.arange(ROT_HALF)[:, None]) & (lane[None, :] < ROT_DIM)).astype(F32)
    sign = jnp.where(lane < ROT_HALF, -1.0, 1.0)
    exact = dict(precision=lax.Precision.HIGHEST, preferred_element_type=F32)
    cos = jnp.dot(jnp.cos(ang), spread, **exact) + (lane >= ROT_DIM).astype(F32)[None, :]
    sin = jnp.dot(jnp.sin(ang), spread * sign[None, :], **exact)
    return cos, sin


def _rope(x, cos, sin):
    n = x.shape[1]
    reps = n // LANES
    c = jnp.concatenate([cos] * reps, axis=1)
    s = jnp.concatenate([sin] * reps, axis=1)
    lane = lax.broadcasted_iota(I32, x.shape, 1)
    first = (lane & (HEAD_DIM - 1)) < ROT_HALF
    partner = jnp.where(first, pltpu.roll(x, n - ROT_HALF, 1), pltpu.roll(x, ROT_HALF, 1))
    return x * c + partner * s


def _kv_kernel(seqs, xp_ref, xs_ref, mod_ref, cos_ref, sin_ref, w_ref, b_ref, kt_ref, v_ref):
    mod = mod_ref[0]
    sh1 = mod[:, 0:D_MODEL]
    sc1 = mod[:, D_MODEL:2 * D_MODEL]
    x = jnp.where(pl.program_id(0) * TK >= seqs.n_prompt, xs_ref[...], xp_ref[...])
    h = (x * (1.0 + sc1) + sh1).astype(BF16)
    kv = jnp.dot(h, w_ref[...], preferred_element_type=F32) + b_ref[...]
    k = _rope(kv[:, 0:2 * LANES], cos_ref[...], sin_ref[...])
    kt_ref[...] = k.T.astype(BF16)
    v_ref[...] = kv[:, 2 * LANES:4 * LANES].astype(BF16)


def _group_x_specs(seqs, tile):
    ntp = seqs.n_prompt // tile
    return [pl.BlockSpec((tile, D_MODEL), lambda i, *_: (jnp.minimum(i, ntp - 1), 0)),
            pl.BlockSpec((tile, D_MODEL), lambda i, *_: (jnp.maximum(i - ntp, 0), 0))]


def _kv(seqs, xp, xs, mod3, cos_t, sin_t, w_kv, b_kv):
    T = seqs.n_tokens

    def mod_map(i):
        return (_tile_pos(seqs, i * TK)[2], 0, 0)

    def rope_map(i):
        return (_tile_pos(seqs, i * TK)[1] // TK, 0)

    return pl.pallas_call(
        functools.partial(_kv_kernel, seqs),
        grid=(T // TK,),
        in_specs=_group_x_specs(seqs, TK) + [
                  pl.BlockSpec((1, 1, 6 * D_MODEL), mod_map),
                  pl.BlockSpec((TK, LANES), rope_map),
                  pl.BlockSpec((TK, LANES), rope_map),
                  pl.BlockSpec((D_MODEL, 4 * LANES), lambda i: (0, 0)),
                  pl.BlockSpec((1, 4 * LANES), lambda i: (0, 0))],
        out_specs=[pl.BlockSpec((2 * LANES, TK), lambda i: (0, i)),
                   pl.BlockSpec((TK, 2 * LANES), lambda i: (i, 0))],
        out_shape=[jax.ShapeDtypeStruct((2 * LANES, T), BF16),
                   jax.ShapeDtypeStruct((T, 2 * LANES), BF16)],
        compiler_params=pltpu.CompilerParams(dimension_semantics=("arbitrary",),
                                             vmem_limit_bytes=VMEM_LIMIT_BYTES),
        name="kv",
    )(xp, xs, mod3, cos_t, sin_t, w_kv, b_kv)


def _layer_norm(x, g, b):
    mu = jnp.mean(x, axis=-1, keepdims=True)
    xc = x - mu
    var = jnp.mean(xc * xc, axis=-1, keepdims=True)
    return xc * lax.rsqrt(var + LN_EPS) * g + b


def _attention(q, kfull, vfull, valids, sink_ref):
    lane = lax.broadcasted_iota(I32, (BLOCK, LANES), 1)
    lo = lane < HEAD_DIM
    ones = jnp.ones((3 * BLOCK, LANES), BF16)
    units = [(jb, hk) for jb in range(MT // BLOCK) for hk in range(N_KV_HEADS)]
    scores, sinks = [], []
    for jb, hk in units:
        parts = []
        for p in range(2):
            qp = q[jb * BLOCK:(jb + 1) * BLOCK, (2 * hk + p) * LANES:(2 * hk + p + 1) * LANES]
            parts.append(jnp.where(lo, qp, 0.0).astype(BF16))
            parts.append(jnp.where(lo, 0.0, qp).astype(BF16))
        kwin = kfull[hk * LANES:(hk + 1) * LANES, jb * BLOCK:(jb + 3) * BLOCK]
        s = jnp.dot(jnp.concatenate(parts, axis=0), kwin, preferred_element_type=F32)
        scores.append(jnp.where(jnp.concatenate([valids[jb]] * 4, axis=0), s, NEG_INF))
        sinks.extend(jnp.full((BLOCK, 1), sink_ref[hk * 4 + g], F32) for g in range(4))
    s = jnp.concatenate(scores, axis=0)
    sk = jnp.concatenate(sinks, axis=0)
    m = jnp.maximum(jnp.max(s, axis=-1, keepdims=True), sk)
    p = jnp.exp(s - m).astype(BF16)
    sink_term = jnp.exp(sk - m)
    rows = []
    for ui, (jb, hk) in enumerate(units):
        vwin = jnp.concatenate([vfull[jb * BLOCK:(jb + 3) * BLOCK, hk * LANES:(hk + 1) * LANES], ones], axis=1)
        r0 = ui * 4 * BLOCK
        ov = jnp.dot(p[r0:r0 + 4 * BLOCK], vwin, preferred_element_type=F32)
        o = ov[:, 0:LANES] / (ov[:, LANES:2 * LANES] + sink_term[r0:r0 + 4 * BLOCK])
        pair = [jnp.where(lo, o[(2 * p2) * BLOCK:(2 * p2 + 1) * BLOCK],
                          o[(2 * p2 + 1) * BLOCK:(2 * p2 + 2) * BLOCK]) for p2 in range(2)]
        rows.append(jnp.concatenate(pair, axis=1))
    n_h = N_KV_HEADS
    return jnp.concatenate(
        [jnp.concatenate(rows[jb * n_h:(jb + 1) * n_h], axis=1) for jb in range(MT // BLOCK)], axis=0)


def _mixer_kernel(seqs, sink_ref, xp_ref, xs_ref, mod_ref, cos_ref, sin_ref,
                  ktp_ref, ktc_ref, ktn_ref, vp_ref, vc_ref, vn_ref,
                  wmix_ref, bmix_ref, wsp_ref, bspt_ref, sg_ref, sb_ref,
                  wba_ref, wbs_ref, wout_ref, l1g_ref, l1b_ref, wr_ref, br_ref,
                  x1_ref, h2_ref, logit_ref):
    i = pl.program_id(0)
    seq_len, pos0, _ = _tile_pos(seqs, i * MT)
    mod = mod_ref[0]
    sh1, sc1, g1 = (mod[:, j * D_MODEL:(j + 1) * D_MODEL] for j in range(3))
    sh2, sc2 = (mod[:, j * D_MODEL:(j + 1) * D_MODEL] for j in range(3, 5))
    x = jnp.where(i * MT >= seqs.n_prompt, xs_ref[...], xp_ref[...])
    h = (x * (1.0 + sc1) + sh1).astype(BF16)
    kv_end = ATT_W + 2 * KV_W
    zq = jnp.dot(h, wmix_ref[:, 0:ATT_W], preferred_element_type=F32) + bmix_ref[:, 0:ATT_W]
    z = jnp.dot(h, wmix_ref[:, kv_end:], preferred_element_type=F32) + bmix_ref[:, kv_end:]
    q = _rope(zq, cos_ref[...], sin_ref[...]) * (HEAD_DIM ** -0.5)
    u = jax.nn.gelu(z[:, 0:SGU_W])
    vs = _layer_norm(jax.nn.gelu(z[:, SGU_W:2 * SGU_W]), sg_ref[...], sb_ref[...])
    ga = z[:, 2 * SGU_W:2 * SGU_W + D_MODEL]
    gs = z[:, 2 * SGU_W + D_MODEL:]

    kfull = jnp.concatenate([ktp_ref[...], ktc_ref[...], ktn_ref[...]], axis=1)
    vfull = jnp.concatenate([vp_ref[...], vc_ref[...], vn_ref[...]], axis=0)
    qi = lax.broadcasted_iota(I32, (BLOCK, 3 * BLOCK), 0)
    ki = lax.broadcasted_iota(I32, (BLOCK, 3 * BLOCK), 1)
    band = (ki >= qi) & (ki <= qi + 2 * BLOCK)
    vs_b = vs.astype(BF16)
    valids, sgu_rows = [], []
    for jb in range(MT // BLOCK):
        posb = pos0 + jb * BLOCK
        valids.append(band & (ki >= jnp.where(posb == 0, BLOCK, 0))
                      & (ki < jnp.where(posb + BLOCK == seq_len, 2 * BLOCK, 3 * BLOCK)))
        groups = []
        for g in range(SGU_GROUPS):
            vg = vs_b[jb * BLOCK:(jb + 1) * BLOCK, g * LANES:(g + 1) * LANES]
            sv = jnp.dot(wsp_ref[g], vg, preferred_element_type=F32) + bspt_ref[:, g:g + 1]
            groups.append(sv)
        sgu_rows.append(jnp.concatenate(groups, axis=1))
    attn = _attention(q, kfull, vfull, valids, sink_ref)
    sgu = u * jnp.concatenate(sgu_rows, axis=0)

    a1 = jnp.dot(attn.astype(BF16), wba_ref[...], preferred_element_type=F32)
    a2 = jnp.dot(sgu.astype(BF16), wbs_ref[...], preferred_element_type=F32)
    merged = jax.nn.sigmoid(ga) * a1 + jax.nn.sigmoid(gs) * a2
    mix = jnp.dot(merged.astype(BF16), wout_ref[...], preferred_element_type=F32)
    x1 = _layer_norm(DN_ALPHA * x + g1 * mix, l1g_ref[...], l1b_ref[...])
    x1_ref[...] = x1
    h2 = x1 * (1.0 + sc2) + sh2
    hi = h2.astype(BF16)
    h2_ref[...] = hi

    lo_part = (h2 - hi.astype(F32)).astype(BF16)
    l1 = jnp.dot(hi, wr_ref[...], preferred_element_type=F32)
    l2 = jnp.dot(lo_part, wr_ref[:, 0:N_EXPERTS], preferred_element_type=F32)
    logit_ref[...] = l1[:, 0:N_EXPERTS] + l1[:, N_EXPERTS:2 * N_EXPERTS] + l2 + br_ref[...]


PACK = 4096.0
assert DISP_ROWS <= PACK and COMB_ROWS <= PACK and DISP_ROWS * PACK < 2 ** 24


def _router_kernel(l_ref, rcol_ref, rrow_ref, cnt_ref):
    rt = l_ref.shape[0]
    ns = rt // TM
    padded = jnp.concatenate([l_ref[...], jnp.zeros((rt, LANES - N_EXPERTS), F32)], axis=1)
    work = padded.T[0:N_EXPERTS]
    eidx = lax.broadcasted_iota(I32, (N_EXPERTS, rt), 0)
    idxs, vals = [], []
    for _ in range(TOP_K):
        m = jnp.max(work, axis=0, keepdims=True)
        ix = jnp.min(jnp.where(work == m, eidx, N_EXPERTS), axis=0, keepdims=True)
        idxs.append(ix)
        vals.append(m)
        work = jnp.where(eidx == ix, NEG_INF, work)
    exps = [jnp.exp(v - vals[0]) for v in vals]
    esum = exps[0] + exps[1] + exps[2] + exps[3]
    wts = [e / esum for e in exps]

    sel = jnp.zeros((N_EXPERTS, rt), F32)
    for ix in idxs:
        sel = sel + jnp.where(eidx == ix, 1.0, 0.0)
    sel_b = sel.astype(BF16)
    ti = lax.broadcasted_iota(I32, (TM, TM), 0)
    tj = lax.broadcasted_iota(I32, (TM, TM), 1)
    earlier = jnp.where(ti < tj, 1.0, 0.0).astype(BF16)
    tiles = [slice(s * TM, (s + 1) * TM) for s in range(ns)]
    ranks = [jnp.dot(sel_b[:, t], earlier, preferred_element_type=F32) for t in tiles]
    cnts, nchs = [], []
    for t in tiles:
        cnt = jnp.sum(sel[:, t], axis=1, keepdims=True)
        cnt = cnt + (cnt - 2.0 * jnp.floor(cnt * 0.5))
        cnts.append(cnt)
        nchs.append(jnp.maximum(jnp.floor((cnt + (COMB_CHUNK - 0.5)) * (1.0 / COMB_CHUNK)), 1.0))
    lane = lax.broadcasted_iota(I32, (N_EXPERTS, LANES), 1)
    pre = jnp.zeros((N_EXPERTS, LANES), F32)
    for j, col in enumerate(cnts + nchs):
        pre = jnp.where(lane == j, col, pre)
    ei = lax.broadcasted_iota(I32, (N_EXPERTS, N_EXPERTS), 0)
    ej = lax.broadcasted_iota(I32, (N_EXPERTS, N_EXPERTS), 1)
    before = jnp.where(ej < ei, 1.0, 0.0).astype(BF16)
    base = jnp.dot(before, pre.astype(BF16), preferred_element_type=F32)
    both = jnp.concatenate(
        [(ranks[s] + base[:, s:s + 1]) * PACK + (ranks[s] + base[:, ns + s:ns + s + 1] * COMB_CHUNK)
         for s in range(ns)], axis=1)

    packed = [jnp.sum(jnp.where(eidx == ix, both, 0.0), axis=0, keepdims=True) for ix in idxs]
    drow = [jnp.floor(v * (1.0 / PACK)) for v in packed]
    table = [ix.astype(F32) for ix in idxs] + drow + [v - d * PACK for v, d in zip(packed, drow)] + wts
    sub = lax.broadcasted_iota(I32, (LANES, rt), 0)
    rr = jnp.zeros((LANES, rt), F32)
    for j, row in enumerate(table):
        rr = jnp.where(sub == j, row, rr)
    rcol_ref[...] = rr.T
    for s in range(ns):
        rrow_ref[s] = rr[0:16, tiles[s]]
        cnt_ref[s] = jnp.broadcast_to(cnts[s], (N_EXPERTS, LANES)).astype(I32)


def _router(logits):
    T = logits.shape[0]
    rt = math.gcd(T, ROUTER_TOKENS)
    nt = T // TM
    assert 2 * (rt // TM) <= LANES
    return pl.pallas_call(
        _router_kernel,
        grid=(T // rt,),
        in_specs=[pl.BlockSpec((rt, N_EXPERTS), lambda i: (i, 0))],
        out_specs=[pl.BlockSpec((rt, LANES), lambda i: (i, 0)),
                   pl.BlockSpec((rt // TM, 16, TM), lambda i: (i, 0, 0)),
                   pl.BlockSpec((rt // TM, N_EXPERTS, LANES), lambda i: (i, 0, 0))],
        out_shape=[jax.ShapeDtypeStruct((T, LANES), F32),
                   jax.ShapeDtypeStruct((nt, 16, TM), F32),
                   jax.ShapeDtypeStruct((nt, N_EXPERTS, LANES), I32)],
        compiler_params=pltpu.CompilerParams(dimension_semantics=("arbitrary",),
                                             vmem_limit_bytes=VMEM_LIMIT_BYTES),
        name="router",
    )(logits)


def _mixer(seqs, sink, xp, xs, mod3, cos_t, sin_t, kt, v, p):
    T = seqs.n_tokens
    nt = T // MT
    nb = T // BLOCK
    r = MT // BLOCK

    def mod_map(i, s):
        return (_tile_pos(seqs, i * MT)[2], 0, 0)

    def rope_map(i, s):
        return (_tile_pos(seqs, i * MT)[1] // MT, 0)

    const2 = lambda i, s: (0, 0)
    once = dict(pipeline_mode=pl.Buffered(1))
    in_specs = _group_x_specs(seqs, MT) + [
        pl.BlockSpec((1, 1, 6 * D_MODEL), mod_map),
        pl.BlockSpec((MT, LANES), rope_map),
        pl.BlockSpec((MT, LANES), rope_map),
        pl.BlockSpec((2 * LANES, BLOCK), lambda i, s: (0, jnp.maximum(i * r - 1, 0))),
        pl.BlockSpec((2 * LANES, MT), lambda i, s: (0, i)),
        pl.BlockSpec((2 * LANES, BLOCK), lambda i, s: (0, jnp.minimum(i * r + r, nb - 1))),
        pl.BlockSpec((BLOCK, 2 * LANES), lambda i, s: (jnp.maximum(i * r - 1, 0), 0)),
        pl.BlockSpec((MT, 2 * LANES), lambda i, s: (i, 0)),
        pl.BlockSpec((BLOCK, 2 * LANES), lambda i, s: (jnp.minimum(i * r + r, nb - 1), 0)),
        pl.BlockSpec(p["w_mix"].shape, const2, **once),
        pl.BlockSpec(p["b_mix"].shape, const2, **once),
        pl.BlockSpec(p["w_sp"].shape, lambda i, s: (0, 0, 0), **once),
        pl.BlockSpec(p["b_spt"].shape, const2, **once),
        pl.BlockSpec(p["sgu_g"].shape, const2, **once),
        pl.BlockSpec(p["sgu_b"].shape, const2, **once),
        pl.BlockSpec(p["w_ba"].shape, const2, **once),
        pl.BlockSpec(p["w_bs"].shape, const2, **once),
        pl.BlockSpec(p["w_out"].shape, const2, **once),
        pl.BlockSpec(p["ln1_g"].shape, const2, **once),
        pl.BlockSpec(p["ln1_b"].shape, const2, **once),
        pl.BlockSpec(p["w_r"].shape, const2, **once),
        pl.BlockSpec(p["b_r"].shape, const2, **once),
    ]
    out_specs = [
        pl.BlockSpec((MT, D_MODEL), lambda i, s: (i, 0)),
        pl.BlockSpec((MT, D_MODEL), lambda i, s: (i, 0)),
        pl.BlockSpec((MT, N_EXPERTS), lambda i, s: (i, 0)),
    ]
    out_shape = [
        jax.ShapeDtypeStruct((T, D_MODEL), F32),
        jax.ShapeDtypeStruct((T, D_MODEL), BF16),
        jax.ShapeDtypeStruct((T, N_EXPERTS), F32),
    ]
    return pl.pallas_call(
        functools.partial(_mixer_kernel, seqs),
        grid_spec=pltpu.PrefetchScalarGridSpec(
            num_scalar_prefetch=1, grid=(nt,), in_specs=in_specs, out_specs=out_specs),
        out_shape=out_shape,
        compiler_params=pltpu.CompilerParams(dimension_semantics=("arbitrary",),
                                             vmem_limit_bytes=VMEM_LIMIT_BYTES),
        name="mixer",
    )(sink, xp, xs, mod3, cos_t, sin_t, kt, kt, kt, v, v, v,
      p["w_mix"], p["b_mix"], p["w_sp"], p["b_spt"], p["sgu_g"], p["sgu_b"],
      p["w_ba"], p["w_bs"], p["w_out"], p["ln1_g"], p["ln1_b"], p["w_r"], p["b_r"])


def _pair_rows(row):
    return pl.multiple_of((row // PAIR) * ROW_TILES, ROW_TILES)


def _to_row_tiles(dst_ref, rows, n, start=0):
    words = pltpu.bitcast(rows, U32)
    base = start // PAIR * ROW_TILES
    for c in range(ROW_TILES):
        dst_ref[pl.ds(base + c, n // PAIR, stride=ROW_TILES), :] = words[:, c * LANES:(c + 1) * LANES]


def _from_row_tiles(src_ref, start, n):
    base = start // PAIR * ROW_TILES
    words = jnp.concatenate(
        [src_ref[pl.ds(base + c, n // PAIR, stride=ROW_TILES), :] for c in range(ROW_TILES)], axis=1)
    return pltpu.bitcast(words, BF16)


WAIT_GROUP = 16


def _issue_tile(copy, k0, src_ref, dst_ref, extra_ref, n_extra, step):
    for e in range(N_EXPERTS):
        copy(src_ref[k0 + e], dst_ref[k0 + e]).start()

    @pl.when(n_extra > 0)
    def _():
        def per_expert(e, c):
            def per_chunk(j, c2):
                copy(src_ref[k0 + e] + j * step, dst_ref[k0 + e] + j * step).start()
                return c2
            lax.fori_loop(1, extra_ref[k0 + e] + 1, per_chunk, 0)
            return c
        lax.fori_loop(0, N_EXPERTS, per_expert, 0)


def _wait_tile(group_copy, chunk_copy, n_extra):
    for _ in range(N_EXPERTS // WAIT_GROUP):
        group_copy.wait()

    def body(_, c):
        chunk_copy.wait()
        return c
    lax.fori_loop(0, n_extra, body, 0)


def _dispatch_kernel(b_ref, s_ref, x_ref, nx_ref, tail_ref, h2_ref, rrow_ref, xin_ref, stg_ref, zero_ref, sem):
    i = pl.program_id(0)
    nt = pl.num_programs(0)
    slot = i % 2
    chunk = DISP_CHUNK // PAIR * ROW_TILES

    def copy(src_row, dst_row, sl):
        return pltpu.make_async_copy(stg_ref.at[sl, pl.ds(_pair_rows(src_row), chunk)],
                                     xin_ref.at[pl.ds(_pair_rows(dst_row), chunk)], sem)

    def wait_tile(tile):
        group = pltpu.make_async_copy(stg_ref.at[0, pl.ds(0, WAIT_GROUP * chunk)],
                                      xin_ref.at[pl.ds(0, WAIT_GROUP * chunk)], sem)
        _wait_tile(group, copy(0, 0, 0), nx_ref[tile])

    def issue(tile, sl):
        _issue_tile(lambda src, dst: copy(src, dst, sl), tile * N_EXPERTS, b_ref, s_ref, x_ref, nx_ref[tile],
                    DISP_CHUNK)

    @pl.when(i > 1)
    def _():
        wait_tile(i - 2)

    @pl.when(i > 0)
    def _():
        issue(i - 1, 1 - slot)

    rr = rrow_ref[0]
    pos = [rr[TOP_K + k:TOP_K + k + 1].astype(I32) for k in range(TOP_K)]
    for a in range(DISP_ROWS // DISP_BLOCK):
        rho = lax.broadcasted_iota(I32, (DISP_BLOCK, TM), 0) + a * DISP_BLOCK
        pt = jnp.zeros((DISP_BLOCK, TM), F32)
        for k in range(TOP_K):
            pt = pt + jnp.where(rho == pos[k], 1.0, 0.0)
        rows = jnp.dot(pt.astype(BF16), h2_ref[...], preferred_element_type=F32).astype(BF16)
        _to_row_tiles(stg_ref.at[slot], rows, DISP_BLOCK, a * DISP_BLOCK)

    @pl.when(i == nt - 1)
    def _():
        @pl.when(i > 0)
        def _():
            wait_tile(i - 1)
        issue(i, slot)
        wait_tile(i)
        zero_ref[...] = jnp.zeros_like(zero_ref)

        def zcopy(dst_row):
            return pltpu.make_async_copy(zero_ref, xin_ref.at[pl.ds(_pair_rows(dst_row), chunk)], sem)

        def zwait(count):
            def body(_, c):
                zcopy(0).wait()
                return c
            lax.fori_loop(0, count, body, 0)

        def per_expert(e, total):
            lo = tail_ref[e]
            nz = (tail_ref[N_EXPERTS + e] - lo) // DISP_CHUNK

            def per_chunk(j, c):
                zcopy(lo + j * DISP_CHUNK).start()
                return c
            lax.fori_loop(0, nz, per_chunk, 0)
            return total + nz
        zwait(lax.fori_loop(0, N_EXPERTS, per_expert, 0))

        def last_chunk(e, c):
            zcopy(tail_ref[N_EXPERTS + e] - DISP_CHUNK).start()
            return c
        lax.fori_loop(0, N_EXPERTS, last_chunk, 0)
        zwait(N_EXPERTS)


def _dispatch(seqs, b_te, s_te, x_te, nx_t, tails, h2, rrow, n_rows):
    nt = seqs.n_tokens // TM
    return pl.pallas_call(
        _dispatch_kernel,
        grid_spec=pltpu.PrefetchScalarGridSpec(
            num_scalar_prefetch=5, grid=(nt,),
            in_specs=[pl.BlockSpec((TM, D_MODEL), lambda i, *_: (i, 0)),
                      pl.BlockSpec((1, 16, TM), lambda i, *_: (i, 0, 0))],
            out_specs=pl.BlockSpec(memory_space=pl.ANY),
            scratch_shapes=[pltpu.VMEM((2, DISP_ROWS // PAIR * ROW_TILES, LANES), U32),
                            pltpu.VMEM((DISP_CHUNK // PAIR * ROW_TILES, LANES), U32),
                            pltpu.SemaphoreType.DMA]),
        out_shape=jax.ShapeDtypeStruct((n_rows // PAIR * ROW_TILES, LANES), U32),
        compiler_params=pltpu.CompilerParams(dimension_semantics=("arbitrary",),
                                             vmem_limit_bytes=VMEM_LIMIT_BYTES),
        name="dispatch",
    )(b_te, s_te, x_te, nx_t, tails, h2, rrow)


def _expert_kernel(be_ref, bv_ref, x_ref, wup_hbm, bup_ref, wdn_hbm, bdn_ref, o_ref,
                   wup_f, wdn_f, wup_b, wdn_b, sem):
    i = pl.program_id(0)
    e = be_ref[i]
    first = jnp.logical_or(i == 0, e != be_ref[jnp.maximum(i - 1, 0)])

    def fetch(ex):
        slot = ex % 2
        return (pltpu.make_async_copy(wup_hbm.at[ex], wup_f.at[slot], sem.at[0, slot]),
                pltpu.make_async_copy(wdn_hbm.at[ex], wdn_f.at[slot], sem.at[1, slot]))

    @pl.when(i == 0)
    def _():
        for d in fetch(e):
            d.start()

    @pl.when(first)
    def _():
        for d in fetch(e):
            d.wait()

        @pl.when(e + 1 < N_EXPERTS)
        def _():
            for d in fetch(e + 1):
                d.start()

        @pl.when(bv_ref[i * EXPERT_SUBS] > 0)
        def _():
            wup_b[...] = wup_f[e % 2].astype(BF16)
            wdn_b[...] = wdn_f[e % 2].astype(BF16)

    def ffn(start, n, valid):
        x = _from_row_tiles(x_ref, start, n)
        row = lax.broadcasted_iota(I32, (n, 1), 0)
        xb = jnp.where(row < valid, x, jnp.zeros_like(x))
        hu = jnp.dot(xb, wup_b[...], preferred_element_type=F32) + bup_ref[0]
        gate = jnp.minimum(hu[:, 0:D_FF], SWIGLU_LIMIT)
        lin = jnp.clip(hu[:, D_FF:], -SWIGLU_LIMIT, SWIGLU_LIMIT)
        act = gate * jax.nn.sigmoid(SWIGLU_ALPHA * gate) * (lin + 1.0)
        y = jnp.dot(act.astype(BF16), wdn_b[...], preferred_element_type=F32) + bdn_ref[0]
        _to_row_tiles(o_ref, y.astype(BF16), n, start)

    def zero(start, n):
        o_ref[start // PAIR * ROW_TILES:(start + n) // PAIR * ROW_TILES, :] = jnp.zeros(
            (n // PAIR * ROW_TILES, LANES), U32)

    half = BM // 2
    for sub in range(EXPERT_SUBS):
        valid = bv_ref[i * EXPERT_SUBS + sub]
        start = sub * BM

        @pl.when(valid > half)
        def _():
            ffn(start, BM, valid)

        @pl.when(jnp.logical_and(valid > 0, valid <= half))
        def _():
            ffn(start, half, valid)
            zero(start + half, half)

        @pl.when(valid == 0)
        def _():
            zero(start, BM)


def _experts(blk_e, blk_v, xin, w_up, b_up, w_down, b_down, n_blk):
    gb = EXPERT_SUBS * BM
    return pl.pallas_call(
        _expert_kernel,
        grid_spec=pltpu.PrefetchScalarGridSpec(
            num_scalar_prefetch=2, grid=(n_blk,),
            in_specs=[pl.BlockSpec((gb // PAIR * ROW_TILES, LANES), lambda i, be, bv: (i, 0)),
                      pl.BlockSpec(memory_space=pl.ANY),
                      pl.BlockSpec((1, 1, 2 * D_FF), lambda i, be, bv: (be[i], 0, 0)),
                      pl.BlockSpec(memory_space=pl.ANY),
                      pl.BlockSpec((1, 1, D_MODEL), lambda i, be, bv: (be[i], 0, 0))],
            out_specs=pl.BlockSpec((gb // PAIR * ROW_TILES, LANES), lambda i, be, bv: (i, 0)),
            scratch_shapes=[pltpu.VMEM((2, D_MODEL, 2 * D_FF), F32),
                            pltpu.VMEM((2, D_FF, D_MODEL), F32),
                            pltpu.VMEM((D_MODEL, 2 * D_FF), BF16),
                            pltpu.VMEM((D_FF, D_MODEL), BF16),
                            pltpu.SemaphoreType.DMA((2, 2))]),
        out_shape=jax.ShapeDtypeStruct((n_blk * gb // PAIR * ROW_TILES, LANES), U32),
        compiler_params=pltpu.CompilerParams(dimension_semantics=("arbitrary",),
                                             vmem_limit_bytes=VMEM_LIMIT_BYTES),
        name="experts",
    )(blk_e, blk_v, xin, w_up, b_up.reshape(N_EXPERTS, 1, 2 * D_FF), w_down,
      b_down.reshape(N_EXPERTS, 1, D_MODEL))


COMB_KC = COMB_ROWS // MXU_DEPTH
COMB_KC_MIN = N_EXPERTS * COMB_CHUNK // MXU_DEPTH
assert COMB_KC_MIN * MXU_DEPTH == N_EXPERTS * COMB_CHUNK


def _combine_kernel(tile0, s_ref, b_ref, x_ref, nx_ref, rows_ref, x1_ref, rcol_ref, mod_ref, g_ref, bb_ref,
                    eo_ref, y_ref, stg_ref, acc_ref, sem):
    i = pl.program_id(0)
    nt = pl.num_programs(0)
    slot = i % 2
    chunk = COMB_CHUNK // PAIR * ROW_TILES

    def copy(src_row, dst_row, sl):
        return pltpu.make_async_copy(eo_ref.at[pl.ds(_pair_rows(src_row), chunk)],
                                     stg_ref.at[sl, pl.ds(_pair_rows(dst_row), chunk)], sem.at[sl])

    def issue(tile, sl):
        _issue_tile(lambda src, dst: copy(src, dst, sl), tile * N_EXPERTS, s_ref, b_ref, x_ref, nx_ref[tile],
                    COMB_CHUNK)

    @pl.when(i == 0)
    def _():
        stg_ref[...] = jnp.zeros_like(stg_ref)
        issue(tile0, 0)

    rc = rcol_ref[...]
    col_k = [rc[:, 2 * TOP_K + k:2 * TOP_K + k + 1].astype(I32) for k in range(TOP_K)]
    w_k = [rc[:, 3 * TOP_K + k:3 * TOP_K + k + 1] for k in range(TOP_K)]

    def weights(c):
        jl = lax.broadcasted_iota(I32, (TM, MXU_DEPTH), 1) + c * MXU_DEPTH
        pm = jnp.zeros((TM, MXU_DEPTH), F32)
        for k in range(TOP_K):
            pm = pm + jnp.where(jl == col_k[k], w_k[k], 0.0)
        return pm.astype(BF16)

    @pl.when(i + 1 < nt)
    def _():
        issue(tile0 + i + 1, 1 - slot)

    stg = stg_ref.at[slot]
    group = pltpu.make_async_copy(eo_ref.at[pl.ds(0, WAIT_GROUP * chunk)],
                                  stg.at[pl.ds(0, WAIT_GROUP * chunk)], sem.at[slot])
    _wait_tile(group, copy(0, 0, slot), nx_ref[tile0 + i])

    pm = jnp.concatenate([weights(c) for c in range(COMB_KC_MIN)], axis=1)
    acc_ref[...] = jnp.dot(pm, _from_row_tiles(stg, 0, COMB_KC_MIN * MXU_DEPTH), preferred_element_type=F32)
    used = (rows_ref[tile0 + i] + MXU_DEPTH - 1) // MXU_DEPTH
    for c in range(COMB_KC_MIN, COMB_KC):
        @pl.when(c < used)
        def _():
            acc_ref[...] += jnp.dot(weights(c), _from_row_tiles(stg, c * MXU_DEPTH, MXU_DEPTH),
                                    preferred_element_type=F32)

    mod = mod_ref[0]
    g2 = mod[:, 5 * D_MODEL:6 * D_MODEL]
    y_ref[...] = _layer_norm(DN_ALPHA * x1_ref[...] + g2 * acc_ref[...], g_ref[...], bb_ref[...])


def _combine(seqs, tile0, n_tiles, s_te, b_te, x_te, nx_t, rows_t, x1, rcol, mod3, ln2_g, ln2_b, eo):
    def mod_map(i, *_):
        return (_tile_pos(seqs, (i + tile0) * TM)[2], 0, 0)

    return pl.pallas_call(
        functools.partial(_combine_kernel, tile0),
        grid_spec=pltpu.PrefetchScalarGridSpec(
            num_scalar_prefetch=5, grid=(n_tiles,),
            in_specs=[pl.BlockSpec((TM, D_MODEL), lambda i, *_: (i + tile0, 0)),
                      pl.BlockSpec((TM, LANES), lambda i, *_: (i + tile0, 0)),
                      pl.BlockSpec((1, 1, 6 * D_MODEL), mod_map),
                      pl.BlockSpec((1, D_MODEL), lambda i, *_: (0, 0)),
                      pl.BlockSpec((1, D_MODEL), lambda i, *_: (0, 0)),
                      pl.BlockSpec(memory_space=pl.ANY)],
            out_specs=pl.BlockSpec((TM, D_MODEL), lambda i, *_: (i, 0)),
            scratch_shapes=[pltpu.VMEM((2, COMB_ROWS // PAIR * ROW_TILES, LANES), U32),
                            pltpu.VMEM((TM, D_MODEL), F32),
                            pltpu.SemaphoreType.DMA((2,))]),
        out_shape=jax.ShapeDtypeStruct((n_tiles * TM, D_MODEL), F32),
        compiler_params=pltpu.CompilerParams(dimension_semantics=("arbitrary",),
                                             vmem_limit_bytes=VMEM_LIMIT_BYTES),
        name="combine",
    )(s_te, b_te, x_te, nx_t, rows_t, x1, rcol, mod3, ln2_g, ln2_b, eo)


def _prep_params(w_in, b_in, w_spatial, b_spatial, sgu_ln_g, sgu_ln_b, w_br_attn, w_br_sgu, w_out,
                 ln1_g, ln1_b, w_router, b_router):
    q_end, k_end, v_end = ATT_W, ATT_W + KV_W, ATT_W + 2 * KV_W

    def dup(w, lo):
        h0, h1 = w[..., lo:lo + HEAD_DIM], w[..., lo + HEAD_DIM:lo + 2 * HEAD_DIM]
        return jnp.concatenate([h0, h0, h1, h1], axis=-1)

    w_kv = jnp.concatenate([dup(w_in, q_end), dup(w_in, k_end)], axis=1).astype(BF16)
    b_kv = jnp.concatenate([dup(b_in, q_end), dup(b_in, k_end)], axis=0).reshape(1, -1)
    w_mix = w_in.astype(BF16)
    b_mix = b_in.reshape(1, -1)
    w_hi = w_router.astype(BF16)
    w_lo = (w_router - w_hi.astype(F32)).astype(BF16)
    p = dict(
        w_mix=w_mix, b_mix=b_mix,
        w_sp=w_spatial.astype(BF16), b_spt=b_spatial.T,
        sgu_g=sgu_ln_g.reshape(1, -1), sgu_b=sgu_ln_b.reshape(1, -1),
        w_ba=w_br_attn.astype(BF16), w_bs=w_br_sgu.astype(BF16), w_out=w_out.astype(BF16),
        ln1_g=ln1_g.reshape(1, -1), ln1_b=ln1_b.reshape(1, -1),
        w_r=jnp.concatenate([w_hi, w_lo], axis=1), b_r=b_router.reshape(1, -1),
    )
    return w_kv, b_kv, p


def _layer(seqs, xp, xs, c_all, w_ada, b_ada, w_in, b_in, sink, sgu_ln_g, sgu_ln_b, w_spatial, b_spatial,
           w_br_attn, w_br_sgu, w_out, ln1_g, ln1_b, w_router, b_router, w_up, b_up, w_down, b_down,
           ln2_g, ln2_b):
    T = seqs.n_tokens
    nt = T // TM
    ntp = seqs.n_prompt // TM
    c_pad = jnp.zeros((8, D_MODEL), F32).at[:c_all.shape[0]].set(c_all)
    mod3 = _ada(c_pad, w_ada, b_ada).reshape(8, 1, 6 * D_MODEL)
    cos_t, sin_t = _rope_tables(max(seqs.prompt_len, seqs.sample_len))
    w_kv, b_kv, p = _prep_params(w_in, b_in, w_spatial, b_spatial, sgu_ln_g, sgu_ln_b, w_br_attn,
                                 w_br_sgu, w_out, ln1_g, ln1_b, w_router, b_router)
    kt, v = _kv(seqs, xp, xs, mod3, cos_t, sin_t, w_kv, b_kv)
    x1, h2, logits = _mixer(seqs, sink, xp, xs, mod3, cos_t, sin_t, kt, v, p)
    rcol, rrow, cnt3 = _router(logits)

    cnt = cnt3[:, :, 0]
    count = cnt.sum(0)
    gb = EXPERT_SUBS * BM
    reg = (count + REGION_SLACK + gb - 1) // gb * gb
    pad_end = jnp.cumsum(reg)
    pad_start = pad_end - reg
    flat = lambda a: a.reshape(-1).astype(I32)
    s_te = flat(pad_start[None, :] + jnp.cumsum(cnt, axis=0) - cnt)
    d_nch = jnp.maximum((cnt + DISP_CHUNK - 1) // DISP_CHUNK, 1)
    c_nch = jnp.maximum((cnt + COMB_CHUNK - 1) // COMB_CHUNK, 1)
    db_te = flat(jnp.cumsum(cnt, axis=1) - cnt)
    cb_te = flat(COMB_CHUNK * (jnp.cumsum(c_nch, axis=1) - c_nch))
    rows_t = flat(COMB_CHUNK * c_nch.sum(1))
    n_blk = (nt * TILE_ROWS + N_EXPERTS * (REGION_SLACK + gb - 1)) // gb + 1
    tails = jnp.concatenate([pad_start + count, pad_end[:-1], jnp.array([n_blk * gb])]).astype(I32)
    blk_e = jnp.minimum(((jnp.arange(n_blk, dtype=I32) * gb)[:, None] >= pad_end[None, :]).sum(1),
                        N_EXPERTS - 1).astype(I32)
    sub_start = jnp.arange(n_blk * EXPERT_SUBS, dtype=I32) * BM
    owner = jnp.repeat(blk_e, EXPERT_SUBS)[:, None] == jnp.arange(N_EXPERTS)[None, :]
    sub_rows = (owner * (pad_start + count)[None, :]).sum(1) - sub_start
    blk_v = jnp.where(sub_start < pad_end[-1], jnp.clip(sub_rows, 0, BM), 0).astype(I32)

    xin = _dispatch(seqs, db_te, s_te, flat(d_nch - 1), flat((d_nch - 1).sum(1)), tails, h2, rrow, n_blk * gb)
    eo = _experts(blk_e, blk_v, xin, w_up, b_up, w_down, b_down, n_blk)
    comb = (s_te, cb_te, flat(c_nch - 1), flat((c_nch - 1).sum(1)), rows_t,
            x1, rcol, mod3, ln2_g.reshape(1, -1), ln2_b.reshape(1, -1), eo)
    return _combine(seqs, 0, ntp, *comb), _combine(seqs, ntp, nt - ntp, *comb)


def kernel(x_prompt, x_sample, c_prompt, c_sample, w_ada, b_ada, w_in, b_in, sink, sgu_ln_g, sgu_ln_b, w_spatial, b_spatial, w_br_attn, w_br_sgu, w_out, ln1_g, ln1_b, w_router, b_router, w_up, b_up, w_down, b_down, ln2_g, ln2_b):
    assert w_ada.shape[0] == DEPTH == 1
    bp, sp, d = x_prompt.shape
    bs, ss, _ = x_sample.shape
    seqs = _Seqs(n_prompt=bp * sp, prompt_len=sp, sample_len=ss, n_tokens=bp * sp + bs * ss)
    c_all = jnp.concatenate([c_prompt, c_sample], axis=0)
    yp, ys = _layer(seqs, x_prompt.reshape(bp * sp, d), x_sample.reshape(bs * ss, d), c_all,
                    w_ada[0], b_ada[0], w_in[0], b_in[0], sink[0], sgu_ln_g[0], sgu_ln_b[0],
                    w_spatial[0], b_spatial[0], w_br_attn[0], w_br_sgu[0], w_out[0], ln1_g[0], ln1_b[0],
                    w_router[0], b_router[0], w_up[0], b_up[0], w_down[0], b_down[0], ln2_g[0], ln2_b[0])
    return (yp.reshape(bp, sp, d), ys.reshape(bs, ss, d))
```

```python
import functools
import math
from typing import NamedTuple

import jax
import jax.numpy as jnp
from jax import lax
from jax.experimental import pallas as pl
from jax.experimental.pallas import tpu as pltpu

F32 = jnp.float32
BF16 = jnp.bfloat16
I32 = jnp.int32

D_MODEL = 1024
N_HEADS = 8
N_KV_HEADS = 2
HEAD_DIM = 64
ATT_W = N_HEADS * HEAD_DIM
KV_W = N_KV_HEADS * HEAD_DIM
BLOCK = 128
ROPE_THETA = 500000.0
ROT_DIM = HEAD_DIM // 4
ROT_HALF = ROT_DIM // 2
SGU_W = D_MODEL // 2
SGU_GROUPS = 4
N_EXPERTS = 32
TOP_K = 4
D_FF = D_MODEL
SWIGLU_LIMIT = 7.0
SWIGLU_ALPHA = 1.702
LN_EPS = 1e-5
DEPTH = 1
DN_ALPHA = (2 * DEPTH) ** 0.25

LANES = 128
MXU_DEPTH = 256
ROW_TILES = D_MODEL // LANES
VMEM_LIMIT_BYTES = 56 * 1024 * 1024
U32 = jnp.uint32
PAIR = 2

TM = 256
MT = 512
TK = 1024
BM = 512
EXPERT_SUBS = 2
ROUTER_TOKENS = 2048
DISP_CHUNK = 48
COMB_CHUNK = 48
REGION_SLACK = DISP_CHUNK
TILE_ROWS = TM * TOP_K + N_EXPERTS
DISP_ROWS = TILE_ROWS + DISP_CHUNK
DISP_BLOCK = DISP_ROWS // 3
assert DISP_BLOCK * 3 == DISP_ROWS and DISP_BLOCK % 16 == 0
COMB_ROWS = -(-(TILE_ROWS + N_EXPERTS * COMB_CHUNK) // MXU_DEPTH) * MXU_DEPTH
NEG_INF = float("-inf")


class _Seqs(NamedTuple):
    n_prompt: int
    prompt_len: int
    sample_len: int
    n_tokens: int


def _tile_pos(seqs, t0):
    is_s = t0 >= seqs.n_prompt
    seq_len = jnp.where(is_s, seqs.sample_len, seqs.prompt_len)
    off = jnp.where(is_s, t0 - seqs.n_prompt, t0)
    pos0 = off % seq_len
    row = jnp.where(is_s, seqs.n_prompt // seqs.prompt_len + off // seq_len, off // seq_len)
    return seq_len, pos0, row


def _ada_kernel(c_ref, w_ref, b_ref, o_ref):
    c = c_ref[...]
    a = c * jax.nn.sigmoid(c)
    o_ref[...] = jnp.dot(a, w_ref[...], preferred_element_type=F32,
                         precision=lax.Precision.HIGHEST) + b_ref[...]


def _ada(c_pad, w_ada, b_ada):
    n = w_ada.shape[1]
    bn = n // 4
    return pl.pallas_call(
        _ada_kernel,
        grid=(n // bn,),
        in_specs=[pl.BlockSpec((8, D_MODEL), lambda j: (0, 0)),
                  pl.BlockSpec((D_MODEL, bn), lambda j: (0, j)),
                  pl.BlockSpec((1, bn), lambda j: (0, j))],
        out_specs=pl.BlockSpec((8, bn), lambda j: (0, j)),
        out_shape=jax.ShapeDtypeStruct((8, n), F32),
        compiler_params=pltpu.CompilerParams(vmem_limit_bytes=VMEM_LIMIT_BYTES),
        name="ada",
    )(c_pad, w_ada, b_ada.reshape(1, n))


def _rope_tables(length):
    inv = ROPE_THETA ** (-jnp.arange(ROT_HALF, dtype=F32) * 2.0 / ROT_DIM)
    ang = jnp.arange(length, dtype=F32)[:, None] * inv[None, :]
    lane = jnp.arange(LANES) % HEAD_DIM
    spread = ((lane[None, :] % ROT_HALF == jnp.arange(ROT_HALF)[:, None]) & (lane[None, :] < ROT_DIM)).astype(F32)
    sign = jnp.where(lane < ROT_HALF, -1.0, 1.0)
    exact = dict(precision=lax.Precision.HIGHEST, preferred_element_type=F32)
    cos = jnp.dot(jnp.cos(ang), spread, **exact) + (lane >= ROT_DIM).astype(F32)[None, :]
    sin = jnp.dot(jnp.sin(ang), spread * sign[None, :], **exact)
    return cos, sin


def _rope(x, cos, sin):
    n = x.shape[1]
    reps = n // LANES
    c = jnp.concatenate([cos] * reps, axis=1)
    s = jnp.concatenate([sin] * reps, axis=1)
    lane = lax.broadcasted_iota(I32, x.shape, 1)
    first = (lane & (HEAD_DIM - 1)) < ROT_HALF
    partner = jnp.where(first, pltpu.roll(x, n - ROT_HALF, 1), pltpu.roll(x, ROT_HALF, 1))
    return x * c + partner * s


def _kv_kernel(seqs, xp_ref, xs_ref, mod_ref, cos_ref, sin_ref, w_ref, b_ref, kt_ref, v_ref):
    mod = mod_ref[0]
    sh1 = mod[:, 0:D_MODEL]
    sc1 = mod[:, D_MODEL:2 * D_MODEL]
    x = jnp.where(pl.program_id(0) * TK >= seqs.n_prompt, xs_ref[...], xp_ref[...])
    h = (x * (1.0 + sc1) + sh1).astype(BF16)
    kv = jnp.dot(h, w_ref[...], preferred_element_type=F32) + b_ref[...]
    k = _rope(kv[:, 0:2 * LANES], cos_ref[...], sin_ref[...])
    kt_ref[...] = k.T.astype(BF16)
    v_ref[...] = kv[:, 2 * LANES:4 * LANES].astype(BF16)


def _group_x_specs(seqs, tile):
    ntp = seqs.n_prompt // tile
    return [pl.BlockSpec((tile, D_MODEL), lambda i, *_: (jnp.minimum(i, ntp - 1), 0)),
            pl.BlockSpec((tile, D_MODEL), lambda i, *_: (jnp.maximum(i - ntp, 0), 0))]


def _kv(seqs, xp, xs, mod3, cos_t, sin_t, w_kv, b_kv):
    T = seqs.n_tokens

    def mod_map(i):
        return (_tile_pos(seqs, i * TK)[2], 0, 0)

    def rope_map(i):
        return (_tile_pos(seqs, i * TK)[1] // TK, 0)

    return pl.pallas_call(
        functools.partial(_kv_kernel, seqs),
        grid=(T // TK,),
        in_specs=_group_x_specs(seqs, TK) + [
                  pl.BlockSpec((1, 1, 6 * D_MODEL), mod_map),
                  pl.BlockSpec((TK, LANES), rope_map),
                  pl.BlockSpec((TK, LANES), rope_map),
                  pl.BlockSpec((D_MODEL, 4 * LANES), lambda i: (0, 0)),
                  pl.BlockSpec((1, 4 * LANES), lambda i: (0, 0))],
        out_specs=[pl.BlockSpec((2 * LANES, TK), lambda i: (0, i)),
                   pl.BlockSpec((TK, 2 * LANES), lambda i: (i, 0))],
        out_shape=[jax.ShapeDtypeStruct((2 * LANES, T), BF16),
                   jax.ShapeDtypeStruct((T, 2 * LANES), BF16)],
        compiler_params=pltpu.CompilerParams(dimension_semantics=("arbitrary",),
                                             vmem_limit_bytes=VMEM_LIMIT_BYTES),
        name="kv",
    )(xp, xs, mod3, cos_t, sin_t, w_kv, b_kv)


def _layer_norm(x, g, b):
    mu = jnp.mean(x, axis=-1, keepdims=True)
    xc = x - mu
    var = jnp.mean(xc * xc, axis=-1, keepdims=True)
    return xc * lax.rsqrt(var + LN_EPS) * g + b


def _attention(q, kfull, vfull, valids, sink_ref):
    lane = lax.broadcasted_iota(I32, (BLOCK, LANES), 1)
    lo = lane < HEAD_DIM
    ones = jnp.ones((3 * BLOCK, LANES), BF16)
    units = [(jb, hk) for jb in range(MT // BLOCK) for hk in range(N_KV_HEADS)]
    scores, sinks = [], []
    for jb, hk in units:
        parts = []
        for p in range(2):
            qp = q[jb * BLOCK:(jb + 1) * BLOCK, (2 * hk + p) * LANES:(2 * hk + p + 1) * LANES]
            parts.append(jnp.where(lo, qp, 0.0).astype(BF16))
            parts.append(jnp.where(lo, 0.0, qp).astype(BF16))
        kwin = kfull[hk * LANES:(hk + 1) * LANES, jb * BLOCK:(jb + 3) * BLOCK]
        s = jnp.dot(jnp.concatenate(parts, axis=0), kwin, preferred_element_type=F32)
        scores.append(jnp.where(jnp.concatenate([valids[jb]] * 4, axis=0), s, NEG_INF))
        sinks.extend(jnp.full((BLOCK, 1), sink_ref[hk * 4 + g], F32) for g in range(4))
    s = jnp.concatenate(scores, axis=0)
    sk = jnp.concatenate(sinks, axis=0)
    m = jnp.maximum(jnp.max(s, axis=-1, keepdims=True), sk)
    p = jnp.exp(s - m).astype(BF16)
    sink_term = jnp.exp(sk - m)
    rows = []
    for ui, (jb, hk) in enumerate(units):
        vwin = jnp.concatenate([vfull[jb * BLOCK:(jb + 3) * BLOCK, hk * LANES:(hk + 1) * LANES], ones], axis=1)
        r0 = ui * 4 * BLOCK
        ov = jnp.dot(p[r0:r0 + 4 * BLOCK], vwin, preferred_element_type=F32)
        o = ov[:, 0:LANES] / (ov[:, LANES:2 * LANES] + sink_term[r0:r0 + 4 * BLOCK])
        pair = [jnp.where(lo, o[(2 * p2) * BLOCK:(2 * p2 + 1) * BLOCK],
                          o[(2 * p2 + 1) * BLOCK:(2 * p2 + 2) * BLOCK]) for p2 in range(2)]
        rows.append(jnp.concatenate(pair, axis=1))
    n_h = N_KV_HEADS
    return jnp.concatenate(
        [jnp.concatenate(rows[jb * n_h:(jb + 1) * n_h], axis=1) for jb in range(MT // BLOCK)], axis=0)


def _mixer_kernel(seqs, sink_ref, xp_ref, xs_ref, mod_ref, cos_ref, sin_ref,
                  ktp_ref, ktc_ref, ktn_ref, vp_ref, vc_ref, vn_ref,
                  wmix_ref, bmix_ref, wsp_ref, bspt_ref, sg_ref, sb_ref,
                  wba_ref, wbs_ref, wout_ref, l1g_ref, l1b_ref, wr_ref, br_ref,
                  x1_ref, h2_ref, logit_ref):
    i = pl.program_id(0)
    seq_len, pos0, _ = _tile_pos(seqs, i * MT)
    mod = mod_ref[0]
    sh1, sc1, g1 = (mod[:, j * D_MODEL:(j + 1) * D_MODEL] for j in range(3))
    sh2, sc2 = (mod[:, j * D_MODEL:(j + 1) * D_MODEL] for j in range(3, 5))
    x = jnp.where(i * MT >= seqs.n_prompt, xs_ref[...], xp_ref[...])
    h = (x * (1.0 + sc1) + sh1).astype(BF16)
    kv_end = ATT_W + 2 * KV_W
    zq = jnp.dot(h, wmix_ref[:, 0:ATT_W], preferred_element_type=F32) + bmix_ref[:, 0:ATT_W]
    z = jnp.dot(h, wmix_ref[:, kv_end:], preferred_element_type=F32) + bmix_ref[:, kv_end:]
    q = _rope(zq, cos_ref[...], sin_ref[...]) * (HEAD_DIM ** -0.5)
    u = jax.nn.gelu(z[:, 0:SGU_W])
    vs = _layer_norm(jax.nn.gelu(z[:, SGU_W:2 * SGU_W]), sg_ref[...], sb_ref[...])
    ga = z[:, 2 * SGU_W:2 * SGU_W + D_MODEL]
    gs = z[:, 2 * SGU_W + D_MODEL:]

    kfull = jnp.concatenate([ktp_ref[...], ktc_ref[...], ktn_ref[...]], axis=1)
    vfull = jnp.concatenate([vp_ref[...], vc_ref[...], vn_ref[...]], axis=0)
    qi = lax.broadcasted_iota(I32, (BLOCK, 3 * BLOCK), 0)
    ki = lax.broadcasted_iota(I32, (BLOCK, 3 * BLOCK), 1)
    band = (ki >= qi) & (ki <= qi + 2 * BLOCK)
    vs_b = vs.astype(BF16)
    valids, sgu_rows = [], []
    for jb in range(MT // BLOCK):
        posb = pos0 + jb * BLOCK
        valids.append(band & (ki >= jnp.where(posb == 0, BLOCK, 0))
                      & (ki < jnp.where(posb + BLOCK == seq_len, 2 * BLOCK, 3 * BLOCK)))
        groups = []
        for g in range(SGU_GROUPS):
            vg = vs_b[jb * BLOCK:(jb + 1) * BLOCK, g * LANES:(g + 1) * LANES]
            sv = jnp.dot(wsp_ref[g], vg, preferred_element_type=F32) + bspt_ref[:, g:g + 1]
            groups.append(sv)
        sgu_rows.append(jnp.concatenate(groups, axis=1))
    attn = _attention(q, kfull, vfull, valids, sink_ref)
    sgu = u * jnp.concatenate(sgu_rows, axis=0)

    a1 = jnp.dot(attn.astype(BF16), wba_ref[...], preferred_element_type=F32)
    a2 = jnp.dot(sgu.astype(BF16), wbs_ref[...], preferred_element_type=F32)
    merged = jax.nn.sigmoid(ga) * a1 + jax.nn.sigmoid(gs) * a2
    mix = jnp.dot(merged.astype(BF16), wout_ref[...], preferred_element_type=F32)
    x1 = _layer_norm(DN_ALPHA * x + g1 * mix, l1g_ref[...], l1b_ref[...])
    x1_ref[...] = x1
    h2 = x1 * (1.0 + sc2) + sh2
    hi = h2.astype(BF16)
    h2_ref[...] = hi

    lo_part = (h2 - hi.astype(F32)).astype(BF16)
    l1 = jnp.dot(hi, wr_ref[...], preferred_element_type=F32)
    l2 = jnp.dot(lo_part, wr_ref[:, 0:N_EXPERTS], preferred_element_type=F32)
    logit_ref[...] = l1[:, 0:N_EXPERTS] + l1[:, N_EXPERTS:2 * N_EXPERTS] + l2 + br_ref[...]


PACK = 4096.0
assert DISP_ROWS <= PACK and COMB_ROWS <= PACK and DISP_ROWS * PACK < 2 ** 24


def _router_kernel(l_ref, rcol_ref, rrow_ref, cnt_ref):
    rt = l_ref.shape[0]
    ns = rt // TM
    padded = jnp.concatenate([l_ref[...], jnp.zeros((rt, LANES - N_EXPERTS), F32)], axis=1)
    work = padded.T[0:N_EXPERTS]
    eidx = lax.broadcasted_iota(I32, (N_EXPERTS, rt), 0)
    idxs, vals = [], []
    for _ in range(TOP_K):
        m = jnp.max(work, axis=0, keepdims=True)
        ix = jnp.min(jnp.where(work == m, eidx, N_EXPERTS), axis=0, keepdims=True)
        idxs.append(ix)
        vals.append(m)
        work = jnp.where(eidx == ix, NEG_INF, work)
    exps = [jnp.exp(v - vals[0]) for v in vals]
    esum = exps[0] + exps[1] + exps[2] + exps[3]
    wts = [e / esum for e in exps]

    sel = jnp.zeros((N_EXPERTS, rt), F32)
    for ix in idxs:
        sel = sel + jnp.where(eidx == ix, 1.0, 0.0)
    sel_b = sel.astype(BF16)
    ti = lax.broadcasted_iota(I32, (TM, TM), 0)
    tj = lax.broadcasted_iota(I32, (TM, TM), 1)
    earlier = jnp.where(ti < tj, 1.0, 0.0).astype(BF16)
    tiles = [slice(s * TM, (s + 1) * TM) for s in range(ns)]
    ranks = [jnp.dot(sel_b[:, t], earlier, preferred_element_type=F32) for t in tiles]
    cnts, nchs = [], []
    for t in tiles:
        cnt = jnp.sum(sel[:, t], axis=1, keepdims=True)
        cnt = cnt + (cnt - 2.0 * jnp.floor(cnt * 0.5))
        cnts.append(cnt)
        nchs.append(jnp.maximum(jnp.floor((cnt + (COMB_CHUNK - 0.5)) * (1.0 / COMB_CHUNK)), 1.0))
    lane = lax.broadcasted_iota(I32, (N_EXPERTS, LANES), 1)
    pre = jnp.zeros((N_EXPERTS, LANES), F32)
    for j, col in enumerate(cnts + nchs):
        pre = jnp.where(lane == j, col, pre)
    ei = lax.broadcasted_iota(I32, (N_EXPERTS, N_EXPERTS), 0)
    ej = lax.broadcasted_iota(I32, (N_EXPERTS, N_EXPERTS), 1)
    before = jnp.where(ej < ei, 1.0, 0.0).astype(BF16)
    base = jnp.dot(before, pre.astype(BF16), preferred_element_type=F32)
    both = jnp.concatenate(
        [(ranks[s] + base[:, s:s + 1]) * PACK + (ranks[s] + base[:, ns + s:ns + s + 1] * COMB_CHUNK)
         for s in range(ns)], axis=1)

    packed = [jnp.sum(jnp.where(eidx == ix, both, 0.0), axis=0, keepdims=True) for ix in idxs]
    drow = [jnp.floor(v * (1.0 / PACK)) for v in packed]
    table = [ix.astype(F32) for ix in idxs] + drow + [v - d * PACK for v, d in zip(packed, drow)] + wts
    sub = lax.broadcasted_iota(I32, (LANES, rt), 0)
    rr = jnp.zeros((LANES, rt), F32)
    for j, row in enumerate(table):
        rr = jnp.where(sub == j, row, rr)
    rcol_ref[...] = rr.T
    for s in range(ns):
        rrow_ref[s] = rr[0:16, tiles[s]]
        cnt_ref[s] = jnp.broadcast_to(cnts[s], (N_EXPERTS, LANES)).astype(I32)


def _router(logits):
    T = logits.shape[0]
    rt = math.gcd(T, ROUTER_TOKENS)
    nt = T // TM
    assert 2 * (rt // TM) <= LANES
    return pl.pallas_call(
        _router_kernel,
        grid=(T // rt,),
        in_specs=[pl.BlockSpec((rt, N_EXPERTS), lambda i: (i, 0))],
        out_specs=[pl.BlockSpec((rt, LANES), lambda i: (i, 0)),
                   pl.BlockSpec((rt // TM, 16, TM), lambda i: (i, 0, 0)),
                   pl.BlockSpec((rt // TM, N_EXPERTS, LANES), lambda i: (i, 0, 0))],
        out_shape=[jax.ShapeDtypeStruct((T, LANES), F32),
                   jax.ShapeDtypeStruct((nt, 16, TM), F32),
                   jax.ShapeDtypeStruct((nt, N_EXPERTS, LANES), I32)],
        compiler_params=pltpu.CompilerParams(dimension_semantics=("arbitrary",),
                                             vmem_limit_bytes=VMEM_LIMIT_BYTES),
        name="router",
    )(logits)


def _mixer(seqs, sink, xp, xs, mod3, cos_t, sin_t, kt, v, p):
    T = seqs.n_tokens
    nt = T // MT
    nb = T // BLOCK
    r = MT // BLOCK

    def mod_map(i, s):
        return (_tile_pos(seqs, i * MT)[2], 0, 0)

    def rope_map(i, s):
        return (_tile_pos(seqs, i * MT)[1] // MT, 0)

    const2 = lambda i, s: (0, 0)
    once = dict(pipeline_mode=pl.Buffered(1))
    in_specs = _group_x_specs(seqs, MT) + [
        pl.BlockSpec((1, 1, 6 * D_MODEL), mod_map),
        pl.BlockSpec((MT, LANES), rope_map),
        pl.BlockSpec((MT, LANES), rope_map),
        pl.BlockSpec((2 * LANES, BLOCK), lambda i, s: (0, jnp.maximum(i * r - 1, 0))),
        pl.BlockSpec((2 * LANES, MT), lambda i, s: (0, i)),
        pl.BlockSpec((2 * LANES, BLOCK), lambda i, s: (0, jnp.minimum(i * r + r, nb - 1))),
        pl.BlockSpec((BLOCK, 2 * LANES), lambda i, s: (jnp.maximum(i * r - 1, 0), 0)),
        pl.BlockSpec((MT, 2 * LANES), lambda i, s: (i, 0)),
        pl.BlockSpec((BLOCK, 2 * LANES), lambda i, s: (jnp.minimum(i * r + r, nb - 1), 0)),
        pl.BlockSpec(p["w_mix"].shape, const2, **once),
        pl.BlockSpec(p["b_mix"].shape, const2, **once),
        pl.BlockSpec(p["w_sp"].shape, lambda i, s: (0, 0, 0), **once),
        pl.BlockSpec(p["b_spt"].shape, const2, **once),
        pl.BlockSpec(p["sgu_g"].shape, const2, **once),
        pl.BlockSpec(p["sgu_b"].shape, const2, **once),
        pl.BlockSpec(p["w_ba"].shape, const2, **once),
        pl.BlockSpec(p["w_bs"].shape, const2, **once),
        pl.BlockSpec(p["w_out"].shape, const2, **once),
        pl.BlockSpec(p["ln1_g"].shape, const2, **once),
        pl.BlockSpec(p["ln1_b"].shape, const2, **once),
        pl.BlockSpec(p["w_r"].shape, const2, **once),
        pl.BlockSpec(p["b_r"].shape, const2, **once),
    ]
    out_specs = [
        pl.BlockSpec((MT, D_MODEL), lambda i, s: (i, 0)),
        pl.BlockSpec((MT, D_MODEL), lambda i, s: (i, 0)),
        pl.BlockSpec((MT, N_EXPERTS), lambda i, s: (i, 0)),
    ]
    out_shape = [
        jax.ShapeDtypeStruct((T, D_MODEL), F32),
        jax.ShapeDtypeStruct((T, D_MODEL), BF16),
        jax.ShapeDtypeStruct((T, N_EXPERTS), F32),
    ]
    return pl.pallas_call(
        functools.partial(_mixer_kernel, seqs),
        grid_spec=pltpu.PrefetchScalarGridSpec(
            num_scalar_prefetch=1, grid=(nt,), in_specs=in_specs, out_specs=out_specs),
        out_shape=out_shape,
        compiler_params=pltpu.CompilerParams(dimension_semantics=("arbitrary",),
                                             vmem_limit_bytes=VMEM_LIMIT_BYTES),
        name="mixer",
    )(sink, xp, xs, mod3, cos_t, sin_t, kt, kt, kt, v, v, v,
      p["w_mix"], p["b_mix"], p["w_sp"], p["b_spt"], p["sgu_g"], p["sgu_b"],
      p["w_ba"], p["w_bs"], p["w_out"], p["ln1_g"], p["ln1_b"], p["w_r"], p["b_r"])


def _pair_rows(row):
    return pl.multiple_of((row // PAIR) * ROW_TILES, ROW_TILES)


def _to_row_tiles(dst_ref, rows, n, start=0):
    words = pltpu.bitcast(rows, U32)
    base = start // PAIR * ROW_TILES
    for c in range(ROW_TILES):
        dst_ref[pl.ds(base + c, n // PAIR, stride=ROW_TILES), :] = words[:, c * LANES:(c + 1) * LANES]


def _from_row_tiles(src_ref, start, n):
    base = start // PAIR * ROW_TILES
    words = jnp.concatenate(
        [src_ref[pl.ds(base + c, n // PAIR, stride=ROW_TILES), :] for c in range(ROW_TILES)], axis=1)
    return pltpu.bitcast(words, BF16)


WAIT_GROUP = 16


def _issue_tile(copy, k0, src_ref, dst_ref, extra_ref, n_extra, step, lo=0, hi=N_EXPERTS):
    for e in range(lo, hi):
        copy(src_ref[k0 + e], dst_ref[k0 + e]).start()
    if hi < N_EXPERTS:
        return

    @pl.when(n_extra > 0)
    def _():
        def per_expert(e, c):
            def per_chunk(j, c2):
                copy(src_ref[k0 + e] + j * step, dst_ref[k0 + e] + j * step).start()
                return c2
            lax.fori_loop(1, extra_ref[k0 + e] + 1, per_chunk, 0)
            return c
        lax.fori_loop(0, N_EXPERTS, per_expert, 0)


def _wait_tile(group_copy, chunk_copy, n_extra):
    for _ in range(N_EXPERTS // WAIT_GROUP):
        group_copy.wait()

    def body(_, c):
        chunk_copy.wait()
        return c
    lax.fori_loop(0, n_extra, body, 0)


def _dispatch_kernel(b_ref, s_ref, x_ref, nx_ref, tail_ref, h2_ref, rrow_ref, xin_ref, stg_ref, zero_ref, sem):
    i = pl.program_id(0)
    nt = pl.num_programs(0)
    slot = i % 2
    chunk = DISP_CHUNK // PAIR * ROW_TILES

    def copy(src_row, dst_row, sl):
        return pltpu.make_async_copy(stg_ref.at[sl, pl.ds(_pair_rows(src_row), chunk)],
                                     xin_ref.at[pl.ds(_pair_rows(dst_row), chunk)], sem)

    def wait_tile(tile):
        group = pltpu.make_async_copy(stg_ref.at[0, pl.ds(0, WAIT_GROUP * chunk)],
                                      xin_ref.at[pl.ds(0, WAIT_GROUP * chunk)], sem)
        _wait_tile(group, copy(0, 0, 0), nx_ref[tile])

    def issue(tile, sl):
        _issue_tile(lambda src, dst: copy(src, dst, sl), tile * N_EXPERTS, b_ref, s_ref, x_ref, nx_ref[tile],
                    DISP_CHUNK)

    @pl.when(i > 1)
    def _():
        wait_tile(i - 2)

    @pl.when(i > 0)
    def _():
        issue(i - 1, 1 - slot)

    rr = rrow_ref[0]
    pos = [rr[TOP_K + k:TOP_K + k + 1].astype(I32) for k in range(TOP_K)]
    for a in range(DISP_ROWS // DISP_BLOCK):
        rho = lax.broadcasted_iota(I32, (DISP_BLOCK, TM), 0) + a * DISP_BLOCK
        pt = jnp.zeros((DISP_BLOCK, TM), F32)
        for k in range(TOP_K):
            pt = pt + jnp.where(rho == pos[k], 1.0, 0.0)
        rows = jnp.dot(pt.astype(BF16), h2_ref[...], preferred_element_type=F32).astype(BF16)
        _to_row_tiles(stg_ref.at[slot], rows, DISP_BLOCK, a * DISP_BLOCK)

    @pl.when(i == nt - 1)
    def _():
        @pl.when(i > 0)
        def _():
            wait_tile(i - 1)
        issue(i, slot)
        wait_tile(i)
        zero_ref[...] = jnp.zeros_like(zero_ref)

        def zcopy(dst_row):
            return pltpu.make_async_copy(zero_ref, xin_ref.at[pl.ds(_pair_rows(dst_row), chunk)], sem)

        def zwait(count):
            def body(_, c):
                zcopy(0).wait()
                return c
            lax.fori_loop(0, count, body, 0)

        def per_expert(e, total):
            lo = tail_ref[e]
            nz = (tail_ref[N_EXPERTS + e] - lo) // DISP_CHUNK

            def per_chunk(j, c):
                zcopy(lo + j * DISP_CHUNK).start()
                return c
            lax.fori_loop(0, nz, per_chunk, 0)
            return total + nz
        zwait(lax.fori_loop(0, N_EXPERTS, per_expert, 0))

        def last_chunk(e, c):
            zcopy(tail_ref[N_EXPERTS + e] - DISP_CHUNK).start()
            return c
        lax.fori_loop(0, N_EXPERTS, last_chunk, 0)
        zwait(N_EXPERTS)


def _dispatch(seqs, b_te, s_te, x_te, nx_t, tails, h2, rrow, n_rows):
    nt = seqs.n_tokens // TM
    return pl.pallas_call(
        _dispatch_kernel,
        grid_spec=pltpu.PrefetchScalarGridSpec(
            num_scalar_prefetch=5, grid=(nt,),
            in_specs=[pl.BlockSpec((TM, D_MODEL), lambda i, *_: (i, 0)),
                      pl.BlockSpec((1, 16, TM), lambda i, *_: (i, 0, 0))],
            out_specs=pl.BlockSpec(memory_space=pl.ANY),
            scratch_shapes=[pltpu.VMEM((2, DISP_ROWS // PAIR * ROW_TILES, LANES), U32),
                            pltpu.VMEM((DISP_CHUNK // PAIR * ROW_TILES, LANES), U32),
                            pltpu.SemaphoreType.DMA]),
        out_shape=jax.ShapeDtypeStruct((n_rows // PAIR * ROW_TILES, LANES), U32),
        compiler_params=pltpu.CompilerParams(dimension_semantics=("arbitrary",),
                                             vmem_limit_bytes=VMEM_LIMIT_BYTES),
        name="dispatch",
    )(b_te, s_te, x_te, nx_t, tails, h2, rrow)


def _expert_kernel(be_ref, bv_ref, x_ref, wup_hbm, bup_ref, wdn_hbm, bdn_ref, o_ref,
                   wup_f, wdn_f, wup_b, wdn_b, sem):
    i = pl.program_id(0)
    e = be_ref[i]
    first = jnp.logical_or(i == 0, e != be_ref[jnp.maximum(i - 1, 0)])

    def fetch(ex):
        slot = ex % 2
        return (pltpu.make_async_copy(wup_hbm.at[ex], wup_f.at[slot], sem.at[0, slot]),
                pltpu.make_async_copy(wdn_hbm.at[ex], wdn_f.at[slot], sem.at[1, slot]))

    @pl.when(i == 0)
    def _():
        for d in fetch(e):
            d.start()

    @pl.when(first)
    def _():
        for d in fetch(e):
            d.wait()

        @pl.when(e + 1 < N_EXPERTS)
        def _():
            for d in fetch(e + 1):
                d.start()

        @pl.when(bv_ref[i * EXPERT_SUBS] > 0)
        def _():
            wup_b[...] = wup_f[e % 2].astype(BF16)
            wdn_b[...] = wdn_f[e % 2].astype(BF16)

    def ffn(start, n, valid):
        x = _from_row_tiles(x_ref, start, n)
        row = lax.broadcasted_iota(I32, (n, 1), 0)
        xb = jnp.where(row < valid, x, jnp.zeros_like(x))
        hu = jnp.dot(xb, wup_b[...], preferred_element_type=F32) + bup_ref[0]
        gate = jnp.minimum(hu[:, 0:D_FF], SWIGLU_LIMIT)
        lin = jnp.clip(hu[:, D_FF:], -SWIGLU_LIMIT, SWIGLU_LIMIT)
        act = gate * jax.nn.sigmoid(SWIGLU_ALPHA * gate) * (lin + 1.0)
        y = jnp.dot(act.astype(BF16), wdn_b[...], preferred_element_type=F32) + bdn_ref[0]
        _to_row_tiles(o_ref, y.astype(BF16), n, start)

    def zero(start, n):
        o_ref[start // PAIR * ROW_TILES:(start + n) // PAIR * ROW_TILES, :] = jnp.zeros(
            (n // PAIR * ROW_TILES, LANES), U32)

    half = BM // 2
    for sub in range(EXPERT_SUBS):
        valid = bv_ref[i * EXPERT_SUBS + sub]
        start = sub * BM

        @pl.when(valid > half)
        def _():
            ffn(start, BM, valid)

        @pl.when(jnp.logical_and(valid > 0, valid <= half))
        def _():
            ffn(start, half, valid)
            zero(start + half, half)

        @pl.when(valid == 0)
        def _():
            zero(start, BM)


def _experts(blk_e, blk_v, xin, w_up, b_up, w_down, b_down, n_blk):
    gb = EXPERT_SUBS * BM
    return pl.pallas_call(
        _expert_kernel,
        grid_spec=pltpu.PrefetchScalarGridSpec(
            num_scalar_prefetch=2, grid=(n_blk,),
            in_specs=[pl.BlockSpec((gb // PAIR * ROW_TILES, LANES), lambda i, be, bv: (i, 0)),
                      pl.BlockSpec(memory_space=pl.ANY),
                      pl.BlockSpec((1, 1, 2 * D_FF), lambda i, be, bv: (be[i], 0, 0)),
                      pl.BlockSpec(memory_space=pl.ANY),
                      pl.BlockSpec((1, 1, D_MODEL), lambda i, be, bv: (be[i], 0, 0))],
            out_specs=pl.BlockSpec((gb // PAIR * ROW_TILES, LANES), lambda i, be, bv: (i, 0)),
            scratch_shapes=[pltpu.VMEM((2, D_MODEL, 2 * D_FF), F32),
                            pltpu.VMEM((2, D_FF, D_MODEL), F32),
                            pltpu.VMEM((D_MODEL, 2 * D_FF), BF16),
                            pltpu.VMEM((D_FF, D_MODEL), BF16),
                            pltpu.SemaphoreType.DMA((2, 2))]),
        out_shape=jax.ShapeDtypeStruct((n_blk * gb // PAIR * ROW_TILES, LANES), U32),
        compiler_params=pltpu.CompilerParams(dimension_semantics=("arbitrary",),
                                             vmem_limit_bytes=VMEM_LIMIT_BYTES),
        name="experts",
    )(blk_e, blk_v, xin, w_up, b_up.reshape(N_EXPERTS, 1, 2 * D_FF), w_down,
      b_down.reshape(N_EXPERTS, 1, D_MODEL))


COMB_KC = COMB_ROWS // MXU_DEPTH
COMB_KC_MIN = N_EXPERTS * COMB_CHUNK // MXU_DEPTH
assert COMB_KC_MIN * MXU_DEPTH == N_EXPERTS * COMB_CHUNK
ISSUE_SPLIT = (10, 24)


def _combine_kernel(tile0, s_ref, b_ref, x_ref, nx_ref, rows_ref, x1_ref, rcol_ref, mod_ref, g_ref, bb_ref,
                    eo_ref, y_ref, stg_ref, acc_ref, sem):
    i = pl.program_id(0)
    nt = pl.num_programs(0)
    slot = i % 2
    chunk = COMB_CHUNK // PAIR * ROW_TILES

    def copy(src_row, dst_row, sl):
        return pltpu.make_async_copy(eo_ref.at[pl.ds(_pair_rows(src_row), chunk)],
                                     stg_ref.at[sl, pl.ds(_pair_rows(dst_row), chunk)], sem.at[sl])

    def issue(tile, sl, lo=0, hi=N_EXPERTS):
        _issue_tile(lambda src, dst: copy(src, dst, sl), tile * N_EXPERTS, s_ref, b_ref, x_ref, nx_ref[tile],
                    COMB_CHUNK, lo, hi)

    def issue_next(lo, hi):
        @pl.when(i + 1 < nt)
        def _():
            issue(tile0 + i + 1, 1 - slot, lo, hi)

    @pl.when(i == 0)
    def _():
        stg_ref[...] = jnp.zeros_like(stg_ref)
        issue(tile0, 0)

    rc = rcol_ref[...]
    col_k = [rc[:, 2 * TOP_K + k:2 * TOP_K + k + 1].astype(I32) for k in range(TOP_K)]
    w_k = [rc[:, 3 * TOP_K + k:3 * TOP_K + k + 1] for k in range(TOP_K)]

    def weights(c):
        jl = lax.broadcasted_iota(I32, (TM, MXU_DEPTH), 1) + c * MXU_DEPTH
        pm = jnp.zeros((TM, MXU_DEPTH), F32)
        for k in range(TOP_K):
            pm = pm + jnp.where(jl == col_k[k], w_k[k], 0.0)
        return pm.astype(BF16)

    issue_next(0, ISSUE_SPLIT[0])

    stg = stg_ref.at[slot]
    group = pltpu.make_async_copy(eo_ref.at[pl.ds(0, WAIT_GROUP * chunk)],
                                  stg.at[pl.ds(0, WAIT_GROUP * chunk)], sem.at[slot])
    _wait_tile(group, copy(0, 0, slot), nx_ref[tile0 + i])
    issue_next(ISSUE_SPLIT[0], ISSUE_SPLIT[1])

    pm = jnp.concatenate([weights(c) for c in range(COMB_KC_MIN)], axis=1)
    acc_ref[...] = jnp.dot(pm, _from_row_tiles(stg, 0, COMB_KC_MIN * MXU_DEPTH), preferred_element_type=F32)
    issue_next(ISSUE_SPLIT[1], N_EXPERTS)
    used = (rows_ref[tile0 + i] + MXU_DEPTH - 1) // MXU_DEPTH
    for c in range(COMB_KC_MIN, COMB_KC):
        @pl.when(c < used)
        def _():
            acc_ref[...] += jnp.dot(weights(c), _from_row_tiles(stg, c * MXU_DEPTH, MXU_DEPTH),
                                    preferred_element_type=F32)

    mod = mod_ref[0]
    g2 = mod[:, 5 * D_MODEL:6 * D_MODEL]
    y_ref[...] = _layer_norm(DN_ALPHA * x1_ref[...] + g2 * acc_ref[...], g_ref[...], bb_ref[...])


def _combine(seqs, tile0, n_tiles, s_te, b_te, x_te, nx_t, rows_t, x1, rcol, mod3, ln2_g, ln2_b, eo):
    def mod_map(i, *_):
        return (_tile_pos(seqs, (i + tile0) * TM)[2], 0, 0)

    return pl.pallas_call(
        functools.partial(_combine_kernel, tile0),
        grid_spec=pltpu.PrefetchScalarGridSpec(
            num_scalar_prefetch=5, grid=(n_tiles,),
            in_specs=[pl.BlockSpec((TM, D_MODEL), lambda i, *_: (i + tile0, 0)),
                      pl.BlockSpec((TM, LANES), lambda i, *_: (i + tile0, 0)),
                      pl.BlockSpec((1, 1, 6 * D_MODEL), mod_map),
                      pl.BlockSpec((1, D_MODEL), lambda i, *_: (0, 0)),
                      pl.BlockSpec((1, D_MODEL), lambda i, *_: (0, 0)),
                      pl.BlockSpec(memory_space=pl.ANY)],
            out_specs=pl.BlockSpec((TM, D_MODEL), lambda i, *_: (i, 0)),
            scratch_shapes=[pltpu.VMEM((2, COMB_ROWS // PAIR * ROW_TILES, LANES), U32),
                            pltpu.VMEM((TM, D_MODEL), F32),
                            pltpu.SemaphoreType.DMA((2,))]),
        out_shape=jax.ShapeDtypeStruct((n_tiles * TM, D_MODEL), F32),
        compiler_params=pltpu.CompilerParams(dimension_semantics=("arbitrary",),
                                             vmem_limit_bytes=VMEM_LIMIT_BYTES),
        name="combine",
    )(s_te, b_te, x_te, nx_t, rows_t, x1, rcol, mod3, ln2_g, ln2_b, eo)


def _prep_params(w_in, b_in, w_spatial, b_spatial, sgu_ln_g, sgu_ln_b, w_br_attn, w_br_sgu, w_out,
                 ln1_g, ln1_b, w_router, b_router):
    q_end, k_end, v_end = ATT_W, ATT_W + KV_W, ATT_W + 2 * KV_W

    def dup(w, lo):
        h0, h1 = w[..., lo:lo + HEAD_DIM], w[..., lo + HEAD_DIM:lo + 2 * HEAD_DIM]
        return jnp.concatenate([h0, h0, h1, h1], axis=-1)

    w_kv = jnp.concatenate([dup(w_in, q_end), dup(w_in, k_end)], axis=1).astype(BF16)
    b_kv = jnp.concatenate([dup(b_in, q_end), dup(b_in, k_end)], axis=0).reshape(1, -1)
    w_mix = w_in.astype(BF16)
    b_mix = b_in.reshape(1, -1)
    w_hi = w_router.astype(BF16)
    w_lo = (w_router - w_hi.astype(F32)).astype(BF16)
    p = dict(
        w_mix=w_mix, b_mix=b_mix,
        w_sp=w_spatial.astype(BF16), b_spt=b_spatial.T,
        sgu_g=sgu_ln_g.reshape(1, -1), sgu_b=sgu_ln_b.reshape(1, -1),
        w_ba=w_br_attn.astype(BF16), w_bs=w_br_sgu.astype(BF16), w_out=w_out.astype(BF16),
        ln1_g=ln1_g.reshape(1, -1), ln1_b=ln1_b.reshape(1, -1),
        w_r=jnp.concatenate([w_hi, w_lo], axis=1), b_r=b_router.reshape(1, -1),
    )
    return w_kv, b_kv, p


def _layer(seqs, xp, xs, c_all, w_ada, b_ada, w_in, b_in, sink, sgu_ln_g, sgu_ln_b, w_spatial, b_spatial,
           w_br_attn, w_br_sgu, w_out, ln1_g, ln1_b, w_router, b_router, w_up, b_up, w_down, b_down,
           ln2_g, ln2_b):
    T = seqs.n_tokens
    nt = T // TM
    ntp = seqs.n_prompt // TM
    c_pad = jnp.zeros((8, D_MODEL), F32).at[:c_all.shape[0]].set(c_all)
    mod3 = _ada(c_pad, w_ada, b_ada).reshape(8, 1, 6 * D_MODEL)
    cos_t, sin_t = _rope_tables(max(seqs.prompt_len, seqs.sample_len))
    w_kv, b_kv, p = _prep_params(w_in, b_in, w_spatial, b_spatial, sgu_ln_g, sgu_ln_b, w_br_attn,
                                 w_br_sgu, w_out, ln1_g, ln1_b, w_router, b_router)
    kt, v = _kv(seqs, xp, xs, mod3, cos_t, sin_t, w_kv, b_kv)
    x1, h2, logits = _mixer(seqs, sink, xp, xs, mod3, cos_t, sin_t, kt, v, p)
    rcol, rrow, cnt3 = _router(logits)

    cnt = cnt3[:, :, 0]
    count = cnt.sum(0)
    gb = EXPERT_SUBS * BM
    reg = (count + REGION_SLACK + gb - 1) // gb * gb
    pad_end = jnp.cumsum(reg)
    pad_start = pad_end - reg
    flat = lambda a: a.reshape(-1).astype(I32)
    s_te = flat(pad_start[None, :] + jnp.cumsum(cnt, axis=0) - cnt)
    d_nch = jnp.maximum((cnt + DISP_CHUNK - 1) // DISP_CHUNK, 1)
    c_nch = jnp.maximum((cnt + COMB_CHUNK - 1) // COMB_CHUNK, 1)
    db_te = flat(jnp.cumsum(cnt, axis=1) - cnt)
    cb_te = flat(COMB_CHUNK * (jnp.cumsum(c_nch, axis=1) - c_nch))
    rows_t = flat(COMB_CHUNK * c_nch.sum(1))
    n_blk = (nt * TILE_ROWS + N_EXPERTS * (REGION_SLACK + gb - 1)) // gb + 1
    tails = jnp.concatenate([pad_start + count, pad_end[:-1], jnp.array([n_blk * gb])]).astype(I32)
    blk_e = jnp.minimum(((jnp.arange(n_blk, dtype=I32) * gb)[:, None] >= pad_end[None, :]).sum(1),
                        N_EXPERTS - 1).astype(I32)
    sub_start = jnp.arange(n_blk * EXPERT_SUBS, dtype=I32) * BM
    owner = jnp.repeat(blk_e, EXPERT_SUBS)[:, None] == jnp.arange(N_EXPERTS)[None, :]
    sub_rows = (owner * (pad_start + count)[None, :]).sum(1) - sub_start
    blk_v = jnp.where(sub_start < pad_end[-1], jnp.clip(sub_rows, 0, BM), 0).astype(I32)

    xin = _dispatch(seqs, db_te, s_te, flat(d_nch - 1), flat((d_nch - 1).sum(1)), tails, h2, rrow, n_blk * gb)
    eo = _experts(blk_e, blk_v, xin, w_up, b_up, w_down, b_down, n_blk)
    comb = (s_te, cb_te, flat(c_nch - 1), flat((c_nch - 1).sum(1)), rows_t,
            x1, rcol, mod3, ln2_g.reshape(1, -1), ln2_b.reshape(1, -1), eo)
    return _combine(seqs, 0, ntp, *comb), _combine(seqs, ntp, nt - ntp, *comb)


def kernel(x_prompt, x_sample, c_prompt, c_sample, w_ada, b_ada, w_in, b_in, sink, sgu_ln_g, sgu_ln_b, w_spatial, b_spatial, w_br_attn, w_br_sgu, w_out, ln1_g, ln1_b, w_router, b_router, w_up, b_up, w_down, b_down, ln2_g, ln2_b):
    assert w_ada.shape[0] == DEPTH == 1
    bp, sp, d = x_prompt.shape
    bs, ss, _ = x_sample.shape
    seqs = _Seqs(n_prompt=bp * sp, prompt_len=sp, sample_len=ss, n_tokens=bp * sp + bs * ss)
    c_all = jnp.concatenate([c_prompt, c_sample], axis=0)
    yp, ys = _layer(seqs, x_prompt.reshape(bp * sp, d), x_sample.reshape(bs * ss, d), c_all,
                    w_ada[0], b_ada[0], w_in[0], b_in[0], sink[0], sgu_ln_g[0], sgu_ln_b[0],
                    w_spatial[0], b_spatial[0], w_br_attn[0], w_br_sgu[0], w_out[0], ln1_g[0], ln1_b[0],
                    w_router[0], b_router[0], w_up[0], b_up[0], w_down[0], b_down[0], ln2_g[0], ln2_b[0])
    return (yp.reshape(bp, sp, d), ys.reshape(bs, ss, d))
```

```python
import functools
import math
from typing import NamedTuple

import jax
import jax.numpy as jnp
from jax import lax
from jax.experimental import pallas as pl
from jax.experimental.pallas import tpu as pltpu

F32 = jnp.float32
BF16 = jnp.bfloat16
I32 = jnp.int32

D_MODEL = 1024
N_HEADS = 8
N_KV_HEADS = 2
HEAD_DIM = 64
ATT_W = N_HEADS * HEAD_DIM
KV_W = N_KV_HEADS * HEAD_DIM
BLOCK = 128
ROPE_THETA = 500000.0
ROT_DIM = HEAD_DIM // 4
ROT_HALF = ROT_DIM // 2
SGU_W = D_MODEL // 2
SGU_GROUPS = 4
N_EXPERTS = 32
TOP_K = 4
D_FF = D_MODEL
SWIGLU_LIMIT = 7.0
SWIGLU_ALPHA = 1.702
LN_EPS = 1e-5
DEPTH = 1
DN_ALPHA = (2 * DEPTH) ** 0.25

LANES = 128
MXU_DEPTH = 256
ROW_TILES = D_MODEL // LANES
VMEM_LIMIT_BYTES = 56 * 1024 * 1024
U32 = jnp.uint32
PAIR = 2

TM = 256
MT = 512
TK = 1024
BM = 512
EXPERT_SUBS = 2
ROUTER_TOKENS = 2048
DISP_CHUNK = 48
COMB_CHUNK = 40
REGION_SLACK = DISP_CHUNK
TILE_ROWS = TM * TOP_K + N_EXPERTS
DISP_ROWS = TILE_ROWS + DISP_CHUNK
DISP_BLOCK = DISP_ROWS // 3
assert DISP_BLOCK * 3 == DISP_ROWS and DISP_BLOCK % 16 == 0
COMB_ROWS = -(-(TILE_ROWS + N_EXPERTS * COMB_CHUNK) // MXU_DEPTH) * MXU_DEPTH
NEG_INF = float("-inf")


class _Seqs(NamedTuple):
    n_prompt: int
    prompt_len: int
    sample_len: int
    n_tokens: int


def _tile_pos(seqs, t0):
    is_s = t0 >= seqs.n_prompt
    seq_len = jnp.where(is_s, seqs.sample_len, seqs.prompt_len)
    off = jnp.where(is_s, t0 - seqs.n_prompt, t0)
    pos0 = off % seq_len
    row = jnp.where(is_s, seqs.n_prompt // seqs.prompt_len + off // seq_len, off // seq_len)
    return seq_len, pos0, row


def _ada_kernel(c_ref, w_ref, b_ref, o_ref):
    c = c_ref[...]
    a = c * jax.nn.sigmoid(c)
    o_ref[...] = jnp.dot(a, w_ref[...], preferred_element_type=F32,
                         precision=lax.Precision.HIGHEST) + b_ref[...]


def _ada(c_pad, w_ada, b_ada):
    n = w_ada.shape[1]
    bn = n // 4
    return pl.pallas_call(
        _ada_kernel,
        grid=(n // bn,),
        in_specs=[pl.BlockSpec((8, D_MODEL), lambda j: (0, 0)),
                  pl.BlockSpec((D_MODEL, bn), lambda j: (0, j)),
                  pl.BlockSpec((1, bn), lambda j: (0, j))],
        out_specs=pl.BlockSpec((8, bn), lambda j: (0, j)),
        out_shape=jax.ShapeDtypeStruct((8, n), F32),
        compiler_params=pltpu.CompilerParams(vmem_limit_bytes=VMEM_LIMIT_BYTES),
        name="ada",
    )(c_pad, w_ada, b_ada.reshape(1, n))


def _rope_tables(length):
    inv = ROPE_THETA ** (-jnp.arange(ROT_HALF, dtype=F32) * 2.0 / ROT_DIM)
    ang = jnp.arange(length, dtype=F32)[:, None] * inv[None, :]
    lane = jnp.arange(LANES) % HEAD_DIM
    spread = ((lane[None, :] % ROT_HALF == jnp.arange(ROT_HALF)[:, None]) & (lane[None, :] < ROT_DIM)).astype(F32)
    sign = jnp.where(lane < ROT_HALF, -1.0, 1.0)
    exact = dict(precision=lax.Precision.HIGHEST, preferred_element_type=F32)
    cos = jnp.dot(jnp.cos(ang), spread, **exact) + (lane >= ROT_DIM).astype(F32)[None, :]
    sin = jnp.dot(jnp.sin(ang), spread * sign[None, :], **exact)
    return cos, sin


def _rope(x, cos, sin):
    n = x.shape[1]
    reps = n // LANES
    c = jnp.concatenate([cos] * reps, axis=1)
    s = jnp.concatenate([sin] * reps, axis=1)
    lane = lax.broadcasted_iota(I32, x.shape, 1)
    first = (lane & (HEAD_DIM - 1)) < ROT_HALF
    partner = jnp.where(first, pltpu.roll(x, n - ROT_HALF, 1), pltpu.roll(x, ROT_HALF, 1))
    return x * c + partner * s


def _kv_kernel(seqs, xp_ref, xs_ref, mod_ref, cos_ref, sin_ref, w_ref, b_ref, kt_ref, v_ref):
    mod = mod_ref[0]
    sh1 = mod[:, 0:D_MODEL]
    sc1 = mod[:, D_MODEL:2 * D_MODEL]
    x = jnp.where(pl.program_id(0) * TK >= seqs.n_prompt, xs_ref[...], xp_ref[...])
    h = (x * (1.0 + sc1) + sh1).astype(BF16)
    kv = jnp.dot(h, w_ref[...], preferred_element_type=F32) + b_ref[...]
    k = _rope(kv[:, 0:2 * LANES], cos_ref[...], sin_ref[...])
    kt_ref[...] = k.T.astype(BF16)
    v_ref[...] = kv[:, 2 * LANES:4 * LANES].astype(BF16)


def _group_x_specs(seqs, tile):
    ntp = seqs.n_prompt // tile
    return [pl.BlockSpec((tile, D_MODEL), lambda i, *_: (jnp.minimum(i, ntp - 1), 0)),
            pl.BlockSpec((tile, D_MODEL), lambda i, *_: (jnp.maximum(i - ntp, 0), 0))]


def _kv(seqs, xp, xs, mod3, cos_t, sin_t, w_kv, b_kv):
    T = seqs.n_tokens

    def mod_map(i):
        return (_tile_pos(seqs, i * TK)[2], 0, 0)

    def rope_map(i):
        return (_tile_pos(seqs, i * TK)[1] // TK, 0)

    return pl.pallas_call(
        functools.partial(_kv_kernel, seqs),
        grid=(T // TK,),
        in_specs=_group_x_specs(seqs, TK) + [
                  pl.BlockSpec((1, 1, 6 * D_MODEL), mod_map),
                  pl.BlockSpec((TK, LANES), rope_map),
                  pl.BlockSpec((TK, LANES), rope_map),
                  pl.BlockSpec((D_MODEL, 4 * LANES), lambda i: (0, 0)),
                  pl.BlockSpec((1, 4 * LANES), lambda i: (0, 0))],
        out_specs=[pl.BlockSpec((2 * LANES, TK), lambda i: (0, i)),
                   pl.BlockSpec((TK, 2 * LANES), lambda i: (i, 0))],
        out_shape=[jax.ShapeDtypeStruct((2 * LANES, T), BF16),
                   jax.ShapeDtypeStruct((T, 2 * LANES), BF16)],
        compiler_params=pltpu.CompilerParams(dimension_semantics=("arbitrary",),
                                             vmem_limit_bytes=VMEM_LIMIT_BYTES),
        name="kv",
    )(xp, xs, mod3, cos_t, sin_t, w_kv, b_kv)


def _layer_norm(x, g, b):
    mu = jnp.mean(x, axis=-1, keepdims=True)
    xc = x - mu
    var = jnp.mean(xc * xc, axis=-1, keepdims=True)
    return xc * lax.rsqrt(var + LN_EPS) * g + b


def _attention(q, kfull, vfull, valids, sink_ref):
    lane = lax.broadcasted_iota(I32, (BLOCK, LANES), 1)
    lo = lane < HEAD_DIM
    ones = jnp.ones((3 * BLOCK, LANES), BF16)
    units = [(jb, hk) for jb in range(MT // BLOCK) for hk in range(N_KV_HEADS)]
    scores, sinks = [], []
    for jb, hk in units:
        parts = []
        for p in range(2):
            qp = q[jb * BLOCK:(jb + 1) * BLOCK, (2 * hk + p) * LANES:(2 * hk + p + 1) * LANES]
            parts.append(jnp.where(lo, qp, 0.0).astype(BF16))
            parts.append(jnp.where(lo, 0.0, qp).astype(BF16))
        kwin = kfull[hk * LANES:(hk + 1) * LANES, jb * BLOCK:(jb + 3) * BLOCK]
        s = jnp.dot(jnp.concatenate(parts, axis=0), kwin, preferred_element_type=F32)
        scores.append(jnp.where(jnp.concatenate([valids[jb]] * 4, axis=0), s, NEG_INF))
        sinks.extend(jnp.full((BLOCK, 1), sink_ref[hk * 4 + g], F32) for g in range(4))
    s = jnp.concatenate(scores, axis=0)
    sk = jnp.concatenate(sinks, axis=0)
    m = jnp.maximum(jnp.max(s, axis=-1, keepdims=True), sk)
    p = jnp.exp(s - m).astype(BF16)
    sink_term = jnp.exp(sk - m)
    rows = []
    for ui, (jb, hk) in enumerate(units):
        vwin = jnp.concatenate([vfull[jb * BLOCK:(jb + 3) * BLOCK, hk * LANES:(hk + 1) * LANES], ones], axis=1)
        r0 = ui * 4 * BLOCK
        ov = jnp.dot(p[r0:r0 + 4 * BLOCK], vwin, preferred_element_type=F32)
        o = ov[:, 0:LANES] / (ov[:, LANES:2 * LANES] + sink_term[r0:r0 + 4 * BLOCK])
        pair = [jnp.where(lo, o[(2 * p2) * BLOCK:(2 * p2 + 1) * BLOCK],
                          o[(2 * p2 + 1) * BLOCK:(2 * p2 + 2) * BLOCK]) for p2 in range(2)]
        rows.append(jnp.concatenate(pair, axis=1))
    n_h = N_KV_HEADS
    return jnp.concatenate(
        [jnp.concatenate(rows[jb * n_h:(jb + 1) * n_h], axis=1) for jb in range(MT // BLOCK)], axis=0)


def _mixer_kernel(seqs, sink_ref, xp_ref, xs_ref, mod_ref, cos_ref, sin_ref,
                  ktp_ref, ktc_ref, ktn_ref, vp_ref, vc_ref, vn_ref,
                  wmix_ref, bmix_ref, wsp_ref, bspt_ref, sg_ref, sb_ref,
                  wba_ref, wbs_ref, wout_ref, l1g_ref, l1b_ref, wr_ref, br_ref,
                  x1_ref, h2_ref, logit_ref):
    i = pl.program_id(0)
    seq_len, pos0, _ = _tile_pos(seqs, i * MT)
    mod = mod_ref[0]
    sh1, sc1, g1 = (mod[:, j * D_MODEL:(j + 1) * D_MODEL] for j in range(3))
    sh2, sc2 = (mod[:, j * D_MODEL:(j + 1) * D_MODEL] for j in range(3, 5))
    x = jnp.where(i * MT >= seqs.n_prompt, xs_ref[...], xp_ref[...])
    h = (x * (1.0 + sc1) + sh1).astype(BF16)
    kv_end = ATT_W + 2 * KV_W
    zq = jnp.dot(h, wmix_ref[:, 0:ATT_W], preferred_element_type=F32) + bmix_ref[:, 0:ATT_W]
    z = jnp.dot(h, wmix_ref[:, kv_end:], preferred_element_type=F32) + bmix_ref[:, kv_end:]
    q = _rope(zq, cos_ref[...], sin_ref[...]) * (HEAD_DIM ** -0.5)
    u = jax.nn.gelu(z[:, 0:SGU_W])
    vs = _layer_norm(jax.nn.gelu(z[:, SGU_W:2 * SGU_W]), sg_ref[...], sb_ref[...])
    ga = z[:, 2 * SGU_W:2 * SGU_W + D_MODEL]
    gs = z[:, 2 * SGU_W + D_MODEL:]

    kfull = jnp.concatenate([ktp_ref[...], ktc_ref[...], ktn_ref[...]], axis=1)
    vfull = jnp.concatenate([vp_ref[...], vc_ref[...], vn_ref[...]], axis=0)
    qi = lax.broadcasted_iota(I32, (BLOCK, 3 * BLOCK), 0)
    ki = lax.broadcasted_iota(I32, (BLOCK, 3 * BLOCK), 1)
    band = (ki >= qi) & (ki <= qi + 2 * BLOCK)
    vs_b = vs.astype(BF16)
    valids, sgu_rows = [], []
    for jb in range(MT // BLOCK):
        posb = pos0 + jb * BLOCK
        valids.append(band & (ki >= jnp.where(posb == 0, BLOCK, 0))
                      & (ki < jnp.where(posb + BLOCK == seq_len, 2 * BLOCK, 3 * BLOCK)))
        groups = []
        for g in range(SGU_GROUPS):
            vg = vs_b[jb * BLOCK:(jb + 1) * BLOCK, g * LANES:(g + 1) * LANES]
            sv = jnp.dot(wsp_ref[g], vg, preferred_element_type=F32) + bspt_ref[:, g:g + 1]
            groups.append(sv)
        sgu_rows.append(jnp.concatenate(groups, axis=1))
    attn = _attention(q, kfull, vfull, valids, sink_ref)
    sgu = u * jnp.concatenate(sgu_rows, axis=0)

    a1 = jnp.dot(attn.astype(BF16), wba_ref[...], preferred_element_type=F32)
    a2 = jnp.dot(sgu.astype(BF16), wbs_ref[...], preferred_element_type=F32)
    merged = jax.nn.sigmoid(ga) * a1 + jax.nn.sigmoid(gs) * a2
    mix = jnp.dot(merged.astype(BF16), wout_ref[...], preferred_element_type=F32)
    x1 = _layer_norm(DN_ALPHA * x + g1 * mix, l1g_ref[...], l1b_ref[...])
    x1_ref[...] = x1
    h2 = x1 * (1.0 + sc2) + sh2
    hi = h2.astype(BF16)
    h2_ref[...] = hi

    lo_part = (h2 - hi.astype(F32)).astype(BF16)
    l1 = jnp.dot(hi, wr_ref[...], preferred_element_type=F32)
    l2 = jnp.dot(lo_part, wr_ref[:, 0:N_EXPERTS], preferred_element_type=F32)
    logit_ref[...] = l1[:, 0:N_EXPERTS] + l1[:, N_EXPERTS:2 * N_EXPERTS] + l2 + br_ref[...]


PACK = 4096.0
assert DISP_ROWS <= PACK and COMB_ROWS <= PACK and DISP_ROWS * PACK < 2 ** 24


def _router_kernel(l_ref, rcol_ref, rrow_ref, cnt_ref):
    rt = l_ref.shape[0]
    ns = rt // TM
    padded = jnp.concatenate([l_ref[...], jnp.zeros((rt, LANES - N_EXPERTS), F32)], axis=1)
    work = padded.T[0:N_EXPERTS]
    eidx = lax.broadcasted_iota(I32, (N_EXPERTS, rt), 0)
    idxs, vals = [], []
    for _ in range(TOP_K):
        m = jnp.max(work, axis=0, keepdims=True)
        ix = jnp.min(jnp.where(work == m, eidx, N_EXPERTS), axis=0, keepdims=True)
        idxs.append(ix)
        vals.append(m)
        work = jnp.where(eidx == ix, NEG_INF, work)
    exps = [jnp.exp(v - vals[0]) for v in vals]
    esum = exps[0] + exps[1] + exps[2] + exps[3]
    wts = [e / esum for e in exps]

    sel = jnp.zeros((N_EXPERTS, rt), F32)
    for ix in idxs:
        sel = sel + jnp.where(eidx == ix, 1.0, 0.0)
    sel_b = sel.astype(BF16)
    ti = lax.broadcasted_iota(I32, (TM, TM), 0)
    tj = lax.broadcasted_iota(I32, (TM, TM), 1)
    earlier = jnp.where(ti < tj, 1.0, 0.0).astype(BF16)
    tiles = [slice(s * TM, (s + 1) * TM) for s in range(ns)]
    ranks = [jnp.dot(sel_b[:, t], earlier, preferred_element_type=F32) for t in tiles]
    cnts, nchs = [], []
    for t in tiles:
        cnt = jnp.sum(sel[:, t], axis=1, keepdims=True)
        cnt = cnt + (cnt - 2.0 * jnp.floor(cnt * 0.5))
        cnts.append(cnt)
        nchs.append(jnp.maximum(jnp.floor((cnt + (COMB_CHUNK - 0.5)) * (1.0 / COMB_CHUNK)), 1.0))
    lane = lax.broadcasted_iota(I32, (N_EXPERTS, LANES), 1)
    pre = jnp.zeros((N_EXPERTS, LANES), F32)
    for j, col in enumerate(cnts + nchs):
        pre = jnp.where(lane == j, col, pre)
    ei = lax.broadcasted_iota(I32, (N_EXPERTS, N_EXPERTS), 0)
    ej = lax.broadcasted_iota(I32, (N_EXPERTS, N_EXPERTS), 1)
    before = jnp.where(ej < ei, 1.0, 0.0).astype(BF16)
    base = jnp.dot(before, pre.astype(BF16), preferred_element_type=F32)
    both = jnp.concatenate(
        [(ranks[s] + base[:, s:s + 1]) * PACK + (ranks[s] + base[:, ns + s:ns + s + 1] * COMB_CHUNK)
         for s in range(ns)], axis=1)

    packed = [jnp.sum(jnp.where(eidx == ix, both, 0.0), axis=0, keepdims=True) for ix in idxs]
    drow = [jnp.floor(v * (1.0 / PACK)) for v in packed]
    table = [ix.astype(F32) for ix in idxs] + drow + [v - d * PACK for v, d in zip(packed, drow)] + wts
    sub = lax.broadcasted_iota(I32, (LANES, rt), 0)
    rr = jnp.zeros((LANES, rt), F32)
    for j, row in enumerate(table):
        rr = jnp.where(sub == j, row, rr)
    rcol_ref[...] = rr.T
    for s in range(ns):
        rrow_ref[s] = rr[0:16, tiles[s]]
        cnt_ref[s] = jnp.broadcast_to(cnts[s], (N_EXPERTS, LANES)).astype(I32)


def _router(logits):
    T = logits.shape[0]
    rt = math.gcd(T, ROUTER_TOKENS)
    nt = T // TM
    assert 2 * (rt // TM) <= LANES
    return pl.pallas_call(
        _router_kernel,
        grid=(T // rt,),
        in_specs=[pl.BlockSpec((rt, N_EXPERTS), lambda i: (i, 0))],
        out_specs=[pl.BlockSpec((rt, LANES), lambda i: (i, 0)),
                   pl.BlockSpec((rt // TM, 16, TM), lambda i: (i, 0, 0)),
                   pl.BlockSpec((rt // TM, N_EXPERTS, LANES), lambda i: (i, 0, 0))],
        out_shape=[jax.ShapeDtypeStruct((T, LANES), F32),
                   jax.ShapeDtypeStruct((nt, 16, TM), F32),
                   jax.ShapeDtypeStruct((nt, N_EXPERTS, LANES), I32)],
        compiler_params=pltpu.CompilerParams(dimension_semantics=("arbitrary",),
                                             vmem_limit_bytes=VMEM_LIMIT_BYTES),
        name="router",
    )(logits)


def _mixer(seqs, sink, xp, xs, mod3, cos_t, sin_t, kt, v, p):
    T = seqs.n_tokens
    nt = T // MT
    nb = T // BLOCK
    r = MT // BLOCK

    def mod_map(i, s):
        return (_tile_pos(seqs, i * MT)[2], 0, 0)

    def rope_map(i, s):
        return (_tile_pos(seqs, i * MT)[1] // MT, 0)

    const2 = lambda i, s: (0, 0)
    once = dict(pipeline_mode=pl.Buffered(1))
    in_specs = _group_x_specs(seqs, MT) + [
        pl.BlockSpec((1, 1, 6 * D_MODEL), mod_map),
        pl.BlockSpec((MT, LANES), rope_map),
        pl.BlockSpec((MT, LANES), rope_map),
        pl.BlockSpec((2 * LANES, BLOCK), lambda i, s: (0, jnp.maximum(i * r - 1, 0))),
        pl.BlockSpec((2 * LANES, MT), lambda i, s: (0, i)),
        pl.BlockSpec((2 * LANES, BLOCK), lambda i, s: (0, jnp.minimum(i * r + r, nb - 1))),
        pl.BlockSpec((BLOCK, 2 * LANES), lambda i, s: (jnp.maximum(i * r - 1, 0), 0)),
        pl.BlockSpec((MT, 2 * LANES), lambda i, s: (i, 0)),
        pl.BlockSpec((BLOCK, 2 * LANES), lambda i, s: (jnp.minimum(i * r + r, nb - 1), 0)),
        pl.BlockSpec(p["w_mix"].shape, const2, **once),
        pl.BlockSpec(p["b_mix"].shape, const2, **once),
        pl.BlockSpec(p["w_sp"].shape, lambda i, s: (0, 0, 0), **once),
        pl.BlockSpec(p["b_spt"].shape, const2, **once),
        pl.BlockSpec(p["sgu_g"].shape, const2, **once),
        pl.BlockSpec(p["sgu_b"].shape, const2, **once),
        pl.BlockSpec(p["w_ba"].shape, const2, **once),
        pl.BlockSpec(p["w_bs"].shape, const2, **once),
        pl.BlockSpec(p["w_out"].shape, const2, **once),
        pl.BlockSpec(p["ln1_g"].shape, const2, **once),
        pl.BlockSpec(p["ln1_b"].shape, const2, **once),
        pl.BlockSpec(p["w_r"].shape, const2, **once),
        pl.BlockSpec(p["b_r"].shape, const2, **once),
    ]
    out_specs = [
        pl.BlockSpec((MT, D_MODEL), lambda i, s: (i, 0)),
        pl.BlockSpec((MT, D_MODEL), lambda i, s: (i, 0)),
        pl.BlockSpec((MT, N_EXPERTS), lambda i, s: (i, 0)),
    ]
    out_shape = [
        jax.ShapeDtypeStruct((T, D_MODEL), F32),
        jax.ShapeDtypeStruct((T, D_MODEL), BF16),
        jax.ShapeDtypeStruct((T, N_EXPERTS), F32),
    ]
    return pl.pallas_call(
        functools.partial(_mixer_kernel, seqs),
        grid_spec=pltpu.PrefetchScalarGridSpec(
            num_scalar_prefetch=1, grid=(nt,), in_specs=in_specs, out_specs=out_specs),
        out_shape=out_shape,
        compiler_params=pltpu.CompilerParams(dimension_semantics=("arbitrary",),
                                             vmem_limit_bytes=VMEM_LIMIT_BYTES),
        name="mixer",
    )(sink, xp, xs, mod3, cos_t, sin_t, kt, kt, kt, v, v, v,
      p["w_mix"], p["b_mix"], p["w_sp"], p["b_spt"], p["sgu_g"], p["sgu_b"],
      p["w_ba"], p["w_bs"], p["w_out"], p["ln1_g"], p["ln1_b"], p["w_r"], p["b_r"])


def _pair_rows(row):
    return pl.multiple_of((row // PAIR) * ROW_TILES, ROW_TILES)


def _to_row_tiles(dst_ref, rows, n, start=0):
    words = pltpu.bitcast(rows, U32)
    base = start // PAIR * ROW_TILES
    for c in range(ROW_TILES):
        dst_ref[pl.ds(base + c, n // PAIR, stride=ROW_TILES), :] = words[:, c * LANES:(c + 1) * LANES]


def _from_row_tiles(src_ref, start, n):
    base = start // PAIR * ROW_TILES
    words = jnp.concatenate(
        [src_ref[pl.ds(base + c, n // PAIR, stride=ROW_TILES), :] for c in range(ROW_TILES)], axis=1)
    return pltpu.bitcast(words, BF16)


WAIT_GROUP = 16


def _issue_tile(copy, k0, src_ref, dst_ref, extra_ref, n_extra, step):
    for e in range(N_EXPERTS):
        copy(src_ref[k0 + e], dst_ref[k0 + e]).start()

    @pl.when(n_extra > 0)
    def _():
        def per_expert(e, c):
            def per_chunk(j, c2):
                copy(src_ref[k0 + e] + j * step, dst_ref[k0 + e] + j * step).start()
                return c2
            lax.fori_loop(1, extra_ref[k0 + e] + 1, per_chunk, 0)
            return c
        lax.fori_loop(0, N_EXPERTS, per_expert, 0)


def _wait_tile(group_copy, chunk_copy, n_extra):
    for _ in range(N_EXPERTS // WAIT_GROUP):
        group_copy.wait()

    def body(_, c):
        chunk_copy.wait()
        return c
    lax.fori_loop(0, n_extra, body, 0)


def _dispatch_kernel(b_ref, s_ref, x_ref, nx_ref, tail_ref, h2_ref, rrow_ref, xin_ref, stg_ref, zero_ref, sem):
    i = pl.program_id(0)
    nt = pl.num_programs(0)
    slot = i % 2
    chunk = DISP_CHUNK // PAIR * ROW_TILES

    def copy(src_row, dst_row, sl):
        return pltpu.make_async_copy(stg_ref.at[sl, pl.ds(_pair_rows(src_row), chunk)],
                                     xin_ref.at[pl.ds(_pair_rows(dst_row), chunk)], sem)

    def wait_tile(tile):
        group = pltpu.make_async_copy(stg_ref.at[0, pl.ds(0, WAIT_GROUP * chunk)],
                                      xin_ref.at[pl.ds(0, WAIT_GROUP * chunk)], sem)
        _wait_tile(group, copy(0, 0, 0), nx_ref[tile])

    def issue(tile, sl):
        _issue_tile(lambda src, dst: copy(src, dst, sl), tile * N_EXPERTS, b_ref, s_ref, x_ref, nx_ref[tile],
                    DISP_CHUNK)

    @pl.when(i > 1)
    def _():
        wait_tile(i - 2)

    @pl.when(i > 0)
    def _():
        issue(i - 1, 1 - slot)

    rr = rrow_ref[0]
    pos = [rr[TOP_K + k:TOP_K + k + 1].astype(I32) for k in range(TOP_K)]
    for a in range(DISP_ROWS // DISP_BLOCK):
        rho = lax.broadcasted_iota(I32, (DISP_BLOCK, TM), 0) + a * DISP_BLOCK
        pt = jnp.zeros((DISP_BLOCK, TM), F32)
        for k in range(TOP_K):
            pt = pt + jnp.where(rho == pos[k], 1.0, 0.0)
        rows = jnp.dot(pt.astype(BF16), h2_ref[...], preferred_element_type=F32).astype(BF16)
        _to_row_tiles(stg_ref.at[slot], rows, DISP_BLOCK, a * DISP_BLOCK)

    @pl.when(i == nt - 1)
    def _():
        @pl.when(i > 0)
        def _():
            wait_tile(i - 1)
        issue(i, slot)
        wait_tile(i)
        zero_ref[...] = jnp.zeros_like(zero_ref)

        def zcopy(dst_row):
            return pltpu.make_async_copy(zero_ref, xin_ref.at[pl.ds(_pair_rows(dst_row), chunk)], sem)

        def zwait(count):
            def body(_, c):
                zcopy(0).wait()
                return c
            lax.fori_loop(0, count, body, 0)

        def per_expert(e, total):
            lo = tail_ref[e]
            nz = (tail_ref[N_EXPERTS + e] - lo) // DISP_CHUNK

            def per_chunk(j, c):
                zcopy(lo + j * DISP_CHUNK).start()
                return c
            lax.fori_loop(0, nz, per_chunk, 0)
            return total + nz
        zwait(lax.fori_loop(0, N_EXPERTS, per_expert, 0))

        def last_chunk(e, c):
            zcopy(tail_ref[N_EXPERTS + e] - DISP_CHUNK).start()
            return c
        lax.fori_loop(0, N_EXPERTS, last_chunk, 0)
        zwait(N_EXPERTS)


def _dispatch(seqs, b_te, s_te, x_te, nx_t, tails, h2, rrow, n_rows):
    nt = seqs.n_tokens // TM
    return pl.pallas_call(
        _dispatch_kernel,
        grid_spec=pltpu.PrefetchScalarGridSpec(
            num_scalar_prefetch=5, grid=(nt,),
            in_specs=[pl.BlockSpec((TM, D_MODEL), lambda i, *_: (i, 0)),
                      pl.BlockSpec((1, 16, TM), lambda i, *_: (i, 0, 0))],
            out_specs=pl.BlockSpec(memory_space=pl.ANY),
            scratch_shapes=[pltpu.VMEM((2, DISP_ROWS // PAIR * ROW_TILES, LANES), U32),
                            pltpu.VMEM((DISP_CHUNK // PAIR * ROW_TILES, LANES), U32),
                            pltpu.SemaphoreType.DMA]),
        out_shape=jax.ShapeDtypeStruct((n_rows // PAIR * ROW_TILES, LANES), U32),
        compiler_params=pltpu.CompilerParams(dimension_semantics=("arbitrary",),
                                             vmem_limit_bytes=VMEM_LIMIT_BYTES),
        name="dispatch",
    )(b_te, s_te, x_te, nx_t, tails, h2, rrow)


def _expert_kernel(be_ref, bv_ref, x_ref, wup_hbm, bup_ref, wdn_hbm, bdn_ref, o_ref,
                   wup_f, wdn_f, wup_b, wdn_b, sem):
    i = pl.program_id(0)
    e = be_ref[i]
    first = jnp.logical_or(i == 0, e != be_ref[jnp.maximum(i - 1, 0)])

    def fetch(ex):
        slot = ex % 2
        return (pltpu.make_async_copy(wup_hbm.at[ex], wup_f.at[slot], sem.at[0, slot]),
                pltpu.make_async_copy(wdn_hbm.at[ex], wdn_f.at[slot], sem.at[1, slot]))

    @pl.when(i == 0)
    def _():
        for d in fetch(e):
            d.start()

    @pl.when(first)
    def _():
        for d in fetch(e):
            d.wait()

        @pl.when(e + 1 < N_EXPERTS)
        def _():
            for d in fetch(e + 1):
                d.start()

        @pl.when(bv_ref[i * EXPERT_SUBS] > 0)
        def _():
            wup_b[...] = wup_f[e % 2].astype(BF16)
            wdn_b[...] = wdn_f[e % 2].astype(BF16)

    def ffn(start, n, valid):
        x = _from_row_tiles(x_ref, start, n)
        row = lax.broadcasted_iota(I32, (n, 1), 0)
        xb = jnp.where(row < valid, x, jnp.zeros_like(x))
        hu = jnp.dot(xb, wup_b[...], preferred_element_type=F32) + bup_ref[0]
        gate = jnp.minimum(hu[:, 0:D_FF], SWIGLU_LIMIT)
        lin = jnp.clip(hu[:, D_FF:], -SWIGLU_LIMIT, SWIGLU_LIMIT)
        act = gate * jax.nn.sigmoid(SWIGLU_ALPHA * gate) * (lin + 1.0)
        y = jnp.dot(act.astype(BF16), wdn_b[...], preferred_element_type=F32) + bdn_ref[0]
        _to_row_tiles(o_ref, y.astype(BF16), n, start)

    def zero(start, n):
        o_ref[start // PAIR * ROW_TILES:(start + n) // PAIR * ROW_TILES, :] = jnp.zeros(
            (n // PAIR * ROW_TILES, LANES), U32)

    half = BM // 2
    for sub in range(EXPERT_SUBS):
        valid = bv_ref[i * EXPERT_SUBS + sub]
        start = sub * BM

        @pl.when(valid > half)
        def _():
            ffn(start, BM, valid)

        @pl.when(jnp.logical_and(valid > 0, valid <= half))
        def _():
            ffn(start, half, valid)
            zero(start + half, half)

        @pl.when(valid == 0)
        def _():
            zero(start, BM)


def _experts(blk_e, blk_v, xin, w_up, b_up, w_down, b_down, n_blk):
    gb = EXPERT_SUBS * BM
    return pl.pallas_call(
        _expert_kernel,
        grid_spec=pltpu.PrefetchScalarGridSpec(
            num_scalar_prefetch=2, grid=(n_blk,),
            in_specs=[pl.BlockSpec((gb // PAIR * ROW_TILES, LANES), lambda i, be, bv: (i, 0)),
                      pl.BlockSpec(memory_space=pl.ANY),
                      pl.BlockSpec((1, 1, 2 * D_FF), lambda i, be, bv: (be[i], 0, 0)),
                      pl.BlockSpec(memory_space=pl.ANY),
                      pl.BlockSpec((1, 1, D_MODEL), lambda i, be, bv: (be[i], 0, 0))],
            out_specs=pl.BlockSpec((gb // PAIR * ROW_TILES, LANES), lambda i, be, bv: (i, 0)),
            scratch_shapes=[pltpu.VMEM((2, D_MODEL, 2 * D_FF), F32),
                            pltpu.VMEM((2, D_FF, D_MODEL), F32),
                            pltpu.VMEM((D_MODEL, 2 * D_FF), BF16),
                            pltpu.VMEM((D_FF, D_MODEL), BF16),
                            pltpu.SemaphoreType.DMA((2, 2))]),
        out_shape=jax.ShapeDtypeStruct((n_blk * gb // PAIR * ROW_TILES, LANES), U32),
        compiler_params=pltpu.CompilerParams(dimension_semantics=("arbitrary",),
                                             vmem_limit_bytes=VMEM_LIMIT_BYTES),
        name="experts",
    )(blk_e, blk_v, xin, w_up, b_up.reshape(N_EXPERTS, 1, 2 * D_FF), w_down,
      b_down.reshape(N_EXPERTS, 1, D_MODEL))


COMB_KC = COMB_ROWS // MXU_DEPTH
COMB_KC_MIN = N_EXPERTS * COMB_CHUNK // MXU_DEPTH
assert COMB_KC_MIN * MXU_DEPTH == N_EXPERTS * COMB_CHUNK


def _combine_kernel(tile0, s_ref, b_ref, x_ref, nx_ref, rows_ref, x1_ref, rcol_ref, mod_ref, g_ref, bb_ref,
                    eo_ref, y_ref, stg_ref, acc_ref, sem):
    i = pl.program_id(0)
    nt = pl.num_programs(0)
    slot = i % 2
    chunk = COMB_CHUNK // PAIR * ROW_TILES

    def copy(src_row, dst_row, sl):
        return pltpu.make_async_copy(eo_ref.at[pl.ds(_pair_rows(src_row), chunk)],
                                     stg_ref.at[sl, pl.ds(_pair_rows(dst_row), chunk)], sem.at[sl])

    def issue(tile, sl):
        _issue_tile(lambda src, dst: copy(src, dst, sl), tile * N_EXPERTS, s_ref, b_ref, x_ref, nx_ref[tile],
                    COMB_CHUNK)

    @pl.when(i == 0)
    def _():
        stg_ref[...] = jnp.zeros_like(stg_ref)
        issue(tile0, 0)

    rc = rcol_ref[...]
    col_k = [rc[:, 2 * TOP_K + k:2 * TOP_K + k + 1].astype(I32) for k in range(TOP_K)]
    w_k = [rc[:, 3 * TOP_K + k:3 * TOP_K + k + 1] for k in range(TOP_K)]

    def weights(c):
        jl = lax.broadcasted_iota(I32, (TM, MXU_DEPTH), 1) + c * MXU_DEPTH
        pm = jnp.zeros((TM, MXU_DEPTH), F32)
        for k in range(TOP_K):
            pm = pm + jnp.where(jl == col_k[k], w_k[k], 0.0)
        return pm.astype(BF16)

    @pl.when(i + 1 < nt)
    def _():
        issue(tile0 + i + 1, 1 - slot)

    stg = stg_ref.at[slot]
    group = pltpu.make_async_copy(eo_ref.at[pl.ds(0, WAIT_GROUP * chunk)],
                                  stg.at[pl.ds(0, WAIT_GROUP * chunk)], sem.at[slot])
    _wait_tile(group, copy(0, 0, slot), nx_ref[tile0 + i])

    pm = jnp.concatenate([weights(c) for c in range(COMB_KC_MIN)], axis=1)
    acc_ref[...] = jnp.dot(pm, _from_row_tiles(stg, 0, COMB_KC_MIN * MXU_DEPTH), preferred_element_type=F32)
    used = (rows_ref[tile0 + i] + MXU_DEPTH - 1) // MXU_DEPTH
    for c in range(COMB_KC_MIN, COMB_KC):
        @pl.when(c < used)
        def _():
            acc_ref[...] += jnp.dot(weights(c), _from_row_tiles(stg, c * MXU_DEPTH, MXU_DEPTH),
                                    preferred_element_type=F32)

    mod = mod_ref[0]
    g2 = mod[:, 5 * D_MODEL:6 * D_MODEL]
    y_ref[...] = _layer_norm(DN_ALPHA * x1_ref[...] + g2 * acc_ref[...], g_ref[...], bb_ref[...])


def _combine(seqs, tile0, n_tiles, s_te, b_te, x_te, nx_t, rows_t, x1, rcol, mod3, ln2_g, ln2_b, eo):
    def mod_map(i, *_):
        return (_tile_pos(seqs, (i + tile0) * TM)[2], 0, 0)

    return pl.pallas_call(
        functools.partial(_combine_kernel, tile0),
        grid_spec=pltpu.PrefetchScalarGridSpec(
            num_scalar_prefetch=5, grid=(n_tiles,),
            in_specs=[pl.BlockSpec((TM, D_MODEL), lambda i, *_: (i + tile0, 0)),
                      pl.BlockSpec((TM, LANES), lambda i, *_: (i + tile0, 0)),
                      pl.BlockSpec((1, 1, 6 * D_MODEL), mod_map),
                      pl.BlockSpec((1, D_MODEL), lambda i, *_: (0, 0)),
                      pl.BlockSpec((1, D_MODEL), lambda i, *_: (0, 0)),
                      pl.BlockSpec(memory_space=pl.ANY)],
            out_specs=pl.BlockSpec((TM, D_MODEL), lambda i, *_: (i, 0)),
            scratch_shapes=[pltpu.VMEM((2, COMB_ROWS // PAIR * ROW_TILES, LANES), U32),
                            pltpu.VMEM((TM, D_MODEL), F32),
                            pltpu.SemaphoreType.DMA((2,))]),
        out_shape=jax.ShapeDtypeStruct((n_tiles * TM, D_MODEL), F32),
        compiler_params=pltpu.CompilerParams(dimension_semantics=("arbitrary",),
                                             vmem_limit_bytes=VMEM_LIMIT_BYTES),
        name="combine",
    )(s_te, b_te, x_te, nx_t, rows_t, x1, rcol, mod3, ln2_g, ln2_b, eo)


def _prep_params(w_in, b_in, w_spatial, b_spatial, sgu_ln_g, sgu_ln_b, w_br_attn, w_br_sgu, w_out,
                 ln1_g, ln1_b, w_router, b_router):
    q_end, k_end, v_end = ATT_W, ATT_W + KV_W, ATT_W + 2 * KV_W

    def dup(w, lo):
        h0, h1 = w[..., lo:lo + HEAD_DIM], w[..., lo + HEAD_DIM:lo + 2 * HEAD_DIM]
        return jnp.concatenate([h0, h0, h1, h1], axis=-1)

    w_kv = jnp.concatenate([dup(w_in, q_end), dup(w_in, k_end)], axis=1).astype(BF16)
    b_kv = jnp.concatenate([dup(b_in, q_end), dup(b_in, k_end)], axis=0).reshape(1, -1)
    w_mix = w_in.astype(BF16)
    b_mix = b_in.reshape(1, -1)
    w_hi = w_router.astype(BF16)
    w_lo = (w_router - w_hi.astype(F32)).astype(BF16)
    p = dict(
        w_mix=w_mix, b_mix=b_mix,
        w_sp=w_spatial.astype(BF16), b_spt=b_spatial.T,
        sgu_g=sgu_ln_g.reshape(1, -1), sgu_b=sgu_ln_b.reshape(1, -1),
        w_ba=w_br_attn.astype(BF16), w_bs=w_br_sgu.astype(BF16), w_out=w_out.astype(BF16),
        ln1_g=ln1_g.reshape(1, -1), ln1_b=ln1_b.reshape(1, -1),
        w_r=jnp.concatenate([w_hi, w_lo], axis=1), b_r=b_router.reshape(1, -1),
    )
    return w_kv, b_kv, p


def _layer(seqs, xp, xs, c_all, w_ada, b_ada, w_in, b_in, sink, sgu_ln_g, sgu_ln_b, w_spatial, b_spatial,
           w_br_attn, w_br_sgu, w_out, ln1_g, ln1_b, w_router, b_router, w_up, b_up, w_down, b_down,
           ln2_g, ln2_b):
    T = seqs.n_tokens
    nt = T // TM
    ntp = seqs.n_prompt // TM
    c_pad = jnp.zeros((8, D_MODEL), F32).at[:c_all.shape[0]].set(c_all)
    mod3 = _ada(c_pad, w_ada, b_ada).reshape(8, 1, 6 * D_MODEL)
    cos_t, sin_t = _rope_tables(max(seqs.prompt_len, seqs.sample_len))
    w_kv, b_kv, p = _prep_params(w_in, b_in, w_spatial, b_spatial, sgu_ln_g, sgu_ln_b, w_br_attn,
                                 w_br_sgu, w_out, ln1_g, ln1_b, w_router, b_router)
    kt, v = _kv(seqs, xp, xs, mod3, cos_t, sin_t, w_kv, b_kv)
    x1, h2, logits = _mixer(seqs, sink, xp, xs, mod3, cos_t, sin_t, kt, v, p)
    rcol, rrow, cnt3 = _router(logits)

    cnt = cnt3[:, :, 0]
    count = cnt.sum(0)
    gb = EXPERT_SUBS * BM
    reg = (count + REGION_SLACK + gb - 1) // gb * gb
    pad_end = jnp.cumsum(reg)
    pad_start = pad_end - reg
    flat = lambda a: a.reshape(-1).astype(I32)
    s_te = flat(pad_start[None, :] + jnp.cumsum(cnt, axis=0) - cnt)
    d_nch = jnp.maximum((cnt + DISP_CHUNK - 1) // DISP_CHUNK, 1)
    c_nch = jnp.maximum((cnt + COMB_CHUNK - 1) // COMB_CHUNK, 1)
    db_te = flat(jnp.cumsum(cnt, axis=1) - cnt)
    cb_te = flat(COMB_CHUNK * (jnp.cumsum(c_nch, axis=1) - c_nch))
    rows_t = flat(COMB_CHUNK * c_nch.sum(1))
    n_blk = (nt * TILE_ROWS + N_EXPERTS * (REGION_SLACK + gb - 1)) // gb + 1
    tails = jnp.concatenate([pad_start + count, pad_end[:-1], jnp.array([n_blk * gb])]).astype(I32)
    blk_e = jnp.minimum(((jnp.arange(n_blk, dtype=I32) * gb)[:, None] >= pad_end[None, :]).sum(1),
                        N_EXPERTS - 1).astype(I32)
    sub_start = jnp.arange(n_blk * EXPERT_SUBS, dtype=I32) * BM
    owner = jnp.repeat(blk_e, EXPERT_SUBS)[:, None] == jnp.arange(N_EXPERTS)[None, :]
    sub_rows = (owner * (pad_start + count)[None, :]).sum(1) - sub_start
    blk_v = jnp.where(sub_start < pad_end[-1], jnp.clip(sub_rows, 0, BM), 0).astype(I32)

    xin = _dispatch(seqs, db_te, s_te, flat(d_nch - 1), flat((d_nch - 1).sum(1)), tails, h2, rrow, n_blk * gb)
    eo = _experts(blk_e, blk_v, xin, w_up, b_up, w_down, b_down, n_blk)
    comb = (s_te, cb_te, flat(c_nch - 1), flat((c_nch - 1).sum(1)), rows_t,
            x1, rcol, mod3, ln2_g.reshape(1, -1), ln2_b.reshape(1, -1), eo)
    return _combine(seqs, 0, ntp, *comb), _combine(seqs, ntp, nt - ntp, *comb)


def kernel(x_prompt, x_sample, c_prompt, c_sample, w_ada, b_ada, w_in, b_in, sink, sgu_ln_g, sgu_ln_b, w_spatial, b_spatial, w_br_attn, w_br_sgu, w_out, ln1_g, ln1_b, w_router, b_router, w_up, b_up, w_down, b_down, ln2_g, ln2_b):
    assert w_ada.shape[0] == DEPTH == 1
    bp, sp, d = x_prompt.shape
    bs, ss, _ = x_sample.shape
    seqs = _Seqs(n_prompt=bp * sp, prompt_len=sp, sample_len=ss, n_tokens=bp * sp + bs * ss)
    c_all = jnp.concatenate([c_prompt, c_sample], axis=0)
    yp, ys = _layer(seqs, x_prompt.reshape(bp * sp, d), x_sample.reshape(bs * ss, d), c_all,
                    w_ada[0], b_ada[0], w_in[0], b_in[0], sink[0], sgu_ln_g[0], sgu_ln_b[0],
                    w_spatial[0], b_spatial[0], w_br_attn[0], w_br_sgu[0], w_out[0], ln1_g[0], ln1_b[0],
                    w_router[0], b_router[0], w_up[0], b_up[0], w_down[0], b_down[0], ln2_g[0], ln2_b[0])
    return (yp.reshape(bp, sp, d), ys.reshape(bs, ss, d))
```
